```python
import jax, jax.numpy as jnp
from jax import lax
import numpy as np

D_MODEL = 1024
BATCH = 2
SEQ = 8192
DEPTH = 2
DEC_BATCH = 16
DEC_SEQ = 32
PAST_LEN = 1024

CHUNK = 64
N_EVEN = (DEPTH + 1) // 2
N_ODD = DEPTH // 2
EPS = 1e-6
GLA_HEADS = 4
GLA_DK = 64
GLA_DV = 128
GLA_LOWRANK = 16
GLA_TAU = 16.0
CONV_CH = 512
CONV_WIDTH = 31
EVEN_SPLIT = (GLA_HEADS * GLA_DK, GLA_HEADS * GLA_DK, GLA_HEADS * GLA_DV, GLA_HEADS * GLA_DV, GLA_LOWRANK, CONV_CH, CONV_CH)
EVEN_IN = 2 * GLA_HEADS * GLA_DK + 2 * GLA_HEADS * GLA_DV + GLA_LOWRANK + 2 * CONV_CH
EVEN_MIX = GLA_HEADS * GLA_DV + CONV_CH
ATT_HEADS = 16
ATT_DH = 64
ATT_WIDTH = ATT_HEADS * ATT_DH
BAND_CHUNKS_PAST = 8
BAND_PAST = BAND_CHUNKS_PAST * CHUNK
BAND_LEN = BAND_PAST + CHUNK
REL_CLIP = 256
REL_SIZE = REL_CLIP + CHUNK
N_GROUPS = 4
EXPERTS_PER_GROUP = 8
N_EXPERTS = N_GROUPS * EXPERTS_PER_GROUP
TOP_K_INNER = 2
D_EXPERT = 512

kernel_name = 'hybrid_stream_gla_conv_band_hmoe_step'


def rmsnorm(x, g):
    x32 = x.astype(jnp.float32)
    y = x32 * lax.rsqrt(jnp.mean(x32 * x32, axis=-1, keepdims=True) + EPS)
    return (y * g.astype(jnp.float32)).astype(x.dtype)


def gla_recurrence(q, k, v, lg, s0):
    B, T, H, _ = q.shape
    blk = CHUNK if T % CHUNK == 0 else T
    nb = T // blk

    def to_blocks(a):
        return a.reshape(B, nb, blk, H, a.shape[-1]).transpose(1, 0, 3, 2, 4)

    causal = jnp.tril(jnp.ones((blk, blk), bool))[None, None, :, :, None]

    def step(S, xs):
        qb, kb, vb, lb = xs
        cum = jnp.cumsum(lb, axis=2)
        o_inter = jnp.einsum('bhcd,bhde->bhce', qb * jnp.exp(cum), S)
        diff = cum[:, :, :, None, :] - cum[:, :, None, :, :]
        decay = jnp.exp(jnp.where(causal, diff, -jnp.inf))
        scores = jnp.einsum('bhid,bhjd,bhijd->bhij', qb, kb, decay)
        o = o_inter + jnp.einsum('bhij,bhje->bhie', scores, vb)
        last = cum[:, :, -1, :]
        S = jnp.exp(last)[..., None] * S + jnp.einsum('bhcd,bhce->bhde', kb * jnp.exp(last[:, :, None, :] - cum), vb)
        return S, o

    s_fin, o = lax.scan(step, s0, (to_blocks(q), to_blocks(k), to_blocks(v), to_blocks(lg)))
    o = o.transpose(1, 0, 3, 2, 4).reshape(B, T, H, v.shape[-1])
    return o, s_fin


def even_mixer(h, s0, conv_buf, w_in, w_alpha_up, b_alpha, gla_norm_g, conv_w, conv_b, conv_ln_g, conv_ln_b, w_out):
    B, T, _ = h.shape
    f32 = jnp.float32
    proj = h @ w_in
    offsets = [int(o) for o in np.cumsum(EVEN_SPLIT)[:-1]]
    q, k, v, gate, a_lr, c_val, c_gate = jnp.split(proj, offsets, axis=-1)
    lg = jax.nn.log_sigmoid((a_lr @ w_alpha_up + b_alpha).astype(f32)) / GLA_TAU
    o, s_new = gla_recurrence(
        q.astype(f32).reshape(B, T, GLA_HEADS, GLA_DK) * (GLA_DK ** -0.5),
        k.astype(f32).reshape(B, T, GLA_HEADS, GLA_DK),
        v.astype(f32).reshape(B, T, GLA_HEADS, GLA_DV),
        lg.reshape(B, T, GLA_HEADS, GLA_DK),
        s0.astype(f32))
    o = o * lax.rsqrt(jnp.mean(o * o, axis=-1, keepdims=True) + EPS)
    o = o.reshape(B, T, GLA_HEADS * GLA_DV) * gla_norm_g.astype(f32) * jax.nn.silu(gate.astype(f32))
    u = c_val * jax.nn.sigmoid(c_gate)
    xc = jnp.concatenate([conv_buf.astype(u.dtype), u], axis=1)
    c = lax.conv_general_dilated(xc, conv_w[:, None, :].astype(xc.dtype), (1,), 'VALID',
                                 dimension_numbers=('NWC', 'WIO', 'NWC'), feature_group_count=CONV_CH)
    c32 = (c + conv_b).astype(f32)
    mu = jnp.mean(c32, axis=-1, keepdims=True)
    var = jnp.mean(jnp.square(c32 - mu), axis=-1, keepdims=True)
    cn = jax.nn.silu((c32 - mu) * lax.rsqrt(var + EPS) * conv_ln_g.astype(f32) + conv_ln_b.astype(f32))
    mix = jnp.concatenate([o, cn], axis=-1).astype(h.dtype) @ w_out
    return mix, s_new.astype(h.dtype), xc[:, -(CONV_WIDTH - 1):]


def band_attend(q, k, v, q_pos, k_pos, rel_bias):
    rel = jnp.clip(q_pos[:, None] - k_pos[None, :], -(CHUNK - 1), REL_CLIP) + (CHUNK - 1)
    bias = rel_bias[:, rel].astype(jnp.float32)
    qc = q_pos // CHUNK
    kc = k_pos // CHUNK
    allowed = (k_pos[None, :] >= 0) & (kc[None, :] <= qc[:, None]) & (kc[None, :] >= qc[:, None] - BAND_CHUNKS_PAST)
    s = jnp.einsum('bqhd,bkhd->bhqk', q, k).astype(jnp.float32) * (ATT_DH ** -0.5) + bias[None]
    s = jnp.where(allowed[None, None], s, -1e30)
    p = jax.nn.softmax(s, axis=-1)
    return jnp.einsum('bhqk,bkhd->bqhd', p.astype(v.dtype), v)


def odd_mixer_prompt(h, w_qkv, rel_bias, w_out):
    B, T, _ = h.shape
    qkv = (h @ w_qkv).reshape(B, T, 3, ATT_HEADS, ATT_DH)
    q, k, v = qkv[:, :, 0], qkv[:, :, 1], qkv[:, :, 2]
    nc = T // CHUNK
    pad = ((0, 0), (BAND_PAST, 0), (0, 0), (0, 0))
    k_pad = jnp.pad(k, pad)
    v_pad = jnp.pad(v, pad)
    q_blocks = q.reshape(B, nc, CHUNK, ATT_HEADS, ATT_DH).transpose(1, 0, 2, 3, 4)

    def one_chunk(args):
        n, qb = args
        start = n * CHUNK
        kb = lax.dynamic_slice_in_dim(k_pad, start, BAND_LEN, axis=1)
        vb = lax.dynamic_slice_in_dim(v_pad, start, BAND_LEN, axis=1)
        return band_attend(qb, kb, vb, start + jnp.arange(CHUNK), start - BAND_PAST + jnp.arange(BAND_LEN), rel_bias)

    o = lax.map(one_chunk, (jnp.arange(nc), q_blocks))
    o = o.transpose(1, 0, 2, 3, 4).reshape(B, T, ATT_WIDTH)
    rows = min(BAND_PAST, T)
    return o @ w_out, k[:, T - rows:], v[:, T - rows:]


def odd_mixer_sample(h, cache_k, cache_v, w_qkv, rel_bias, w_out):
    B, S, _ = h.shape
    R = cache_k.shape[1]
    qkv = (h @ w_qkv).reshape(B, S, 3, ATT_HEADS, ATT_DH)
    q, k, v = qkv[:, :, 0], qkv[:, :, 1], qkv[:, :, 2]
    k_all = jnp.concatenate([cache_k.astype(k.dtype), k], axis=1)
    v_all = jnp.concatenate([cache_v.astype(v.dtype), v], axis=1)
    q_pos = PAST_LEN + jnp.arange(S)
    k_pos = PAST_LEN - R + jnp.arange(R + S)
    o = band_attend(q, k_all, v_all, q_pos, k_pos, rel_bias).reshape(B, S, ATT_WIDTH)
    return o @ w_out, k, v


def hier_moe(h, w_rg, b_rg, w_re, b_re, w_gate, w_up, w_down):
    shp = h.shape
    f32 = jnp.float32
    x = h.reshape(-1, D_MODEL)
    n = x.shape[0]
    g_logits = (x @ w_rg + b_rg).astype(f32)
    g_prob = jax.nn.softmax(g_logits, axis=-1)
    g_idx = jnp.argmax(g_logits, axis=-1)
    g_w = jnp.max(g_prob, axis=-1, keepdims=True)
    e_logits = (jnp.einsum('nd,dge->nge', x, w_re) + b_re).astype(f32)
    e_sel = jnp.einsum('nge,ng->ne', e_logits, jax.nn.one_hot(g_idx, N_GROUPS, dtype=f32))
    top_v, top_i = lax.top_k(e_sel, TOP_K_INNER)
    e_w = jax.nn.softmax(top_v, axis=-1) * g_w
    ids = g_idx[:, None] * EXPERTS_PER_GROUP + top_i
    combine = jnp.einsum('nk,nke->ne', e_w, jax.nn.one_hot(ids, N_EXPERTS, dtype=f32))
    y = jnp.zeros((n, D_MODEL), f32)
    for e in range(N_EXPERTS):
        he = jax.nn.silu(x @ w_gate[e]) * (x @ w_up[e])
        y = y + combine[:, e:e + 1] * (he @ w_down[e]).astype(f32)
    return y.astype(h.dtype).reshape(shp)


def setup_inputs(seed: int = 0) -> dict:
    key = jax.random.key(seed)
    keys = iter(jax.random.split(key, 40))

    def nrm(shape, scale):
        return jax.random.normal(next(keys), shape, jnp.float32) * scale

    R = min(BAND_PAST, PAST_LEN)
    return {
        'x_prompt': nrm((BATCH, SEQ, D_MODEL), 1.0),
        'x_sample': nrm((DEC_BATCH, DEC_SEQ, D_MODEL), 1.0),
        'state_gla': nrm((N_EVEN, DEC_BATCH, GLA_HEADS, GLA_DK, GLA_DV), 1.0),
        'state_conv': nrm((N_EVEN, DEC_BATCH, CONV_WIDTH - 1, CONV_CH), 0.5),
        'cache_band_k': nrm((N_ODD, DEC_BATCH, R, ATT_HEADS, ATT_DH), 1.0),
        'cache_band_v': nrm((N_ODD, DEC_BATCH, R, ATT_HEADS, ATT_DH), 1.0),
        'norm_mix_g': 1.0 + nrm((DEPTH, D_MODEL), 0.02),
        'norm_ffn_g': 1.0 + nrm((DEPTH, D_MODEL), 0.02),
        'norm_final_g': 1.0 + nrm((D_MODEL,), 0.02),
        'w_in_even': nrm((N_EVEN, D_MODEL, EVEN_IN), D_MODEL ** -0.5),
        'w_alpha_up': nrm((N_EVEN, GLA_LOWRANK, GLA_HEADS * GLA_DK), GLA_LOWRANK ** -0.5),
        'b_alpha': nrm((N_EVEN, GLA_HEADS * GLA_DK), 0.1),
        'gla_norm_g': 1.0 + nrm((N_EVEN, GLA_HEADS * GLA_DV), 0.02),
        'conv_w': nrm((N_EVEN, CONV_WIDTH, CONV_CH), CONV_WIDTH ** -0.5),
        'conv_b': nrm((N_EVEN, CONV_CH), 0.02),
        'conv_ln_g': 1.0 + nrm((N_EVEN, CONV_CH), 0.02),
        'conv_ln_b': nrm((N_EVEN, CONV_CH), 0.02),
        'w_out_even': nrm((N_EVEN, EVEN_MIX, D_MODEL), EVEN_MIX ** -0.5),
        'w_qkv_odd': nrm((N_ODD, D_MODEL, 3 * ATT_WIDTH), D_MODEL ** -0.5),
        'rel_bias': nrm((N_ODD, ATT_HEADS, REL_SIZE), 0.1),
        'w_out_odd': nrm((N_ODD, ATT_WIDTH, D_MODEL), ATT_WIDTH ** -0.5),
        'w_router_grp': nrm((DEPTH, D_MODEL, N_GROUPS), D_MODEL ** -0.5),
        'b_router_grp': nrm((DEPTH, N_GROUPS), 0.01),
        'w_router_exp': nrm((DEPTH, D_MODEL, N_GROUPS, EXPERTS_PER_GROUP), D_MODEL ** -0.5),
        'b_router_exp': nrm((DEPTH, N_GROUPS, EXPERTS_PER_GROUP), 0.01),
        'w_exp_gate': nrm((DEPTH, N_EXPERTS, D_MODEL, D_EXPERT), D_MODEL ** -0.5),
        'w_exp_up': nrm((DEPTH, N_EXPERTS, D_MODEL, D_EXPERT), D_MODEL ** -0.5),
        'w_exp_down': nrm((DEPTH, N_EXPERTS, D_EXPERT, D_MODEL), D_EXPERT ** -0.5),
    }


def reference(x_prompt, x_sample, state_gla, state_conv, cache_band_k, cache_band_v,
              norm_mix_g, norm_ffn_g, norm_final_g,
              w_in_even, w_alpha_up, b_alpha, gla_norm_g, conv_w, conv_b, conv_ln_g, conv_ln_b, w_out_even,
              w_qkv_odd, rel_bias, w_out_odd,
              w_router_grp, b_router_grp, w_router_exp, b_router_exp, w_exp_gate, w_exp_up, w_exp_down):
    hp, hs = x_prompt, x_sample
    B = hp.shape[0]
    gla_p, gla_s, conv_p, conv_s = [], [], [], []
    kp, vp, ks, vs = [], [], [], []
    for layer in range(DEPTH):
        i = layer // 2
        nprm = rmsnorm(hp, norm_mix_g[layer])
        nsmp = rmsnorm(hs, norm_mix_g[layer])
        if layer % 2 == 0:
            ew = (w_in_even[i], w_alpha_up[i], b_alpha[i], gla_norm_g[i], conv_w[i], conv_b[i],
                  conv_ln_g[i], conv_ln_b[i], w_out_even[i])
            s0 = jnp.zeros((B, GLA_HEADS, GLA_DK, GLA_DV), jnp.float32)
            b0 = jnp.zeros((B, CONV_WIDTH - 1, CONV_CH), hp.dtype)
            mp, sp, cp = even_mixer(nprm, s0, b0, *ew)
            ms, ss, cs = even_mixer(nsmp, state_gla[i], state_conv[i], *ew)
            gla_p.append(sp)
            gla_s.append(ss)
            conv_p.append(cp)
            conv_s.append(cs)
        else:
            mp, kpi, vpi = odd_mixer_prompt(nprm, w_qkv_odd[i], rel_bias[i], w_out_odd[i])
            ms, ksi, vsi = odd_mixer_sample(nsmp, cache_band_k[i], cache_band_v[i], w_qkv_odd[i], rel_bias[i], w_out_odd[i])
            kp.append(kpi)
            vp.append(vpi)
            ks.append(ksi)
            vs.append(vsi)
        hp = hp + mp
        hs = hs + ms
        mw = (w_router_grp[layer], b_router_grp[layer], w_router_exp[layer], b_router_exp[layer],
              w_exp_gate[layer], w_exp_up[layer], w_exp_down[layer])
        hp = hp + hier_moe(rmsnorm(hp, norm_ffn_g[layer]), *mw)
        hs = hs + hier_moe(rmsnorm(hs, norm_ffn_g[layer]), *mw)
    y_prompt = rmsnorm(hp, norm_final_g)
    y_sample = rmsnorm(hs, norm_final_g)
    return (y_prompt, y_sample, jnp.stack(gla_p), jnp.stack(gla_s), jnp.stack(conv_p), jnp.stack(conv_s),
            jnp.stack(kp), jnp.stack(vp), jnp.stack(ks), jnp.stack(vs))
```

```python
import functools

import jax
import jax.numpy as jnp
from jax import lax
from jax.experimental import pallas as pl
from jax.experimental.pallas import tpu as pltpu

F32 = jnp.float32
BF16 = jnp.bfloat16
HIGHEST = lax.Precision.HIGHEST

D_MODEL = 1024
CHUNK = 64
EPS = 1e-6
GLA_HEADS = 4
GLA_DK = 64
GLA_DV = 128
GLA_LOWRANK = 16
GLA_TAU = 16.0
CONV_CH = 512
CONV_WIDTH = 31
HIST = CONV_WIDTH - 1
ATT_HEADS = 16
ATT_DH = 64
BAND_CHUNKS_PAST = 8
BAND_PAST = BAND_CHUNKS_PAST * CHUNK
REL_CLIP = 256
N_GROUPS = 4
EXPERTS_PER_GROUP = 8
N_EXPERTS = N_GROUPS * EXPERTS_PER_GROUP
D_EXPERT = 512

LANES = 128
ROW_BLOCK = 512
GMM_TILE = 256
ATT_QB = 256
ATT_WIN = ATT_QB + BAND_PAST
NEG_BIG = -1e30
VMEM_LIMIT = 56 * 1024 * 1024

QK_W = GLA_HEADS * GLA_DK
V_W = GLA_HEADS * GLA_DV
COL_Q = 0
COL_K = COL_Q + QK_W
COL_V = COL_K + QK_W
COL_GATE = COL_V + V_W
COL_CVAL = COL_GATE + V_W
COL_CGATE = COL_CVAL + CONV_CH
COL_ALR = COL_CGATE + CONV_CH
EVEN_COLS = COL_ALR + LANES


def _rmsnorm(x, g):
    return x * lax.rsqrt(jnp.mean(x * x, axis=-1, keepdims=True) + EPS) * g


def _sigmoid(x):
    return 1.0 / (1.0 + jnp.exp(-x))


def _dot(a, b):
    return jnp.dot(a, b, preferred_element_type=F32)


def _dot_nt(a, b):
    return lax.dot_general(a, b, (((1,), (1,)), ((), ())), preferred_element_type=F32)


def _dot_tn(a, b, precision=None):
    return lax.dot_general(a, b, (((0,), (0,)), ((), ())), preferred_element_type=F32, precision=precision)


def _even_mixer_kernel(*refs, chunk, cps, spb, carry, aliased):
    (x_ref, g_ref, win_ref, wau_ref, bal_ref, gng_ref, cw_ref, cb_ref, lng_ref, lnb_ref, wout_ref,
     s0_ref, c0_ref) = refs[:13]
    refs = refs[13 + (1 if aliased else 0):]
    h_ref, sfin_ref, cfin_ref, proj_ref, lg_ref, cum_ref, mix_ref, s_ref, ubuf_ref, win_scr = refs
    j = pl.program_id(1)
    nj = pl.num_programs(1)
    seg = cps * chunk

    x = x_ref[...]
    xn = _rmsnorm(x, g_ref[...])
    proj_ref[...] = _dot(xn.astype(BF16), win_ref[...])
    alr = proj_ref[:, COL_ALR:COL_ALR + LANES]
    xa = _dot(alr.astype(BF16), wau_ref[...]) + bal_ref[...]
    lg_ref[...] = (jnp.minimum(xa, 0.0) - jnp.log1p(jnp.exp(-jnp.abs(xa)))) * (1.0 / GLA_TAU)

    row_i = lax.broadcasted_iota(jnp.int32, (chunk, QK_W), 0)
    tri_r = lax.broadcasted_iota(jnp.int32, (chunk, chunk), 0)
    tri_c = lax.broadcasted_iota(jnp.int32, (chunk, chunk), 1)
    causal = tri_r >= tri_c
    ones_cols = jnp.ones((chunk, LANES), F32)

    def chunk_body(ci, c):
        r0 = pl.multiple_of(ci * chunk, chunk)
        sq = ci // cps if spb > 1 else 0
        first = (ci % cps) == 0
        last = (ci % cps) == (cps - 1)
        if carry:
            first = jnp.logical_and(first, j == 0)
            last = jnp.logical_and(last, j == nj - 1)

        @pl.when(first)
        def _():
            s_ref[...] = s0_ref[sq]

        lg = lg_ref[pl.ds(r0, chunk), :]
        cum = lg
        shift = 1
        while shift < chunk:
            cum = cum + jnp.where(row_i >= shift, pltpu.roll(cum, shift, 0), 0.0)
            shift *= 2
        cum_ref[...] = cum
        tot = cum_ref[pl.ds(chunk - 1, 1), :]
        mid = cum_ref[pl.ds(chunk // 2 - 1, 1), :]
        tot_col = jnp.exp(_dot_tn(lg, ones_cols, precision=HIGHEST))
        e_in = jnp.exp(cum)
        e_q = jnp.exp(cum - mid)
        e_k = jnp.exp(mid - cum)
        e_s = jnp.exp(tot - cum)
        for h in range(GLA_HEADS):
            ks = slice(h * GLA_DK, (h + 1) * GLA_DK)
            vs = slice(h * GLA_DV, (h + 1) * GLA_DV)
            q = proj_ref[pl.ds(r0, chunk), COL_Q + h * GLA_DK:COL_Q + (h + 1) * GLA_DK] * (GLA_DK ** -0.5)
            k = proj_ref[pl.ds(r0, chunk), COL_K + h * GLA_DK:COL_K + (h + 1) * GLA_DK]
            v = proj_ref[pl.ds(r0, chunk), COL_V + h * GLA_DV:COL_V + (h + 1) * GLA_DV].astype(BF16)
            gate = proj_ref[pl.ds(r0, chunk), COL_GATE + h * GLA_DV:COL_GATE + (h + 1) * GLA_DV]
            s_old = s_ref[h]
            scores = _dot_nt((q * e_q[:, ks]).astype(BF16), (k * e_k[:, ks]).astype(BF16))
            scores = jnp.where(causal, scores, 0.0)
            o = _dot((q * e_in[:, ks]).astype(BF16), s_old.astype(BF16)) + _dot(scores.astype(BF16), v)
            s_ref[h] = tot_col[h * GLA_DK:(h + 1) * GLA_DK, :] * s_old + _dot_tn((k * e_s[:, ks]).astype(BF16), v)
            o = o * lax.rsqrt(jnp.mean(o * o, axis=-1, keepdims=True) + EPS)
            o = o * gng_ref[:, vs] * (gate * _sigmoid(gate))
            mix_ref[pl.ds(r0, chunk), vs] = o.astype(BF16)

        @pl.when(last)
        def _():
            sfin_ref[sq] = s_ref[...]

        return c

    lax.fori_loop(0, spb * cps, chunk_body, 0)

    tile = min(seg, 64)

    def conv_tile(src_ref, base, out_r0):
        acc = jnp.zeros((tile, CONV_CH), F32)
        for w in range(CONV_WIDTH):
            acc = acc + src_ref[pl.ds(base + 2 + w, tile), :] * cw_ref[pl.ds(w, 1), :]
        cv = acc + cb_ref[...]
        mu = jnp.mean(cv, axis=-1, keepdims=True)
        var = jnp.mean(jnp.square(cv - mu), axis=-1, keepdims=True)
        y = (cv - mu) * lax.rsqrt(var + EPS) * lng_ref[...] + lnb_ref[...]
        mix_ref[pl.ds(out_r0, tile), V_W:V_W + CONV_CH] = (y * _sigmoid(y)).astype(BF16)

    def conv_seg(sq, c):
        r0 = pl.multiple_of(sq * seg, seg) if spb > 1 else 0
        cval = proj_ref[pl.ds(r0, seg), COL_CVAL:COL_CVAL + CONV_CH]
        cgate = proj_ref[pl.ds(r0, seg), COL_CGATE:COL_CGATE + CONV_CH]

        def load_history():
            ubuf_ref[pl.ds(0, 8), :] = jnp.zeros((8, CONV_CH), F32)
            ubuf_ref[pl.ds(2, HIST), :] = c0_ref[sq]

        if carry:
            pl.when(j == 0)(load_history)
        else:
            load_history()
        ubuf_ref[pl.ds(32, seg), :] = cval * _sigmoid(cgate)
        if seg == tile:
            conv_tile(ubuf_ref, 0, r0)
        else:
            def tile_body(t, cc):
                t0 = pl.multiple_of(t * tile, tile)
                win_scr[...] = ubuf_ref[pl.ds(t0, tile + 32), :]
                conv_tile(win_scr, 0, r0 + t0)
                return cc
            lax.fori_loop(0, seg // tile, tile_body, 0)
        hist = ubuf_ref[pl.ds(seg + 2, HIST), :]
        if carry:
            @pl.when(j == nj - 1)
            def _():
                cfin_ref[sq] = hist
            ubuf_ref[pl.ds(2, HIST), :] = hist
        else:
            cfin_ref[sq] = hist
        return c

    if spb > 1:
        lax.fori_loop(0, spb, conv_seg, 0)
    else:
        conv_seg(0, 0)

    h_ref[...] = _dot(mix_ref[...], wout_ref[...]) + x


def _even_mixer(x2d, hbuf, weights, s0, c0, *, n_rows_total, block0, n_seq, blocks_per_seq, chunk, cps, spb):
    carry = spb == 1
    aliased = hbuf is not None
    const = lambda s, j: (0, 0)
    seq_blk = (lambda s, j: (s, 0, 0, 0)) if carry else (lambda s, j: (0, 0, 0, 0))
    seq_blk3 = (lambda s, j: (s, 0, 0)) if carry else (lambda s, j: (0, 0, 0))
    n_state = 1 if carry else spb
    wspec = lambda shape: pl.BlockSpec(shape, const, pipeline_mode=pl.Buffered(1))
    in_specs = [
        pl.BlockSpec((ROW_BLOCK, D_MODEL), lambda s, j: (s * blocks_per_seq + j, 0)),
        wspec((1, D_MODEL)),
        wspec((D_MODEL, EVEN_COLS)),
        wspec((LANES, QK_W)),
        wspec((1, QK_W)),
        wspec((1, V_W)),
        wspec((CONV_WIDTH, CONV_CH)),
        wspec((1, CONV_CH)),
        wspec((1, CONV_CH)),
        wspec((1, CONV_CH)),
        wspec((V_W + CONV_CH, D_MODEL)),
        pl.BlockSpec((n_state, GLA_HEADS, GLA_DK, GLA_DV), seq_blk),
        pl.BlockSpec((n_state, HIST, CONV_CH), seq_blk3),
    ]
    args = [x2d, *weights, s0, c0]
    aliases = {}
    if aliased:
        in_specs.append(pl.BlockSpec(memory_space=pl.ANY))
        args.append(hbuf)
        aliases = {len(args) - 1: 0}
    n_all = s0.shape[0]
    out_shape = (
        jax.ShapeDtypeStruct((n_rows_total, D_MODEL), F32),
        jax.ShapeDtypeStruct((n_all, GLA_HEADS, GLA_DK, GLA_DV), F32),
        jax.ShapeDtypeStruct((n_all, HIST, CONV_CH), F32),
    )
    out_specs = (
        pl.BlockSpec((ROW_BLOCK, D_MODEL), lambda s, j: (block0 + s * blocks_per_seq + j, 0)),
        pl.BlockSpec((n_state, GLA_HEADS, GLA_DK, GLA_DV), seq_blk),
        pl.BlockSpec((n_state, HIST, CONV_CH), seq_blk3),
    )
    seg = cps * chunk
    scratch = [
        pltpu.VMEM((ROW_BLOCK, EVEN_COLS), F32),
        pltpu.VMEM((ROW_BLOCK, QK_W), F32),
        pltpu.VMEM((chunk, QK_W), F32),
        pltpu.VMEM((ROW_BLOCK, V_W + CONV_CH), BF16),
        pltpu.VMEM((GLA_HEADS, GLA_DK, GLA_DV), F32),
        pltpu.VMEM((32 + seg, CONV_CH), F32),
        pltpu.VMEM((min(seg, 64) + 32, CONV_CH), F32),
    ]
    kern = functools.partial(_even_mixer_kernel, chunk=chunk, cps=cps, spb=spb, carry=carry, aliased=aliased)
    return pl.pallas_call(
        kern, out_shape=out_shape, grid=(n_seq, blocks_per_seq), in_specs=in_specs, out_specs=out_specs,
        scratch_shapes=scratch, input_output_aliases=aliases,
        compiler_params=pltpu.CompilerParams(dimension_semantics=("arbitrary", "arbitrary"), vmem_limit_bytes=VMEM_LIMIT),
        name="even_mixer_carry" if carry else "even_mixer_step",
    )(*args)


def _route_kernel(h_ref, g_ref, wr_ref, br_ref, xn_ref, mi_ref, mw_ref, cnt_ref, base_ref):
    i = pl.program_id(0)

    @pl.when(i == 0)
    def _():
        base_ref[...] = jnp.zeros_like(base_ref)

    xn = _rmsnorm(h_ref[...], g_ref[...])
    xn_ref[...] = xn
    logits = jnp.dot(xn, wr_ref[...], precision=HIGHEST, preferred_element_type=F32) + br_ref[...]
    lane = lax.broadcasted_iota(jnp.int32, logits.shape, 1).astype(F32)
    far = float(1 << 20)
    gl = jnp.where(lane < N_GROUPS, logits, -jnp.inf)
    gmax = jnp.max(gl, axis=-1, keepdims=True)
    gidx = jnp.min(jnp.where(gl == gmax, lane, far), axis=-1, keepdims=True)
    gw = 1.0 / jnp.sum(jnp.exp(gl - gmax), axis=-1, keepdims=True)
    lo = N_GROUPS + gidx * EXPERTS_PER_GROUP
    el = jnp.where(jnp.logical_and(lane >= lo, lane < lo + EXPERTS_PER_GROUP), logits, -jnp.inf)
    m1 = jnp.max(el, axis=-1, keepdims=True)
    i1 = jnp.min(jnp.where(el == m1, lane, far), axis=-1, keepdims=True)
    el2 = jnp.where(lane == i1, -jnp.inf, el)
    m2 = jnp.max(el2, axis=-1, keepdims=True)
    i2 = jnp.min(jnp.where(el2 == m2, lane, far), axis=-1, keepdims=True)
    e2 = jnp.exp(m2 - m1)
    den = 1.0 + e2
    w0 = (1.0 / den) * gw
    w1 = (e2 / den) * gw
    id0 = i1 - N_GROUPS
    id1 = i2 - N_GROUPS
    oh0 = jnp.where(lane == id0, 1.0, 0.0)
    oh1 = jnp.where(lane == id1, 1.0, 0.0)
    rr = lax.broadcasted_iota(jnp.int32, (ROW_BLOCK, ROW_BLOCK), 0)
    cc = lax.broadcasted_iota(jnp.int32, (ROW_BLOCK, ROW_BLOCK), 1)
    below = jnp.where(rr > cc, 1.0, 0.0).astype(BF16)
    p0 = _dot(below, oh0.astype(BF16))
    p1 = _dot(below, oh1.astype(BF16))
    cnt0 = jnp.sum(oh0, axis=0, keepdims=True)
    cnt1 = jnp.sum(oh1, axis=0, keepdims=True)
    base = base_ref[...]
    rank0 = jnp.sum(oh0 * (p0 + base), axis=-1, keepdims=True)
    rank1 = jnp.sum(oh1 * (p1 + base + cnt0), axis=-1, keepdims=True)
    new_base = base + cnt0 + cnt1
    base_ref[...] = new_base
    cnt_ref[...] = new_base.astype(jnp.int32)
    meta = jnp.where(lane == 0, id0, jnp.where(lane == 1, id1, jnp.where(lane == 2, rank0, jnp.where(lane == 3, rank1, 0.0))))
    mi_ref[0] = meta.T[0:8, :].astype(jnp.int32)
    mw_ref[...] = jnp.where(lane == 0, w0, jnp.where(lane == 1, w1, 0.0))


def _route(h, g, wr, br, n_blocks):
    n_rows = n_blocks * ROW_BLOCK
    const = lambda i: (0, 0)
    return pl.pallas_call(
        _route_kernel,
        out_shape=(
            jax.ShapeDtypeStruct((n_rows, D_MODEL), F32),
            jax.ShapeDtypeStruct((n_blocks, 8, ROW_BLOCK), jnp.int32),
            jax.ShapeDtypeStruct((n_rows, LANES), F32),
            jax.ShapeDtypeStruct((1, LANES), jnp.int32),
        ),
        grid=(n_blocks,),
        in_specs=[
            pl.BlockSpec((ROW_BLOCK, D_MODEL), lambda i: (i, 0)),
            pl.BlockSpec((1, D_MODEL), const),
            pl.BlockSpec((D_MODEL, LANES), const),
            pl.BlockSpec((1, LANES), const),
        ],
        out_specs=(
            pl.BlockSpec((ROW_BLOCK, D_MODEL), lambda i: (i, 0)),
            pl.BlockSpec((1, 8, ROW_BLOCK), lambda i: (i, 0, 0)),
            pl.BlockSpec((ROW_BLOCK, LANES), lambda i: (i, 0)),
            pl.BlockSpec((1, LANES), const),
        ),
        scratch_shapes=[pltpu.VMEM((1, LANES), F32)],
        compiler_params=pltpu.CompilerParams(dimension_semantics=("arbitrary",), vmem_limit_bytes=VMEM_LIMIT),
        name="moe_route",
    )(h, g, wr, br)


def _row_copy_out(x_ref, xs_ref, sem, r, d):
    return pltpu.make_async_copy(x_ref.at[pl.ds(r, 1), :], xs_ref.at[pl.ds(d, 1), :], sem)


def _dispatch_kernel(dest_ref, x_ref, xs_in_ref, xs_ref, sem):
    del xs_in_ref

    def body(r, c):
        for slot in range(2):
            _row_copy_out(x_ref, xs_ref, sem, r, dest_ref[0, slot, r]).start()
        return c

    lax.fori_loop(0, ROW_BLOCK, body, 0, unroll=8)

    def drain(r, c):
        for slot in range(2):
            _row_copy_out(x_ref, xs_ref, sem, r, dest_ref[0, slot, r]).wait()
        return c

    lax.fori_loop(0, ROW_BLOCK, drain, 0, unroll=8)


def _dispatch(dest, xn, xs_zero, n_blocks):
    return pl.pallas_call(
        _dispatch_kernel,
        out_shape=jax.ShapeDtypeStruct(xs_zero.shape, F32),
        grid=(n_blocks,),
        in_specs=[
            pl.BlockSpec((1, 2, ROW_BLOCK), lambda i: (i, 0, 0), memory_space=pltpu.SMEM),
            pl.BlockSpec((ROW_BLOCK, D_MODEL), lambda i: (i, 0)),
            pl.BlockSpec(memory_space=pl.ANY),
        ],
        out_specs=pl.BlockSpec(memory_space=pl.ANY),
        scratch_shapes=[pltpu.SemaphoreType.DMA(())],
        input_output_aliases={2: 0},
        compiler_params=pltpu.CompilerParams(dimension_semantics=("arbitrary",), vmem_limit_bytes=VMEM_LIMIT),
        name="moe_dispatch",
    )(dest, xn, xs_zero)


def _gmm_kernel(te_ref, nv_ref, xs_ref, wg_ref, wu_ref, wd_ref, ys_ref, wgb, wub, wdb):
    i = pl.program_id(0)

    @pl.when(i < nv_ref[0])
    def _():
        prev = te_ref[jnp.maximum(i - 1, 0)]

        @pl.when(jnp.logical_or(i == 0, te_ref[i] != prev))
        def _():
            wgb[...] = wg_ref[0].astype(BF16)
            wub[...] = wu_ref[0].astype(BF16)
            wdb[...] = wd_ref[0].astype(BF16)

        x = xs_ref[...].astype(BF16)
        a = _dot(x, wgb[...])
        b = _dot(x, wub[...])
        he = (a * _sigmoid(a)) * b
        ys_ref[...] = _dot(he.astype(BF16), wdb[...])


def _gmm(tile_expert, n_valid, xs, w_gate, w_up, w_down):
    n_tiles = xs.shape[0] // GMM_TILE

    def tile_idx(i, te, nv):
        return jnp.minimum(i, nv[0] - 1)

    return pl.pallas_call(
        _gmm_kernel,
        out_shape=jax.ShapeDtypeStruct((xs.shape[0], D_MODEL), F32),
        grid_spec=pltpu.PrefetchScalarGridSpec(
            num_scalar_prefetch=2,
            grid=(n_tiles,),
            in_specs=[
                pl.BlockSpec((GMM_TILE, D_MODEL), lambda i, te, nv: (tile_idx(i, te, nv), 0)),
                pl.BlockSpec((1, D_MODEL, D_EXPERT), lambda i, te, nv: (te[tile_idx(i, te, nv)], 0, 0)),
                pl.BlockSpec((1, D_MODEL, D_EXPERT), lambda i, te, nv: (te[tile_idx(i, te, nv)], 0, 0)),
                pl.BlockSpec((1, D_EXPERT, D_MODEL), lambda i, te, nv: (te[tile_idx(i, te, nv)], 0, 0)),
            ],
            out_specs=pl.BlockSpec((GMM_TILE, D_MODEL), lambda i, te, nv: (tile_idx(i, te, nv), 0)),
            scratch_shapes=[
                pltpu.VMEM((D_MODEL, D_EXPERT), BF16),
                pltpu.VMEM((D_MODEL, D_EXPERT), BF16),
                pltpu.VMEM((D_EXPERT, D_MODEL), BF16),
            ],
        ),
        compiler_params=pltpu.CompilerParams(dimension_semantics=("arbitrary",), vmem_limit_bytes=VMEM_LIMIT),
        name="moe_gmm",
    )(tile_expert, n_valid, xs, w_gate, w_up, w_down)


def _row_copy_in(ys_ref, gbuf, sem, slot, r, d):
    return pltpu.make_async_copy(ys_ref.at[pl.ds(d, 1), :], gbuf.at[slot, pl.ds(r, 1), :], sem)


def _combine_kernel(*refs, final):
    if final:
        dest_ref, h_ref, mw_ref, gfin_ref, ys_ref, out_ref, gbuf, sem = refs
    else:
        dest_ref, h_ref, mw_ref, ys_ref, out_ref, gbuf, sem = refs

    def body(r, c):
        for slot in range(2):
            _row_copy_in(ys_ref, gbuf, sem, slot, r, dest_ref[0, slot, r]).start()
        return c

    lax.fori_loop(0, ROW_BLOCK, body, 0, unroll=8)

    def drain(r, c):
        for slot in range(2):
            _row_copy_in(ys_ref, gbuf, sem, slot, r, dest_ref[0, slot, r]).wait()
        return c

    lax.fori_loop(0, ROW_BLOCK, drain, 0, unroll=8)

    mw = mw_ref[...]
    hn = h_ref[...] + (mw[:, 0:1] * gbuf[0] + mw[:, 1:2] * gbuf[1])
    out_ref[...] = _rmsnorm(hn, gfin_ref[...]) if final else hn


def _combine(dest, h, mw, ys, gfin, *, block0, n_blocks):
    final = gfin is not None
    in_specs = [
        pl.BlockSpec((1, 2, ROW_BLOCK), lambda i: (block0 + i, 0, 0), memory_space=pltpu.SMEM),
        pl.BlockSpec((ROW_BLOCK, D_MODEL), lambda i: (block0 + i, 0)),
        pl.BlockSpec((ROW_BLOCK, LANES), lambda i: (block0 + i, 0)),
    ]
    args = [dest, h, mw]
    if final:
        in_specs.append(pl.BlockSpec((1, D_MODEL), lambda i: (0, 0)))
        args.append(gfin)
    in_specs.append(pl.BlockSpec(memory_space=pl.ANY))
    args.append(ys)
    return pl.pallas_call(
        functools.partial(_combine_kernel, final=final),
        out_shape=jax.ShapeDtypeStruct((n_blocks * ROW_BLOCK, D_MODEL), F32),
        grid=(n_blocks,),
        in_specs=in_specs,
        out_specs=pl.BlockSpec((ROW_BLOCK, D_MODEL), lambda i: (i, 0)),
        scratch_shapes=[pltpu.VMEM((2, ROW_BLOCK, D_MODEL), F32), pltpu.SemaphoreType.DMA(())],
        compiler_params=pltpu.CompilerParams(dimension_semantics=("arbitrary",), vmem_limit_bytes=VMEM_LIMIT),
        name="moe_combine_final" if final else "moe_combine",
    )(*args)


def _hier_moe(h, g, w_rg, b_rg, w_re, b_re, w_gate, w_up, w_down, gfin, splits):
    n_rows = h.shape[0]
    n_blocks = n_rows // ROW_BLOCK
    n_re = N_GROUPS * EXPERTS_PER_GROUP
    wr = jnp.zeros((D_MODEL, LANES), F32)
    wr = wr.at[:, :N_GROUPS].set(w_rg).at[:, N_GROUPS:N_GROUPS + n_re].set(w_re.reshape(D_MODEL, n_re))
    br = jnp.zeros((1, LANES), F32)
    br = br.at[0, :N_GROUPS].set(b_rg).at[0, N_GROUPS:N_GROUPS + n_re].set(b_re.reshape(n_re))
    xn, meta_i, meta_w, counts = _route(h, g.reshape(1, D_MODEL), wr, br, n_blocks)

    n_tiles = (2 * n_rows + N_EXPERTS * (GMM_TILE - 1) + GMM_TILE - 1) // GMM_TILE
    cnt = counts[0, :N_EXPERTS]
    padded = ((cnt + GMM_TILE - 1) // GMM_TILE) * GMM_TILE
    ends = jnp.cumsum(padded)
    offs = ends - padded
    tile_start = jnp.arange(n_tiles, dtype=jnp.int32) * GMM_TILE
    tile_expert = jnp.minimum(jnp.sum((tile_start[:, None] >= ends[None, :]).astype(jnp.int32), axis=1), N_EXPERTS - 1)
    n_valid = (ends[-1:] // GMM_TILE).astype(jnp.int32)
    dest = (jnp.take(offs, meta_i[:, 0:2, :]) + meta_i[:, 2:4, :]).astype(jnp.int32)

    xs = _dispatch(dest, xn, jnp.zeros((n_tiles * GMM_TILE, D_MODEL), F32), n_blocks)
    ys = _gmm(tile_expert.astype(jnp.int32), n_valid, xs, w_gate, w_up, w_down)
    gf = None if gfin is None else gfin.reshape(1, D_MODEL)
    return [_combine(dest, h, meta_w, ys, gf, block0=b0, n_blocks=nb) for (b0, nb) in splits]


def _qkv_kernel(h_ref, g_ref, w_ref, q_ref, k_ref, v_ref):
    xn = _rmsnorm(h_ref[...], g_ref[...]).astype(BF16)
    width = ATT_HEADS * ATT_DH
    q_ref[...] = _dot(xn, w_ref[:, 0:width]).astype(BF16)
    k_ref[...] = _dot(xn, w_ref[:, width:2 * width])
    v_ref[...] = _dot(xn, w_ref[:, 2 * width:3 * width])


def _qkv(h, g, w_bf16, n_blocks):
    n_rows = n_blocks * ROW_BLOCK
    width = ATT_HEADS * ATT_DH
    row_blk = lambda i: (i, 0)
    return pl.pallas_call(
        _qkv_kernel,
        out_shape=(
            jax.ShapeDtypeStruct((n_rows, width), BF16),
            jax.ShapeDtypeStruct((n_rows, width), F32),
            jax.ShapeDtypeStruct((n_rows, width), F32),
        ),
        grid=(n_blocks,),
        in_specs=[
            pl.BlockSpec((ROW_BLOCK, D_MODEL), row_blk),
            pl.BlockSpec((1, D_MODEL), lambda i: (0, 0)),
            pl.BlockSpec((D_MODEL, 3 * width), lambda i: (0, 0), pipeline_mode=pl.Buffered(1)),
        ],
        out_specs=(
            pl.BlockSpec((ROW_BLOCK, width), row_blk),
            pl.BlockSpec((ROW_BLOCK, width), row_blk),
            pl.BlockSpec((ROW_BLOCK, width), row_blk),
        ),
        compiler_params=pltpu.CompilerParams(dimension_semantics=("arbitrary",), vmem_limit_bytes=VMEM_LIMIT),
        name="qkv_proj",
    )(h, g, w_bf16)


def _softmax_rows(parts):
    m = functools.reduce(jnp.maximum, [jnp.max(s, axis=-1, keepdims=True) for s in parts])
    ps = [jnp.exp(s - m) for s in parts]
    inv = 1.0 / functools.reduce(jnp.add, [jnp.sum(p, axis=-1, keepdims=True) for p in ps])
    return [p * inv for p in ps]


def _band_prompt_kernel(h_ref, q_ref, ka_ref, kb_ref, kc_ref, va_ref, vb_ref, vc_ref, bm_ref, wout_ref,
                        out_ref, kw_ref, vw_ref, o_ref):
    i = pl.program_id(1)
    for b, (k_ref, v_ref) in enumerate(((ka_ref, va_ref), (kb_ref, vb_ref), (kc_ref, vc_ref))):
        kw_ref[pl.ds(b * ATT_QB, ATT_QB), :] = k_ref[...].astype(BF16)
        vw_ref[pl.ds(b * ATT_QB, ATT_QB), :] = v_ref[...].astype(BF16)
    col = lax.broadcasted_iota(jnp.int32, (1, ATT_WIN), 1)
    first_valid = (2 - jnp.minimum(i, 2)) * ATT_QB
    col_mask = jnp.where(col >= first_valid, 0.0, NEG_BIG)
    for h in range(ATT_HEADS):
        hs = slice(h * ATT_DH, (h + 1) * ATT_DH)
        s = _dot_nt(q_ref[:, hs], kw_ref[:, hs]) * (ATT_DH ** -0.5) + bm_ref[h] + col_mask
        (p,) = _softmax_rows([s])
        o_ref[:, hs] = _dot(p.astype(BF16), vw_ref[:, hs]).astype(BF16)
    out_ref[...] = _dot(o_ref[...], wout_ref[...]) + h_ref[...]


def _band_prompt(h, q, k, v, bm, wout, *, n_rows_total, n_seq, blocks_per_seq):
    width = ATT_HEADS * ATT_DH
    blk = lambda s, i: (s * blocks_per_seq + i, 0)
    back = lambda n: (lambda s, i: (s * blocks_per_seq + jnp.maximum(i - n, 0), 0))
    kv_spec = lambda n: pl.BlockSpec((ATT_QB, width), back(n))
    return pl.pallas_call(
        _band_prompt_kernel,
        out_shape=jax.ShapeDtypeStruct((n_rows_total, D_MODEL), F32),
        grid=(n_seq, blocks_per_seq),
        in_specs=[
            pl.BlockSpec((ATT_QB, D_MODEL), blk),
            pl.BlockSpec((ATT_QB, width), blk),
            kv_spec(2), kv_spec(1), kv_spec(0),
            kv_spec(2), kv_spec(1), kv_spec(0),
            pl.BlockSpec((ATT_HEADS, ATT_QB, ATT_WIN), lambda s, i: (0, 0, 0), pipeline_mode=pl.Buffered(1)),
            pl.BlockSpec((width, D_MODEL), lambda s, i: (0, 0), pipeline_mode=pl.Buffered(1)),
        ],
        out_specs=pl.BlockSpec((ATT_QB, D_MODEL), blk),
        scratch_shapes=[
            pltpu.VMEM((ATT_WIN, width), BF16),
            pltpu.VMEM((ATT_WIN, width), BF16),
            pltpu.VMEM((ATT_QB, width), BF16),
        ],
        compiler_params=pltpu.CompilerParams(dimension_semantics=("arbitrary", "arbitrary"), vmem_limit_bytes=VMEM_LIMIT),
        name="band_attn_prompt",
    )(h, q, k, k, k, v, v, v, bm, wout)


def _band_sample_kernel(h_ref, q_ref, kn_ref, vn_ref, ck_ref, cv_ref, b1_ref, b2_ref, wout_ref, hbuf_ref,
                        out_ref, o_ref):
    del hbuf_ref
    for h in range(ATT_HEADS):
        hs = slice(h * ATT_DH, (h + 1) * ATT_DH)
        qh = q_ref[:, hs]
        s1 = _dot_nt(qh, ck_ref[0, :, hs].astype(BF16)) * (ATT_DH ** -0.5) + b1_ref[h]
        s2 = _dot_nt(qh, kn_ref[:, hs].astype(BF16)) * (ATT_DH ** -0.5) + b2_ref[h]
        p1, p2 = _softmax_rows([s1, s2])
        o = _dot(p1.astype(BF16), cv_ref[0, :, hs].astype(BF16)) + _dot(p2.astype(BF16), vn_ref[:, hs].astype(BF16))
        o_ref[:, hs] = o.astype(BF16)
    out_ref[...] = _dot(o_ref[...], wout_ref[...]) + h_ref[...]


def _band_sample(h, q, k, v, cache_k, cache_v, b1, b2, wout, hbuf, *, row0, n_seq, seq_len):
    width = ATT_HEADS * ATT_DH
    past = cache_k.shape[1]
    blk0 = row0 // seq_len
    blk = lambda s: (blk0 + s, 0)
    const3 = lambda s: (0, 0, 0)
    return pl.pallas_call(
        _band_sample_kernel,
        out_shape=jax.ShapeDtypeStruct(hbuf.shape, F32),
        grid=(n_seq,),
        in_specs=[
            pl.BlockSpec((seq_len, D_MODEL), blk),
            pl.BlockSpec((seq_len, width), blk),
            pl.BlockSpec((seq_len, width), blk),
            pl.BlockSpec((seq_len, width), blk),
            pl.BlockSpec((1, past, width), lambda s: (s, 0, 0)),
            pl.BlockSpec((1, past, width), lambda s: (s, 0, 0)),
            pl.BlockSpec((ATT_HEADS, seq_len, past), const3),
            pl.BlockSpec((ATT_HEADS, seq_len, seq_len), const3),
            pl.BlockSpec((width, D_MODEL), lambda s: (0, 0)),
            pl.BlockSpec(memory_space=pl.ANY),
        ],
        out_specs=pl.BlockSpec((seq_len, D_MODEL), blk),
        scratch_shapes=[pltpu.VMEM((seq_len, width), BF16)],
        input_output_aliases={9: 0},
        compiler_params=pltpu.CompilerParams(dimension_semantics=("arbitrary",), vmem_limit_bytes=VMEM_LIMIT),
        name="band_attn_sample",
    )(h, q, k, v, cache_k, cache_v, b1, b2, wout, hbuf)


def _rel_index(q_minus_k):
    return jnp.clip(q_minus_k, -(CHUNK - 1), REL_CLIP) + (CHUNK - 1)


def kernel(x_prompt, x_sample, state_gla, state_conv, cache_band_k, cache_band_v, norm_mix_g, norm_ffn_g, norm_final_g, w_in_even, w_alpha_up, b_alpha, gla_norm_g, conv_w, conv_b, conv_ln_g, conv_ln_b, w_out_even, w_qkv_odd, rel_bias, w_out_odd, w_router_grp, b_router_grp, w_router_exp, b_router_exp, w_exp_gate, w_exp_up, w_exp_down):
    batch, seq, _ = x_prompt.shape
    dec_batch, dec_seq, _ = x_sample.shape
    n_prompt = batch * seq
    n_sample = dec_batch * dec_seq
    assert seq % ROW_BLOCK == 0 and n_sample == ROW_BLOCK and seq % ATT_QB == 0
    n_rows = n_prompt + n_sample
    prompt_blocks = n_prompt // ROW_BLOCK
    n_blocks = n_rows // ROW_BLOCK
    width = ATT_HEADS * ATT_DH
    past = cache_band_k.shape[2]

    w_in = w_in_even[0]
    c0, c1, c2, c3, c4, c5 = (QK_W, 2 * QK_W, 2 * QK_W + V_W, 2 * QK_W + 2 * V_W,
                              2 * QK_W + 2 * V_W + GLA_LOWRANK, 2 * QK_W + 2 * V_W + GLA_LOWRANK + CONV_CH)
    w_in_p = jnp.concatenate(
        [w_in[:, :c3], w_in[:, c4:], w_in[:, c3:c4], jnp.zeros((D_MODEL, LANES - GLA_LOWRANK), F32)], axis=1).astype(BF16)
    wau_p = jnp.concatenate([w_alpha_up[0], jnp.zeros((LANES - GLA_LOWRANK, QK_W), F32)], axis=0).astype(BF16)
    even_w = (norm_mix_g[0].reshape(1, D_MODEL), w_in_p, wau_p, b_alpha[0].reshape(1, QK_W),
              gla_norm_g[0].reshape(1, V_W), conv_w[0], conv_b[0].reshape(1, CONV_CH),
              conv_ln_g[0].reshape(1, CONV_CH), conv_ln_b[0].reshape(1, CONV_CH), w_out_even[0].astype(BF16))
    h1, gla_p, conv_p = _even_mixer(
        x_prompt.reshape(n_prompt, D_MODEL), None, even_w,
        jnp.zeros((batch, GLA_HEADS, GLA_DK, GLA_DV), F32), jnp.zeros((batch, HIST, CONV_CH), F32),
        n_rows_total=n_rows, block0=0, n_seq=batch, blocks_per_seq=seq // ROW_BLOCK,
        chunk=CHUNK, cps=ROW_BLOCK // CHUNK, spb=1)
    h1, gla_s, conv_s = _even_mixer(
        x_sample.reshape(n_sample, D_MODEL), h1, even_w, state_gla[0], state_conv[0],
        n_rows_total=n_rows, block0=prompt_blocks, n_seq=1, blocks_per_seq=1,
        chunk=dec_seq, cps=1, spb=dec_batch)

    def moe(h, layer, gfin, splits):
        return _hier_moe(h, norm_ffn_g[layer], w_router_grp[layer], b_router_grp[layer], w_router_exp[layer],
                         b_router_exp[layer], w_exp_gate[layer], w_exp_up[layer], w_exp_down[layer], gfin, splits)

    (h2,) = moe(h1, 0, None, [(0, n_blocks)])

    q, k, v = _qkv(h2, norm_mix_g[1].reshape(1, D_MODEL), w_qkv_odd[0].astype(BF16), n_blocks)
    rb = rel_bias[0]
    r = jnp.arange(ATT_QB)[:, None]
    jw = jnp.arange(ATT_WIN)[None, :]
    jb = jw - CHUNK * (r // CHUNK)
    in_band = jnp.logical_and(jb >= 0, jb < BAND_PAST + CHUNK)
    bm = jnp.where(in_band[None], rb[:, _rel_index(BAND_PAST + r - jw)], NEG_BIG)
    wout_odd = w_out_odd[0].astype(BF16)
    h3 = _band_prompt(h2, q, k, v, bm, wout_odd, n_rows_total=n_rows, n_seq=batch, blocks_per_seq=seq // ATT_QB)
    ri = jnp.arange(dec_seq)[:, None]
    b1 = rb[:, _rel_index(past + ri - jnp.arange(past)[None, :])]
    b2 = rb[:, _rel_index(ri - jnp.arange(dec_seq)[None, :])]
    h3 = _band_sample(h2, q, k, v, cache_band_k[0].reshape(dec_batch, past, width),
                      cache_band_v[0].reshape(dec_batch, past, width), b1, b2, wout_odd, h3,
                      row0=n_prompt, n_seq=dec_batch, seq_len=dec_seq)

    y_prompt, y_sample = moe(h3, 1, norm_final_g, [(0, prompt_blocks), (prompt_blocks, n_blocks - prompt_blocks)])

    rows = min(BAND_PAST, seq)
    kp = k[:n_prompt].reshape(batch, seq, ATT_HEADS, ATT_DH)[:, seq - rows:]
    vp = v[:n_prompt].reshape(batch, seq, ATT_HEADS, ATT_DH)[:, seq - rows:]
    ks = k[n_prompt:].reshape(dec_batch, dec_seq, ATT_HEADS, ATT_DH)
    vs = v[n_prompt:].reshape(dec_batch, dec_seq, ATT_HEADS, ATT_DH)
    return (y_prompt.reshape(batch, seq, D_MODEL), y_sample.reshape(dec_batch, dec_seq, D_MODEL),
            gla_p[None], gla_s[None], conv_p[None], conv_s[None], kp[None], vp[None], ks[None], vs[None])
```

```python
import functools

import jax
import jax.numpy as jnp
from jax import lax
from jax.experimental import pallas as pl
from jax.experimental.pallas import tpu as pltpu

F32 = jnp.float32
BF16 = jnp.bfloat16
HIGHEST = lax.Precision.HIGHEST

D_MODEL = 1024
CHUNK = 64
EPS = 1e-6
GLA_HEADS = 4
GLA_DK = 64
GLA_DV = 128
GLA_LOWRANK = 16
GLA_TAU = 16.0
CONV_CH = 512
CONV_WIDTH = 31
HIST = CONV_WIDTH - 1
ATT_HEADS = 16
ATT_DH = 64
BAND_CHUNKS_PAST = 8
BAND_PAST = BAND_CHUNKS_PAST * CHUNK
REL_CLIP = 256
N_GROUPS = 4
EXPERTS_PER_GROUP = 8
N_EXPERTS = N_GROUPS * EXPERTS_PER_GROUP
D_EXPERT = 512

LANES = 128
SUBLANES = 8
ROW_BLOCK = 512
CONV_TILE = 32
GMM_TILE = 256
ATT_QB = 256
ATT_WIN = ATT_QB + BAND_PAST
NEG_BIG = -1e30
VMEM_LIMIT = 56 * 1024 * 1024

QK_W = GLA_HEADS * GLA_DK
V_W = GLA_HEADS * GLA_DV
COL_Q = 0
COL_K = COL_Q + QK_W
COL_V = COL_K + QK_W
COL_GATE = COL_V + V_W
COL_CVAL = COL_GATE + V_W
COL_CGATE = COL_CVAL + CONV_CH
COL_ALR = COL_CGATE + CONV_CH
EVEN_COLS = COL_ALR + LANES


def _rmsnorm(x, g):
    return x * lax.rsqrt(jnp.mean(x * x, axis=-1, keepdims=True) + EPS) * g


def _sigmoid(x):
    return 1.0 / (1.0 + jnp.exp(-x))


def _dot(a, b):
    return jnp.dot(a, b, preferred_element_type=F32)


def _dot_nt(a, b):
    return lax.dot_general(a, b, (((1,), (1,)), ((), ())), preferred_element_type=F32)


def _dot_tn(a, b, precision=None):
    return lax.dot_general(a, b, (((0,), (0,)), ((), ())), preferred_element_type=F32, precision=precision)


def _even_mixer_kernel(*refs, chunk, cps, spb, carry, aliased):
    (x_ref, g_ref, win_ref, wau_ref, bal_ref, gng_ref, cw_ref, cb_ref, lng_ref, lnb_ref, wout_ref,
     s0_ref, c0_ref) = refs[:13]
    refs = refs[13 + (1 if aliased else 0):]
    h_ref, sfin_ref, cfin_ref, proj_ref, lg_ref, cum_ref, mix_ref, s_ref, ubuf_ref = refs
    j = pl.program_id(1)
    nj = pl.num_programs(1)
    seg = cps * chunk

    x = x_ref[...]
    xn = _rmsnorm(x, g_ref[...])
    proj_ref[...] = _dot(xn.astype(BF16), win_ref[...])
    alr = proj_ref[:, COL_ALR:COL_ALR + LANES]
    xa = _dot(alr.astype(BF16), wau_ref[...]) + bal_ref[...]
    lg_ref[...] = (jnp.minimum(xa, 0.0) - jnp.log1p(jnp.exp(-jnp.abs(xa)))) * (1.0 / GLA_TAU)

    row_i = lax.broadcasted_iota(jnp.int32, (chunk, QK_W), 0)
    tri_r = lax.broadcasted_iota(jnp.int32, (chunk, chunk), 0)
    tri_c = lax.broadcasted_iota(jnp.int32, (chunk, chunk), 1)
    causal = tri_r >= tri_c
    ones_cols = jnp.ones((chunk, LANES), F32)

    def chunk_body(ci, c):
        r0 = pl.multiple_of(ci * chunk, chunk)
        sq = ci // cps if spb > 1 else 0
        first = (ci % cps) == 0
        last = (ci % cps) == (cps - 1)
        if carry:
            first = jnp.logical_and(first, j == 0)
            last = jnp.logical_and(last, j == nj - 1)

        @pl.when(first)
        def _():
            s_ref[...] = s0_ref[sq]

        lg = lg_ref[pl.ds(r0, chunk), :]
        cum = lg
        shift = 1
        while shift < chunk:
            cum = cum + jnp.where(row_i >= shift, pltpu.roll(cum, shift, 0), 0.0)
            shift *= 2
        cum_ref[...] = cum
        tot = cum_ref[pl.ds(chunk - 1, 1), :]
        mid = cum_ref[pl.ds(chunk // 2 - 1, 1), :]
        tot_col = jnp.exp(_dot_tn(lg, ones_cols, precision=HIGHEST))
        e_in = jnp.exp(cum)
        e_q = jnp.exp(cum - mid)
        e_k = jnp.exp(mid - cum)
        e_s = jnp.exp(tot - cum)
        for h in range(GLA_HEADS):
            ks = slice(h * GLA_DK, (h + 1) * GLA_DK)
            vs = slice(h * GLA_DV, (h + 1) * GLA_DV)
            q = proj_ref[pl.ds(r0, chunk), COL_Q + h * GLA_DK:COL_Q + (h + 1) * GLA_DK] * (GLA_DK ** -0.5)
            k = proj_ref[pl.ds(r0, chunk), COL_K + h * GLA_DK:COL_K + (h + 1) * GLA_DK]
            v = proj_ref[pl.ds(r0, chunk), COL_V + h * GLA_DV:COL_V + (h + 1) * GLA_DV].astype(BF16)
            gate = proj_ref[pl.ds(r0, chunk), COL_GATE + h * GLA_DV:COL_GATE + (h + 1) * GLA_DV]
            s_old = s_ref[h]
            scores = _dot_nt((q * e_q[:, ks]).astype(BF16), (k * e_k[:, ks]).astype(BF16))
            scores = jnp.where(causal, scores, 0.0)
            o = _dot((q * e_in[:, ks]).astype(BF16), s_old.astype(BF16)) + _dot(scores.astype(BF16), v)
            s_ref[h] = tot_col[h * GLA_DK:(h + 1) * GLA_DK, :] * s_old + _dot_tn((k * e_s[:, ks]).astype(BF16), v)
            o = o * lax.rsqrt(jnp.mean(o * o, axis=-1, keepdims=True) + EPS)
            o = o * gng_ref[:, vs] * (gate * _sigmoid(gate))
            mix_ref[pl.ds(r0, chunk), vs] = o.astype(BF16)

        @pl.when(last)
        def _():
            sfin_ref[sq] = s_ref[...]

        return c

    lax.fori_loop(0, spb * cps, chunk_body, 0, unroll=2)

    tile = CONV_TILE

    def conv_tile(t0, out_r0):
        wv = ubuf_ref[pl.ds(t0, 2 * tile), :]
        acc = jnp.zeros((tile, CONV_CH), F32)
        for b in range(SUBLANES):
            sb = wv if b == 0 else pltpu.roll(wv, 2 * tile - b, 0)
            for a in range(tile // SUBLANES + 1):
                off = SUBLANES * a + b
                if 2 <= off <= HIST + 2:
                    acc = acc + sb[SUBLANES * a:SUBLANES * a + tile, :] * cw_ref[pl.ds(off - 2, 1), :]
        cv = acc + cb_ref[...]
        mu = jnp.mean(cv, axis=-1, keepdims=True)
        var = jnp.mean(jnp.square(cv - mu), axis=-1, keepdims=True)
        y = (cv - mu) * lax.rsqrt(var + EPS) * lng_ref[...] + lnb_ref[...]
        mix_ref[pl.ds(out_r0, tile), V_W:V_W + CONV_CH] = (y * _sigmoid(y)).astype(BF16)

    def conv_seg(sq, c):
        r0 = pl.multiple_of(sq * seg, seg) if spb > 1 else 0
        cval = proj_ref[pl.ds(r0, seg), COL_CVAL:COL_CVAL + CONV_CH]
        cgate = proj_ref[pl.ds(r0, seg), COL_CGATE:COL_CGATE + CONV_CH]

        def load_history():
            ubuf_ref[pl.ds(0, 8), :] = jnp.zeros((8, CONV_CH), F32)
            ubuf_ref[pl.ds(2, HIST), :] = c0_ref[sq]

        if carry:
            pl.when(j == 0)(load_history)
        else:
            load_history()
        ubuf_ref[pl.ds(CONV_TILE, seg), :] = cval * _sigmoid(cgate)
        if seg == tile:
            conv_tile(0, r0)
        else:
            def tile_body(t, cc):
                t0 = pl.multiple_of(t * tile, tile)
                conv_tile(t0, r0 + t0)
                return cc
            lax.fori_loop(0, seg // tile, tile_body, 0)
        hist = ubuf_ref[pl.ds(seg + 2, HIST), :]
        if carry:
            @pl.when(j == nj - 1)
            def _():
                cfin_ref[sq] = hist
            ubuf_ref[pl.ds(2, HIST), :] = hist
        else:
            cfin_ref[sq] = hist
        return c

    if spb > 1:
        lax.fori_loop(0, spb, conv_seg, 0)
    else:
        conv_seg(0, 0)

    h_ref[...] = _dot(mix_ref[...], wout_ref[...]) + x


def _even_mixer(x2d, hbuf, weights, s0, c0, *, n_rows_total, block0, n_seq, blocks_per_seq, chunk, cps, spb):
    carry = spb == 1
    aliased = hbuf is not None
    const = lambda s, j: (0, 0)
    seq_blk = (lambda s, j: (s, 0, 0, 0)) if carry else (lambda s, j: (0, 0, 0, 0))
    seq_blk3 = (lambda s, j: (s, 0, 0)) if carry else (lambda s, j: (0, 0, 0))
    n_state = 1 if carry else spb
    wspec = lambda shape: pl.BlockSpec(shape, const, pipeline_mode=pl.Buffered(1))
    in_specs = [
        pl.BlockSpec((ROW_BLOCK, D_MODEL), lambda s, j: (s * blocks_per_seq + j, 0)),
        wspec((1, D_MODEL)),
        wspec((D_MODEL, EVEN_COLS)),
        wspec((LANES, QK_W)),
        wspec((1, QK_W)),
        wspec((1, V_W)),
        wspec((CONV_WIDTH, CONV_CH)),
        wspec((1, CONV_CH)),
        wspec((1, CONV_CH)),
        wspec((1, CONV_CH)),
        wspec((V_W + CONV_CH, D_MODEL)),
        pl.BlockSpec((n_state, GLA_HEADS, GLA_DK, GLA_DV), seq_blk),
        pl.BlockSpec((n_state, HIST, CONV_CH), seq_blk3),
    ]
    args = [x2d, *weights, s0, c0]
    aliases = {}
    if aliased:
        in_specs.append(pl.BlockSpec(memory_space=pl.ANY))
        args.append(hbuf)
        aliases = {len(args) - 1: 0}
    n_all = s0.shape[0]
    out_shape = (
        jax.ShapeDtypeStruct((n_rows_total, D_MODEL), F32),
        jax.ShapeDtypeStruct((n_all, GLA_HEADS, GLA_DK, GLA_DV), F32),
        jax.ShapeDtypeStruct((n_all, HIST, CONV_CH), F32),
    )
    out_specs = (
        pl.BlockSpec((ROW_BLOCK, D_MODEL), lambda s, j: (block0 + s * blocks_per_seq + j, 0)),
        pl.BlockSpec((n_state, GLA_HEADS, GLA_DK, GLA_DV), seq_blk),
        pl.BlockSpec((n_state, HIST, CONV_CH), seq_blk3),
    )
    seg = cps * chunk
    scratch = [
        pltpu.VMEM((ROW_BLOCK, EVEN_COLS), F32),
        pltpu.VMEM((ROW_BLOCK, QK_W), F32),
        pltpu.VMEM((chunk, QK_W), F32),
        pltpu.VMEM((ROW_BLOCK, V_W + CONV_CH), BF16),
        pltpu.VMEM((GLA_HEADS, GLA_DK, GLA_DV), F32),
        pltpu.VMEM((CONV_TILE + seg, CONV_CH), F32),
    ]
    kern = functools.partial(_even_mixer_kernel, chunk=chunk, cps=cps, spb=spb, carry=carry, aliased=aliased)
    return pl.pallas_call(
        kern, out_shape=out_shape, grid=(n_seq, blocks_per_seq), in_specs=in_specs, out_specs=out_specs,
        scratch_shapes=scratch, input_output_aliases=aliases,
        compiler_params=pltpu.CompilerParams(dimension_semantics=("arbitrary", "arbitrary"), vmem_limit_bytes=VMEM_LIMIT),
        name="even_mixer_carry" if carry else "even_mixer_step",
    )(*args)


def _route_kernel(h_ref, g_ref, wr_ref, br_ref, xn_ref, mi_ref, mw_ref, cnt_ref, base_ref):
    i = pl.program_id(0)

    @pl.when(i == 0)
    def _():
        base_ref[...] = jnp.zeros_like(base_ref)

    xn = _rmsnorm(h_ref[...], g_ref[...])
    xn_ref[...] = xn
    logits = jnp.dot(xn, wr_ref[...], precision=HIGHEST, preferred_element_type=F32) + br_ref[...]
    lane = lax.broadcasted_iota(jnp.int32, logits.shape, 1).astype(F32)
    far = float(1 << 20)
    gl = jnp.where(lane < N_GROUPS, logits, -jnp.inf)
    gmax = jnp.max(gl, axis=-1, keepdims=True)
    gidx = jnp.min(jnp.where(gl == gmax, lane, far), axis=-1, keepdims=True)
    gw = 1.0 / jnp.sum(jnp.exp(gl - gmax), axis=-1, keepdims=True)
    lo = N_GROUPS + gidx * EXPERTS_PER_GROUP
    el = jnp.where(jnp.logical_and(lane >= lo, lane < lo + EXPERTS_PER_GROUP), logits, -jnp.inf)
    m1 = jnp.max(el, axis=-1, keepdims=True)
    i1 = jnp.min(jnp.where(el == m1, lane, far), axis=-1, keepdims=True)
    el2 = jnp.where(lane == i1, -jnp.inf, el)
    m2 = jnp.max(el2, axis=-1, keepdims=True)
    i2 = jnp.min(jnp.where(el2 == m2, lane, far), axis=-1, keepdims=True)
    e2 = jnp.exp(m2 - m1)
    den = 1.0 + e2
    w0 = (1.0 / den) * gw
    w1 = (e2 / den) * gw
    id0 = i1 - N_GROUPS
    id1 = i2 - N_GROUPS
    oh0 = jnp.where(lane == id0, 1.0, 0.0)
    oh1 = jnp.where(lane == id1, 1.0, 0.0)
    rr = lax.broadcasted_iota(jnp.int32, (ROW_BLOCK, ROW_BLOCK), 0)
    cc = lax.broadcasted_iota(jnp.int32, (ROW_BLOCK, ROW_BLOCK), 1)
    below = jnp.where(rr > cc, 1.0, 0.0).astype(BF16)
    p0 = _dot(below, oh0.astype(BF16))
    p1 = _dot(below, oh1.astype(BF16))
    cnt0 = jnp.sum(oh0, axis=0, keepdims=True)
    cnt1 = jnp.sum(oh1, axis=0, keepdims=True)
    base = base_ref[...]
    rank0 = jnp.sum(oh0 * (p0 + base), axis=-1, keepdims=True)
    rank1 = jnp.sum(oh1 * (p1 + base + cnt0), axis=-1, keepdims=True)
    new_base = base + cnt0 + cnt1
    base_ref[...] = new_base
    cnt_ref[...] = new_base.astype(jnp.int32)
    meta = jnp.where(lane == 0, id0, jnp.where(lane == 1, id1, jnp.where(lane == 2, rank0, jnp.where(lane == 3, rank1, 0.0))))
    mi_ref[0] = meta.T[0:8, :].astype(jnp.int32)
    mw_ref[...] = jnp.where(lane == 0, w0, jnp.where(lane == 1, w1, 0.0))


def _route(h, g, wr, br, n_blocks):
    n_rows = n_blocks * ROW_BLOCK
    const = lambda i: (0, 0)
    return pl.pallas_call(
        _route_kernel,
        out_shape=(
            jax.ShapeDtypeStruct((n_rows, D_MODEL), F32),
            jax.ShapeDtypeStruct((n_blocks, 8, ROW_BLOCK), jnp.int32),
            jax.ShapeDtypeStruct((n_rows, LANES), F32),
            jax.ShapeDtypeStruct((1, LANES), jnp.int32),
        ),
        grid=(n_blocks,),
        in_specs=[
            pl.BlockSpec((ROW_BLOCK, D_MODEL), lambda i: (i, 0)),
            pl.BlockSpec((1, D_MODEL), const),
            pl.BlockSpec((D_MODEL, LANES), const),
            pl.BlockSpec((1, LANES), const),
        ],
        out_specs=(
            pl.BlockSpec((ROW_BLOCK, D_MODEL), lambda i: (i, 0)),
            pl.BlockSpec((1, 8, ROW_BLOCK), lambda i: (i, 0, 0)),
            pl.BlockSpec((ROW_BLOCK, LANES), lambda i: (i, 0)),
            pl.BlockSpec((1, LANES), const),
        ),
        scratch_shapes=[pltpu.VMEM((1, LANES), F32)],
        compiler_params=pltpu.CompilerParams(dimension_semantics=("arbitrary",), vmem_limit_bytes=VMEM_LIMIT),
        name="moe_route",
    )(h, g, wr, br)


def _row_copy_out(x_ref, xs_ref, sem, r, d):
    return pltpu.make_async_copy(x_ref.at[pl.ds(r, 1), :], xs_ref.at[pl.ds(d, 1), :], sem)


def _dispatch_kernel(dest_ref, x_ref, xs_in_ref, xs_ref, sem):
    del xs_in_ref

    def body(r, c):
        for slot in range(2):
            _row_copy_out(x_ref, xs_ref, sem, r, dest_ref[0, slot, r]).start()
        return c

    lax.fori_loop(0, ROW_BLOCK, body, 0, unroll=8)

    def drain(r, c):
        for slot in range(2):
            _row_copy_out(x_ref, xs_ref, sem, r, dest_ref[0, slot, r]).wait()
        return c

    lax.fori_loop(0, ROW_BLOCK, drain, 0, unroll=8)


def _dispatch(dest, xn, xs_zero, n_blocks):
    return pl.pallas_call(
        _dispatch_kernel,
        out_shape=jax.ShapeDtypeStruct(xs_zero.shape, F32),
        grid=(n_blocks,),
        in_specs=[
            pl.BlockSpec((1, 2, ROW_BLOCK), lambda i: (i, 0, 0), memory_space=pltpu.SMEM),
            pl.BlockSpec((ROW_BLOCK, D_MODEL), lambda i: (i, 0)),
            pl.BlockSpec(memory_space=pl.ANY),
        ],
        out_specs=pl.BlockSpec(memory_space=pl.ANY),
        scratch_shapes=[pltpu.SemaphoreType.DMA(())],
        input_output_aliases={2: 0},
        compiler_params=pltpu.CompilerParams(dimension_semantics=("arbitrary",), vmem_limit_bytes=VMEM_LIMIT),
        name="moe_dispatch",
    )(dest, xn, xs_zero)


def _gmm_kernel(te_ref, nv_ref, xs_ref, wg_ref, wu_ref, wd_ref, ys_ref, wgb, wub, wdb):
    i = pl.program_id(0)

    @pl.when(i < nv_ref[0])
    def _():
        prev = te_ref[jnp.maximum(i - 1, 0)]

        @pl.when(jnp.logical_or(i == 0, te_ref[i] != prev))
        def _():
            wgb[...] = wg_ref[0, 0].astype(BF16)
            wub[...] = wu_ref[0, 0].astype(BF16)
            wdb[...] = wd_ref[0, 0].astype(BF16)

        x = xs_ref[...].astype(BF16)
        a = _dot(x, wgb[...])
        b = _dot(x, wub[...])
        he = (a * _sigmoid(a)) * b
        ys_ref[...] = _dot(he.astype(BF16), wdb[...])


def _gmm(tile_expert, n_valid, xs, w_gate, w_up, w_down, layer):
    n_tiles = xs.shape[0] // GMM_TILE

    def tile_idx(i, te, nv):
        return jnp.minimum(i, nv[0] - 1)

    def w_idx(i, te, nv):
        return (layer, te[tile_idx(i, te, nv)], 0, 0)

    return pl.pallas_call(
        _gmm_kernel,
        out_shape=jax.ShapeDtypeStruct((xs.shape[0], D_MODEL), F32),
        grid_spec=pltpu.PrefetchScalarGridSpec(
            num_scalar_prefetch=2,
            grid=(n_tiles,),
            in_specs=[
                pl.BlockSpec((GMM_TILE, D_MODEL), lambda i, te, nv: (tile_idx(i, te, nv), 0)),
                pl.BlockSpec((1, 1, D_MODEL, D_EXPERT), w_idx),
                pl.BlockSpec((1, 1, D_MODEL, D_EXPERT), w_idx),
                pl.BlockSpec((1, 1, D_EXPERT, D_MODEL), w_idx),
            ],
            out_specs=pl.BlockSpec((GMM_TILE, D_MODEL), lambda i, te, nv: (tile_idx(i, te, nv), 0)),
            scratch_shapes=[
                pltpu.VMEM((D_MODEL, D_EXPERT), BF16),
                pltpu.VMEM((D_MODEL, D_EXPERT), BF16),
                pltpu.VMEM((D_EXPERT, D_MODEL), BF16),
            ],
        ),
        compiler_params=pltpu.CompilerParams(dimension_semantics=("arbitrary",), vmem_limit_bytes=VMEM_LIMIT),
        name="moe_gmm",
    )(tile_expert, n_valid, xs, w_gate, w_up, w_down)


def _row_copy_in(ys_ref, gbuf, sem, slot, r, d):
    return pltpu.make_async_copy(ys_ref.at[pl.ds(d, 1), :], gbuf.at[slot, pl.ds(r, 1), :], sem)


def _combine_kernel(*refs, final):
    if final:
        dest_ref, h_ref, mw_ref, gfin_ref, ys_ref, out_ref, gbuf, sem = refs
    else:
        dest_ref, h_ref, mw_ref, ys_ref, out_ref, gbuf, sem = refs

    def body(r, c):
        for slot in range(2):
            _row_copy_in(ys_ref, gbuf, sem, slot, r, dest_ref[0, slot, r]).start()
        return c

    lax.fori_loop(0, ROW_BLOCK, body, 0, unroll=8)

    def drain(r, c):
        for slot in range(2):
            _row_copy_in(ys_ref, gbuf, sem, slot, r, dest_ref[0, slot, r]).wait()
        return c

    lax.fori_loop(0, ROW_BLOCK, drain, 0, unroll=8)

    mw = mw_ref[...]
    hn = h_ref[...] + (mw[:, 0:1] * gbuf[0] + mw[:, 1:2] * gbuf[1])
    out_ref[...] = _rmsnorm(hn, gfin_ref[...]) if final else hn


def _combine(dest, h, mw, ys, gfin, *, block0, n_blocks):
    final = gfin is not None
    in_specs = [
        pl.BlockSpec((1, 2, ROW_BLOCK), lambda i: (block0 + i, 0, 0), memory_space=pltpu.SMEM),
        pl.BlockSpec((ROW_BLOCK, D_MODEL), lambda i: (block0 + i, 0)),
        pl.BlockSpec((ROW_BLOCK, LANES), lambda i: (block0 + i, 0)),
    ]
    args = [dest, h, mw]
    if final:
        in_specs.append(pl.BlockSpec((1, D_MODEL), lambda i: (0, 0)))
        args.append(gfin)
    in_specs.append(pl.BlockSpec(memory_space=pl.ANY))
    args.append(ys)
    return pl.pallas_call(
        functools.partial(_combine_kernel, final=final),
        out_shape=jax.ShapeDtypeStruct((n_blocks * ROW_BLOCK, D_MODEL), F32),
        grid=(n_blocks,),
        in_specs=in_specs,
        out_specs=pl.BlockSpec((ROW_BLOCK, D_MODEL), lambda i: (i, 0)),
        scratch_shapes=[pltpu.VMEM((2, ROW_BLOCK, D_MODEL), F32), pltpu.SemaphoreType.DMA(())],
        compiler_params=pltpu.CompilerParams(dimension_semantics=("arbitrary",), vmem_limit_bytes=VMEM_LIMIT),
        name="moe_combine_final" if final else "moe_combine",
    )(*args)


def _moe_tiles(n_rows):
    return (2 * n_rows + N_EXPERTS * (GMM_TILE - 1) + GMM_TILE - 1) // GMM_TILE


def _hier_moe(h, g, w_rg, b_rg, w_re, b_re, w_gate, w_up, w_down, layer, gfin, splits, xs_buf):
    n_rows = h.shape[0]
    n_blocks = n_rows // ROW_BLOCK
    n_re = N_GROUPS * EXPERTS_PER_GROUP
    wr = jnp.zeros((D_MODEL, LANES), F32)
    wr = wr.at[:, :N_GROUPS].set(w_rg).at[:, N_GROUPS:N_GROUPS + n_re].set(w_re.reshape(D_MODEL, n_re))
    br = jnp.zeros((1, LANES), F32)
    br = br.at[0, :N_GROUPS].set(b_rg).at[0, N_GROUPS:N_GROUPS + n_re].set(b_re.reshape(n_re))
    xn, meta_i, meta_w, counts = _route(h, g.reshape(1, D_MODEL), wr, br, n_blocks)

    n_tiles = _moe_tiles(n_rows)
    cnt = counts[0, :N_EXPERTS]
    padded = ((cnt + GMM_TILE - 1) // GMM_TILE) * GMM_TILE
    ends = jnp.cumsum(padded)
    offs = ends - padded
    tile_start = jnp.arange(n_tiles, dtype=jnp.int32) * GMM_TILE
    tile_expert = jnp.minimum(jnp.sum((tile_start[:, None] >= ends[None, :]).astype(jnp.int32), axis=1), N_EXPERTS - 1)
    n_valid = (ends[-1:] // GMM_TILE).astype(jnp.int32)
    eid = meta_i[:, 0:2, :]
    seg_start = jnp.sum(jnp.where(eid[..., None] == jnp.arange(N_EXPERTS, dtype=jnp.int32), offs, 0), axis=-1)
    dest = (seg_start + meta_i[:, 2:4, :]).astype(jnp.int32)

    xs = _dispatch(dest, xn, xs_buf, n_blocks)
    ys = _gmm(tile_expert.astype(jnp.int32), n_valid, xs, w_gate, w_up, w_down, layer)
    gf = None if gfin is None else gfin.reshape(1, D_MODEL)
    return [_combine(dest, h, meta_w, ys, gf, block0=b0, n_blocks=nb) for (b0, nb) in splits], xs


def _qkv_kernel(h_ref, g_ref, w_ref, q_ref, k_ref, v_ref):
    xn = _rmsnorm(h_ref[...], g_ref[...]).astype(BF16)
    width = ATT_HEADS * ATT_DH
    q_ref[...] = (_dot(xn, w_ref[:, 0:width]) * (ATT_DH ** -0.5)).astype(BF16)
    k_ref[...] = _dot(xn, w_ref[:, width:2 * width])
    v_ref[...] = _dot(xn, w_ref[:, 2 * width:3 * width])


def _qkv(h, g, w_bf16, n_blocks):
    n_rows = n_blocks * ROW_BLOCK
    width = ATT_HEADS * ATT_DH
    row_blk = lambda i: (i, 0)
    return pl.pallas_call(
        _qkv_kernel,
        out_shape=(
            jax.ShapeDtypeStruct((n_rows, width), BF16),
            jax.ShapeDtypeStruct((n_rows, width), F32),
            jax.ShapeDtypeStruct((n_rows, width), F32),
        ),
        grid=(n_blocks,),
        in_specs=[
            pl.BlockSpec((ROW_BLOCK, D_MODEL), row_blk),
            pl.BlockSpec((1, D_MODEL), lambda i: (0, 0)),
            pl.BlockSpec((D_MODEL, 3 * width), lambda i: (0, 0), pipeline_mode=pl.Buffered(1)),
        ],
        out_specs=(
            pl.BlockSpec((ROW_BLOCK, width), row_blk),
            pl.BlockSpec((ROW_BLOCK, width), row_blk),
            pl.BlockSpec((ROW_BLOCK, width), row_blk),
        ),
        compiler_params=pltpu.CompilerParams(dimension_semantics=("arbitrary",), vmem_limit_bytes=VMEM_LIMIT),
        name="qkv_proj",
    )(h, g, w_bf16)


def _softmax_parts(parts):
    m = functools.reduce(jnp.maximum, [jnp.max(s, axis=-1, keepdims=True) for s in parts])
    ps = [jnp.exp(s - m) for s in parts]
    inv = 1.0 / functools.reduce(jnp.add, [jnp.sum(p, axis=-1, keepdims=True) for p in ps])
    return ps, inv


def _band_prompt_kernel(h_ref, q_ref, ka_ref, kb_ref, kc_ref, va_ref, vb_ref, vc_ref, bm_ref, wout_ref,
                        out_ref, kw_ref, vw_ref, o_ref):
    i = pl.program_id(1)
    for b, (k_ref, v_ref) in enumerate(((ka_ref, va_ref), (kb_ref, vb_ref), (kc_ref, vc_ref))):
        kw_ref[pl.ds(b * ATT_QB, ATT_QB), :] = k_ref[...].astype(BF16)
        vw_ref[pl.ds(b * ATT_QB, ATT_QB), :] = v_ref[...].astype(BF16)
    col = lax.broadcasted_iota(jnp.int32, (1, ATT_WIN), 1)
    first_valid = (2 - jnp.minimum(i, 2)) * ATT_QB
    col_mask = jnp.where(col >= first_valid, 0.0, NEG_BIG)
    for h in range(ATT_HEADS):
        hs = slice(h * ATT_DH, (h + 1) * ATT_DH)
        s = _dot_nt(q_ref[:, hs], kw_ref[:, hs]) + bm_ref[h] + col_mask
        (p,), inv = _softmax_parts([s])
        o_ref[:, hs] = (_dot(p.astype(BF16), vw_ref[:, hs]) * inv).astype(BF16)
    out_ref[...] = _dot(o_ref[...], wout_ref[...]) + h_ref[...]


def _band_prompt(h, q, k, v, bm, wout, *, n_rows_total, n_seq, blocks_per_seq):
    width = ATT_HEADS * ATT_DH
    blk = lambda s, i: (s * blocks_per_seq + i, 0)
    back = lambda n: (lambda s, i: (s * blocks_per_seq + jnp.maximum(i - n, 0), 0))
    kv_spec = lambda n: pl.BlockSpec((ATT_QB, width), back(n))
    return pl.pallas_call(
        _band_prompt_kernel,
        out_shape=jax.ShapeDtypeStruct((n_rows_total, D_MODEL), F32),
        grid=(n_seq, blocks_per_seq),
        in_specs=[
            pl.BlockSpec((ATT_QB, D_MODEL), blk),
            pl.BlockSpec((ATT_QB, width), blk),
            kv_spec(2), kv_spec(1), kv_spec(0),
            kv_spec(2), kv_spec(1), kv_spec(0),
            pl.BlockSpec((ATT_HEADS, ATT_QB, ATT_WIN), lambda s, i: (0, 0, 0), pipeline_mode=pl.Buffered(1)),
            pl.BlockSpec((width, D_MODEL), lambda s, i: (0, 0), pipeline_mode=pl.Buffered(1)),
        ],
        out_specs=pl.BlockSpec((ATT_QB, D_MODEL), blk),
        scratch_shapes=[
            pltpu.VMEM((ATT_WIN, width), BF16),
            pltpu.VMEM((ATT_WIN, width), BF16),
            pltpu.VMEM((ATT_QB, width), BF16),
        ],
        compiler_params=pltpu.CompilerParams(dimension_semantics=("arbitrary", "arbitrary"), vmem_limit_bytes=VMEM_LIMIT),
        name="band_attn_prompt",
    )(h, q, k, k, k, v, v, v, bm, wout)


def _band_sample_kernel(h_ref, q_ref, kn_ref, vn_ref, ck_ref, cv_ref, b1_ref, b2_ref, wout_ref, hbuf_ref,
                        out_ref, o_ref):
    del hbuf_ref
    past = ck_ref.shape[1] // ATT_HEADS
    for h in range(ATT_HEADS):
        hs = slice(h * ATT_DH, (h + 1) * ATT_DH)
        qh = q_ref[:, hs]
        past_rows = pl.ds(h, past, stride=ATT_HEADS)
        s1 = _dot_nt(qh, ck_ref[0, past_rows, :].astype(BF16)) + b1_ref[h]
        s2 = _dot_nt(qh, kn_ref[:, hs].astype(BF16)) + b2_ref[h]
        (p1, p2), inv = _softmax_parts([s1, s2])
        o = _dot(p1.astype(BF16), cv_ref[0, past_rows, :].astype(BF16)) + _dot(p2.astype(BF16), vn_ref[:, hs].astype(BF16))
        o_ref[:, hs] = (o * inv).astype(BF16)
    out_ref[...] = _dot(o_ref[...], wout_ref[...]) + h_ref[...]


def _band_sample(h, q, k, v, cache_k, cache_v, b1, b2, wout, hbuf, *, row0, n_seq, seq_len):
    width = ATT_HEADS * ATT_DH
    past = cache_k.shape[1] // ATT_HEADS
    blk0 = row0 // seq_len
    blk = lambda s: (blk0 + s, 0)
    const3 = lambda s: (0, 0, 0)
    return pl.pallas_call(
        _band_sample_kernel,
        out_shape=jax.ShapeDtypeStruct(hbuf.shape, F32),
        grid=(n_seq,),
        in_specs=[
            pl.BlockSpec((seq_len, D_MODEL), blk),
            pl.BlockSpec((seq_len, width), blk),
            pl.BlockSpec((seq_len, width), blk),
            pl.BlockSpec((seq_len, width), blk),
            pl.BlockSpec((1, past * ATT_HEADS, ATT_DH), lambda s: (s, 0, 0)),
            pl.BlockSpec((1, past * ATT_HEADS, ATT_DH), lambda s: (s, 0, 0)),
            pl.BlockSpec((ATT_HEADS, seq_len, past), const3),
            pl.BlockSpec((ATT_HEADS, seq_len, seq_len), const3),
            pl.BlockSpec((width, D_MODEL), lambda s: (0, 0)),
            pl.BlockSpec(memory_space=pl.ANY),
        ],
        out_specs=pl.BlockSpec((seq_len, D_MODEL), blk),
        scratch_shapes=[pltpu.VMEM((seq_len, width), BF16)],
        input_output_aliases={9: 0},
        compiler_params=pltpu.CompilerParams(dimension_semantics=("arbitrary",), vmem_limit_bytes=VMEM_LIMIT),
        name="band_attn_sample",
    )(h, q, k, v, cache_k, cache_v, b1, b2, wout, hbuf)


def _band_bias_table(rb):
    n_top = BAND_PAST - REL_CLIP + 1
    n_var = REL_CLIP + CHUNK - 2
    n_bot = ATT_WIN - n_top - n_var
    n = ATT_WIN + ATT_QB
    heads = rb.shape[0]
    top = jnp.broadcast_to(rb[:, n_var + 1:n_var + 2], (heads, n_top))
    by_d = jnp.concatenate([top, rb[:, 1:n_var + 1][:, ::-1], jnp.broadcast_to(rb[:, 0:1], (heads, n_bot)),
                            jnp.broadcast_to(rb[:, n_var + 1:n_var + 2], (heads, ATT_QB))], axis=1)
    shifted = jnp.tile(by_d, (1, ATT_QB))[:, :ATT_QB * (n - 1)].reshape(heads, ATT_QB, n - 1)
    return shifted[:, :, :ATT_WIN]


def kernel(x_prompt, x_sample, state_gla, state_conv, cache_band_k, cache_band_v, norm_mix_g, norm_ffn_g, norm_final_g, w_in_even, w_alpha_up, b_alpha, gla_norm_g, conv_w, conv_b, conv_ln_g, conv_ln_b, w_out_even, w_qkv_odd, rel_bias, w_out_odd, w_router_grp, b_router_grp, w_router_exp, b_router_exp, w_exp_gate, w_exp_up, w_exp_down):
    batch, seq, _ = x_prompt.shape
    dec_batch, dec_seq, _ = x_sample.shape
    n_prompt = batch * seq
    n_sample = dec_batch * dec_seq
    assert seq % ROW_BLOCK == 0 and n_sample == ROW_BLOCK and seq % ATT_QB == 0
    n_rows = n_prompt + n_sample
    prompt_blocks = n_prompt // ROW_BLOCK
    n_blocks = n_rows // ROW_BLOCK
    width = ATT_HEADS * ATT_DH
    past = cache_band_k.shape[2]

    w_in = w_in_even[0]
    c0, c1, c2, c3, c4, c5 = (QK_W, 2 * QK_W, 2 * QK_W + V_W, 2 * QK_W + 2 * V_W,
                              2 * QK_W + 2 * V_W + GLA_LOWRANK, 2 * QK_W + 2 * V_W + GLA_LOWRANK + CONV_CH)
    w_in_p = jnp.concatenate(
        [w_in[:, :c3], w_in[:, c4:], w_in[:, c3:c4], jnp.zeros((D_MODEL, LANES - GLA_LOWRANK), F32)], axis=1).astype(BF16)
    wau_p = jnp.concatenate([w_alpha_up[0], jnp.zeros((LANES - GLA_LOWRANK, QK_W), F32)], axis=0).astype(BF16)
    even_w = (norm_mix_g[0].reshape(1, D_MODEL), w_in_p, wau_p, b_alpha[0].reshape(1, QK_W),
              gla_norm_g[0].reshape(1, V_W), conv_w[0], conv_b[0].reshape(1, CONV_CH),
              conv_ln_g[0].reshape(1, CONV_CH), conv_ln_b[0].reshape(1, CONV_CH), w_out_even[0].astype(BF16))
    h1, gla_p, conv_p = _even_mixer(
        x_prompt.reshape(n_prompt, D_MODEL), None, even_w,
        jnp.zeros((batch, GLA_HEADS, GLA_DK, GLA_DV), F32), jnp.zeros((batch, HIST, CONV_CH), F32),
        n_rows_total=n_rows, block0=0, n_seq=batch, blocks_per_seq=seq // ROW_BLOCK,
        chunk=CHUNK, cps=ROW_BLOCK // CHUNK, spb=1)
    h1, gla_s, conv_s = _even_mixer(
        x_sample.reshape(n_sample, D_MODEL), h1, even_w, state_gla[0], state_conv[0],
        n_rows_total=n_rows, block0=prompt_blocks, n_seq=1, blocks_per_seq=1,
        chunk=dec_seq, cps=1, spb=dec_batch)

    def moe(h, layer, gfin, splits, xs_buf):
        return _hier_moe(h, norm_ffn_g[layer], w_router_grp[layer], b_router_grp[layer], w_router_exp[layer],
                         b_router_exp[layer], w_exp_gate, w_exp_up, w_exp_down, layer, gfin, splits, xs_buf)

    (h2,), xs_buf = moe(h1, 0, None, [(0, n_blocks)], jnp.zeros((_moe_tiles(n_rows) * GMM_TILE, D_MODEL), F32))

    q, k, v = _qkv(h2, norm_mix_g[1].reshape(1, D_MODEL), w_qkv_odd[0].astype(BF16), n_blocks)
    toe = _band_bias_table(rel_bias[0])
    r = jnp.arange(ATT_QB)[:, None]
    jb = jnp.arange(ATT_WIN)[None, :] - CHUNK * (r // CHUNK)
    in_band = jnp.logical_and(jb >= 0, jb < BAND_PAST + CHUNK)
    bm = jnp.where(in_band[None], toe, NEG_BIG)
    wout_odd = w_out_odd[0].astype(BF16)
    h3 = _band_prompt(h2, q, k, v, bm, wout_odd, n_rows_total=n_rows, n_seq=batch, blocks_per_seq=seq // ATT_QB)
    assert past == BAND_PAST and dec_seq <= CHUNK
    b1 = toe[:, :dec_seq, :past]
    b2 = toe[:, :dec_seq, past:past + dec_seq]
    h3 = _band_sample(h2, q, k, v, cache_band_k[0].reshape(dec_batch, past * ATT_HEADS, ATT_DH),
                      cache_band_v[0].reshape(dec_batch, past * ATT_HEADS, ATT_DH), b1, b2, wout_odd, h3,
                      row0=n_prompt, n_seq=dec_batch, seq_len=dec_seq)

    (y_prompt, y_sample), _ = moe(h3, 1, norm_final_g,
                                  [(0, prompt_blocks), (prompt_blocks, n_blocks - prompt_blocks)], xs_buf)

    rows = min(BAND_PAST, seq)
    tail = lambda a: jnp.stack([a[b * seq + seq - rows:(b + 1) * seq] for b in range(batch)]).reshape(
        batch, rows, ATT_HEADS, ATT_DH)
    kp, vp = tail(k), tail(v)
    ks = k[n_prompt:].reshape(dec_batch, dec_seq, ATT_HEADS, ATT_DH)
    vs = v[n_prompt:].reshape(dec_batch, dec_seq, ATT_HEADS, ATT_DH)
    return (y_prompt.reshape(batch, seq, D_MODEL), y_sample.reshape(dec_batch, dec_seq, D_MODEL),
            gla_p[None], gla_s[None], conv_p[None], conv_s[None], kp[None], vp[None], ks[None], vs[None])
```

```python
import functools

import jax
import jax.numpy as jnp
from jax import lax
from jax.experimental import pallas as pl
from jax.experimental.pallas import tpu as pltpu

F32 = jnp.float32
BF16 = jnp.bfloat16
HIGHEST = lax.Precision.HIGHEST

D_MODEL = 1024
CHUNK = 64
EPS = 1e-6
GLA_HEADS = 4
GLA_DK = 64
GLA_DV = 128
GLA_LOWRANK = 16
GLA_TAU = 16.0
CONV_CH = 512
CONV_WIDTH = 31
HIST = CONV_WIDTH - 1
ATT_HEADS = 16
ATT_DH = 64
BAND_CHUNKS_PAST = 8
BAND_PAST = BAND_CHUNKS_PAST * CHUNK
REL_CLIP = 256
N_GROUPS = 4
EXPERTS_PER_GROUP = 8
N_EXPERTS = N_GROUPS * EXPERTS_PER_GROUP
D_EXPERT = 512

LANES = 128
SUBLANES = 8
ROW_BLOCK = 512
CONV_TILE = 32
GMM_TILE = 256
ATT_QB = 256
ATT_WIN = ATT_QB + BAND_PAST
NEG_BIG = -1e30
VMEM_LIMIT = 56 * 1024 * 1024

QK_W = GLA_HEADS * GLA_DK
V_W = GLA_HEADS * GLA_DV
COL_Q = 0
COL_K = COL_Q + QK_W
COL_V = COL_K + QK_W
COL_GATE = COL_V + V_W
COL_CVAL = COL_GATE + V_W
COL_CGATE = COL_CVAL + CONV_CH
COL_ALR = COL_CGATE + CONV_CH
EVEN_COLS = COL_ALR + LANES


def _rmsnorm(x, g):
    return x * lax.rsqrt(jnp.mean(x * x, axis=-1, keepdims=True) + EPS) * g


def _sigmoid(x):
    return 1.0 / (1.0 + jnp.exp(-x))


def _dot(a, b):
    return jnp.dot(a, b, preferred_element_type=F32)


def _dot_nt(a, b):
    return lax.dot_general(a, b, (((1,), (1,)), ((), ())), preferred_element_type=F32)


def _dot_tn(a, b, precision=None):
    return lax.dot_general(a, b, (((0,), (0,)), ((), ())), preferred_element_type=F32, precision=precision)


def _even_mixer_kernel(*refs, chunk, cps, spb, carry, aliased):
    (x_ref, g_ref, win_ref, wau_ref, bal_ref, gng_ref, cw_ref, cb_ref, lng_ref, lnb_ref, wout_ref,
     s0_ref, c0_ref) = refs[:13]
    refs = refs[13 + (1 if aliased else 0):]
    h_ref, sfin_ref, cfin_ref, proj_ref, lg_ref, cum_ref, mix_ref, s_ref, ubuf_ref = refs
    j = pl.program_id(1)
    nj = pl.num_programs(1)
    seg = cps * chunk

    x = x_ref[...]
    xn = _rmsnorm(x, g_ref[...])
    proj_ref[...] = _dot(xn.astype(BF16), win_ref[...])
    alr = proj_ref[:, COL_ALR:COL_ALR + LANES]
    xa = _dot(alr.astype(BF16), wau_ref[...]) + bal_ref[...]
    lg_ref[...] = (jnp.minimum(xa, 0.0) - jnp.log1p(jnp.exp(-jnp.abs(xa)))) * (1.0 / GLA_TAU)

    row_i = lax.broadcasted_iota(jnp.int32, (chunk, QK_W), 0)
    tri_r = lax.broadcasted_iota(jnp.int32, (chunk, chunk), 0)
    tri_c = lax.broadcasted_iota(jnp.int32, (chunk, chunk), 1)
    causal = tri_r >= tri_c
    ones_cols = jnp.ones((chunk, LANES), F32)

    def chunk_body(ci, c):
        r0 = pl.multiple_of(ci * chunk, chunk)
        sq = ci // cps if spb > 1 else 0
        first = (ci % cps) == 0
        last = (ci % cps) == (cps - 1)
        if carry:
            first = jnp.logical_and(first, j == 0)
            last = jnp.logical_and(last, j == nj - 1)

        @pl.when(first)
        def _():
            s_ref[...] = s0_ref[sq]

        lg = lg_ref[pl.ds(r0, chunk), :]
        cum = lg
        shift = 1
        while shift < chunk:
            cum = cum + jnp.where(row_i >= shift, pltpu.roll(cum, shift, 0), 0.0)
            shift *= 2
        cum_ref[...] = cum
        tot = cum_ref[pl.ds(chunk - 1, 1), :]
        mid = cum_ref[pl.ds(chunk // 2 - 1, 1), :]
        tot_col = jnp.exp(_dot_tn(lg, ones_cols, precision=HIGHEST))
        e_in = jnp.exp(cum)
        e_q = jnp.exp(cum - mid)
        e_k = jnp.exp(mid - cum)
        e_s = jnp.exp(tot - cum)
        for h in range(GLA_HEADS):
            ks = slice(h * GLA_DK, (h + 1) * GLA_DK)
            vs = slice(h * GLA_DV, (h + 1) * GLA_DV)
            q = proj_ref[pl.ds(r0, chunk), COL_Q + h * GLA_DK:COL_Q + (h + 1) * GLA_DK] * (GLA_DK ** -0.5)
            k = proj_ref[pl.ds(r0, chunk), COL_K + h * GLA_DK:COL_K + (h + 1) * GLA_DK]
            v = proj_ref[pl.ds(r0, chunk), COL_V + h * GLA_DV:COL_V + (h + 1) * GLA_DV].astype(BF16)
            gate = proj_ref[pl.ds(r0, chunk), COL_GATE + h * GLA_DV:COL_GATE + (h + 1) * GLA_DV]
            s_old = s_ref[h]
            scores = _dot_nt((q * e_q[:, ks]).astype(BF16), (k * e_k[:, ks]).astype(BF16))
            scores = jnp.where(causal, scores, 0.0)
            o = _dot((q * e_in[:, ks]).astype(BF16), s_old.astype(BF16)) + _dot(scores.astype(BF16), v)
            s_ref[h] = tot_col[h * GLA_DK:(h + 1) * GLA_DK, :] * s_old + _dot_tn((k * e_s[:, ks]).astype(BF16), v)
            o = o * lax.rsqrt(jnp.mean(o * o, axis=-1, keepdims=True) + EPS)
            o = o * gng_ref[:, vs] * (gate * _sigmoid(gate))
            mix_ref[pl.ds(r0, chunk), vs] = o.astype(BF16)

        @pl.when(last)
        def _():
            sfin_ref[sq] = s_ref[...]

        return c

    lax.fori_loop(0, spb * cps, chunk_body, 0, unroll=2)

    tile = CONV_TILE

    def conv_tile(t0, out_r0):
        wv = ubuf_ref[pl.ds(t0, 2 * tile), :]
        acc = jnp.zeros((tile, CONV_CH), F32)
        for b in range(SUBLANES):
            sb = wv if b == 0 else pltpu.roll(wv, 2 * tile - b, 0)
            for a in range(tile // SUBLANES + 1):
                off = SUBLANES * a + b
                if 2 <= off <= HIST + 2:
                    acc = acc + sb[SUBLANES * a:SUBLANES * a + tile, :] * cw_ref[pl.ds(off - 2, 1), :]
        cv = acc + cb_ref[...]
        mu = jnp.mean(cv, axis=-1, keepdims=True)
        var = jnp.mean(jnp.square(cv - mu), axis=-1, keepdims=True)
        y = (cv - mu) * lax.rsqrt(var + EPS) * lng_ref[...] + lnb_ref[...]
        mix_ref[pl.ds(out_r0, tile), V_W:V_W + CONV_CH] = (y * _sigmoid(y)).astype(BF16)

    def conv_seg(sq, c):
        r0 = pl.multiple_of(sq * seg, seg) if spb > 1 else 0
        cval = proj_ref[pl.ds(r0, seg), COL_CVAL:COL_CVAL + CONV_CH]
        cgate = proj_ref[pl.ds(r0, seg), COL_CGATE:COL_CGATE + CONV_CH]

        def load_history():
            ubuf_ref[pl.ds(0, 8), :] = jnp.zeros((8, CONV_CH), F32)
            ubuf_ref[pl.ds(2, HIST), :] = c0_ref[sq]

        if carry:
            pl.when(j == 0)(load_history)
        else:
            load_history()
        ubuf_ref[pl.ds(CONV_TILE, seg), :] = cval * _sigmoid(cgate)
        if seg == tile:
            conv_tile(0, r0)
        else:
            def tile_body(t, cc):
                t0 = pl.multiple_of(t * tile, tile)
                conv_tile(t0, r0 + t0)
                return cc
            lax.fori_loop(0, seg // tile, tile_body, 0)
        hist = ubuf_ref[pl.ds(seg + 2, HIST), :]
        if carry:
            @pl.when(j == nj - 1)
            def _():
                cfin_ref[sq] = hist
            ubuf_ref[pl.ds(2, HIST), :] = hist
        else:
            cfin_ref[sq] = hist
        return c

    if spb > 1:
        lax.fori_loop(0, spb, conv_seg, 0)
    else:
        conv_seg(0, 0)

    h_ref[...] = _dot(mix_ref[...], wout_ref[...]) + x


def _even_mixer(x2d, hbuf, weights, s0, c0, *, n_rows_total, block0, n_seq, blocks_per_seq, chunk, cps, spb):
    carry = spb == 1
    aliased = hbuf is not None
    const = lambda s, j: (0, 0)
    seq_blk = (lambda s, j: (s, 0, 0, 0)) if carry else (lambda s, j: (0, 0, 0, 0))
    seq_blk3 = (lambda s, j: (s, 0, 0)) if carry else (lambda s, j: (0, 0, 0))
    n_state = 1 if carry else spb
    wspec = lambda shape: pl.BlockSpec(shape, const, pipeline_mode=pl.Buffered(1))
    in_specs = [
        pl.BlockSpec((ROW_BLOCK, D_MODEL), lambda s, j: (s * blocks_per_seq + j, 0)),
        wspec((1, D_MODEL)),
        wspec((D_MODEL, EVEN_COLS)),
        wspec((LANES, QK_W)),
        wspec((1, QK_W)),
        wspec((1, V_W)),
        wspec((CONV_WIDTH, CONV_CH)),
        wspec((1, CONV_CH)),
        wspec((1, CONV_CH)),
        wspec((1, CONV_CH)),
        wspec((V_W + CONV_CH, D_MODEL)),
        pl.BlockSpec((n_state, GLA_HEADS, GLA_DK, GLA_DV), seq_blk),
        pl.BlockSpec((n_state, HIST, CONV_CH), seq_blk3),
    ]
    args = [x2d, *weights, s0, c0]
    aliases = {}
    if aliased:
        in_specs.append(pl.BlockSpec(memory_space=pl.ANY))
        args.append(hbuf)
        aliases = {len(args) - 1: 0}
    n_all = s0.shape[0]
    out_shape = (
        jax.ShapeDtypeStruct((n_rows_total, D_MODEL), F32),
        jax.ShapeDtypeStruct((n_all, GLA_HEADS, GLA_DK, GLA_DV), F32),
        jax.ShapeDtypeStruct((n_all, HIST, CONV_CH), F32),
    )
    out_specs = (
        pl.BlockSpec((ROW_BLOCK, D_MODEL), lambda s, j: (block0 + s * blocks_per_seq + j, 0)),
        pl.BlockSpec((n_state, GLA_HEADS, GLA_DK, GLA_DV), seq_blk),
        pl.BlockSpec((n_state, HIST, CONV_CH), seq_blk3),
    )
    seg = cps * chunk
    scratch = [
        pltpu.VMEM((ROW_BLOCK, EVEN_COLS), F32),
        pltpu.VMEM((ROW_BLOCK, QK_W), F32),
        pltpu.VMEM((chunk, QK_W), F32),
        pltpu.VMEM((ROW_BLOCK, V_W + CONV_CH), BF16),
        pltpu.VMEM((GLA_HEADS, GLA_DK, GLA_DV), F32),
        pltpu.VMEM((CONV_TILE + seg, CONV_CH), F32),
    ]
    kern = functools.partial(_even_mixer_kernel, chunk=chunk, cps=cps, spb=spb, carry=carry, aliased=aliased)
    return pl.pallas_call(
        kern, out_shape=out_shape, grid=(n_seq, blocks_per_seq), in_specs=in_specs, out_specs=out_specs,
        scratch_shapes=scratch, input_output_aliases=aliases,
        compiler_params=pltpu.CompilerParams(dimension_semantics=("arbitrary", "arbitrary"), vmem_limit_bytes=VMEM_LIMIT),
        name="even_mixer_carry" if carry else "even_mixer_step",
    )(*args)


def _route_kernel(h_ref, g_ref, wr_ref, br_ref, mi_ref, mw_ref, cnt_ref, base_ref):
    i = pl.program_id(0)

    @pl.when(i == 0)
    def _():
        base_ref[...] = jnp.zeros_like(base_ref)

    xn = _rmsnorm(h_ref[...], g_ref[...])
    logits = jnp.dot(xn, wr_ref[...], precision=HIGHEST, preferred_element_type=F32) + br_ref[...]
    lane = lax.broadcasted_iota(jnp.int32, logits.shape, 1).astype(F32)
    far = float(1 << 20)
    gl = jnp.where(lane < N_GROUPS, logits, -jnp.inf)
    gmax = jnp.max(gl, axis=-1, keepdims=True)
    gidx = jnp.min(jnp.where(gl == gmax, lane, far), axis=-1, keepdims=True)
    gw = 1.0 / jnp.sum(jnp.exp(gl - gmax), axis=-1, keepdims=True)
    lo = N_GROUPS + gidx * EXPERTS_PER_GROUP
    el = jnp.where(jnp.logical_and(lane >= lo, lane < lo + EXPERTS_PER_GROUP), logits, -jnp.inf)
    m1 = jnp.max(el, axis=-1, keepdims=True)
    i1 = jnp.min(jnp.where(el == m1, lane, far), axis=-1, keepdims=True)
    el2 = jnp.where(lane == i1, -jnp.inf, el)
    m2 = jnp.max(el2, axis=-1, keepdims=True)
    i2 = jnp.min(jnp.where(el2 == m2, lane, far), axis=-1, keepdims=True)
    e2 = jnp.exp(m2 - m1)
    den = 1.0 + e2
    w0 = (1.0 / den) * gw
    w1 = (e2 / den) * gw
    id0 = i1 - N_GROUPS
    id1 = i2 - N_GROUPS
    oh0 = jnp.where(lane == id0, 1.0, 0.0)
    oh1 = jnp.where(lane == id1, 1.0, 0.0)
    rr = lax.broadcasted_iota(jnp.int32, (ROW_BLOCK, ROW_BLOCK), 0)
    cc = lax.broadcasted_iota(jnp.int32, (ROW_BLOCK, ROW_BLOCK), 1)
    below = jnp.where(rr > cc, 1.0, 0.0).astype(BF16)
    p0 = _dot(below, oh0.astype(BF16))
    p1 = _dot(below, oh1.astype(BF16))
    cnt0 = jnp.sum(oh0, axis=0, keepdims=True)
    cnt1 = jnp.sum(oh1, axis=0, keepdims=True)
    base = base_ref[...]
    rank0 = jnp.sum(oh0 * (p0 + base), axis=-1, keepdims=True)
    rank1 = jnp.sum(oh1 * (p1 + base + cnt0), axis=-1, keepdims=True)
    new_base = base + cnt0 + cnt1
    base_ref[...] = new_base
    cnt_ref[...] = new_base.astype(jnp.int32)
    meta = jnp.where(lane == 0, id0, jnp.where(lane == 1, id1, jnp.where(lane == 2, rank0, jnp.where(lane == 3, rank1, 0.0))))
    mi_ref[0] = meta.T[0:8, :].astype(jnp.int32)
    mw_ref[...] = jnp.where(lane == 0, w0, jnp.where(lane == 1, w1, 0.0))


def _route(h, g, wr, br, n_blocks):
    n_rows = n_blocks * ROW_BLOCK
    const = lambda i: (0, 0)
    return pl.pallas_call(
        _route_kernel,
        out_shape=(
            jax.ShapeDtypeStruct((n_blocks, 8, ROW_BLOCK), jnp.int32),
            jax.ShapeDtypeStruct((n_rows, LANES), F32),
            jax.ShapeDtypeStruct((1, LANES), jnp.int32),
        ),
        grid=(n_blocks,),
        in_specs=[
            pl.BlockSpec((ROW_BLOCK, D_MODEL), lambda i: (i, 0)),
            pl.BlockSpec((1, D_MODEL), const),
            pl.BlockSpec((D_MODEL, LANES), const),
            pl.BlockSpec((1, LANES), const),
        ],
        out_specs=(
            pl.BlockSpec((1, 8, ROW_BLOCK), lambda i: (i, 0, 0)),
            pl.BlockSpec((ROW_BLOCK, LANES), lambda i: (i, 0)),
            pl.BlockSpec((1, LANES), const),
        ),
        scratch_shapes=[pltpu.VMEM((1, LANES), F32)],
        compiler_params=pltpu.CompilerParams(dimension_semantics=("arbitrary",), vmem_limit_bytes=VMEM_LIMIT),
        name="moe_route",
    )(h, g, wr, br)


def _slot_index_kernel(dest_ref, tok0_ref, dst0_ref, tok_ref, dst_ref, sem, *, n_blocks, n_rows):
    init_tok = pltpu.make_async_copy(tok0_ref, tok_ref, sem.at[0])
    init_dst = pltpu.make_async_copy(dst0_ref, dst_ref, sem.at[1])
    init_tok.start()
    init_dst.start()
    init_tok.wait()
    init_dst.wait()

    def block(b, c):
        for k in range(2):
            def body(j, cc):
                d = dest_ref[b * (2 * ROW_BLOCK) + k * ROW_BLOCK + j]
                t = b * ROW_BLOCK + j
                tok_ref[d] = t
                dst_ref[d] = k * n_rows + t
                return cc

            lax.fori_loop(0, ROW_BLOCK, body, 0, unroll=8)
        return c

    lax.fori_loop(0, n_blocks, block, 0)


def _slot_index(dest_flat, tok0, dst0, n_blocks, n_rows):
    n_slots = tok0.shape[0]
    return pl.pallas_call(
        functools.partial(_slot_index_kernel, n_blocks=n_blocks, n_rows=n_rows),
        out_shape=(jax.ShapeDtypeStruct((n_slots,), jnp.int32), jax.ShapeDtypeStruct((n_slots,), jnp.int32)),
        in_specs=[pl.BlockSpec(memory_space=pltpu.SMEM), pl.BlockSpec(memory_space=pl.ANY),
                  pl.BlockSpec(memory_space=pl.ANY)],
        out_specs=(pl.BlockSpec(memory_space=pltpu.SMEM), pl.BlockSpec(memory_space=pltpu.SMEM)),
        scratch_shapes=[pltpu.SemaphoreType.DMA((2,))],
        name="moe_slot_index",
    )(dest_flat, tok0, dst0)


GMM_ISSUE_GROUPS = 8


def _gmm_kernel(te_ref, nv_ref, tok_ref, dst_ref, h_ref, g_ref, wg_ref, wu_ref, wd_ref, yp_ref,
                xbuf, ybuf, wgb, wub, wdb, gsem, ssem):
    i = pl.program_id(0)
    nv = nv_ref[0]
    n_tiles = pl.num_programs(0) - 1
    slot = i % 2
    other = 1 - slot

    def gather_copy(tile, r, buf):
        tok = tok_ref[tile * GMM_TILE + r]
        return pltpu.make_async_copy(h_ref.at[pl.ds(tok, 1), :], xbuf.at[buf, pl.ds(r, 1), :], gsem.at[buf])

    def scatter_copy(tile, r, buf):
        d = dst_ref[tile * GMM_TILE + r]
        return pltpu.make_async_copy(ybuf.at[buf, pl.ds(r, 1), :], yp_ref.at[pl.ds(d, 1), :], ssem.at[buf])

    def wait_tile(copy_of_row, buf):
        def body(r, c):
            copy_of_row(0, 0, buf).wait()
            return c

        lax.fori_loop(0, GMM_TILE, body, 0, unroll=16)

    @pl.when(i == 0)
    def _():
        ybuf[...] = jnp.zeros_like(ybuf)

        def body(r, c):
            gather_copy(0, r, 0).start()
            return c

        lax.fori_loop(0, GMM_TILE, body, 0, unroll=8)

    @pl.when(jnp.logical_and(i >= 1, i <= nv))
    def _():
        wait_tile(scatter_copy, slot)

    @pl.when(i < nv)
    def _():
        wait_tile(gather_copy, slot)
        prev = te_ref[jnp.maximum(i - 1, 0)]

        @pl.when(jnp.logical_or(i == 0, te_ref[i] != prev))
        def _():
            wgb[...] = wg_ref[0, 0].astype(BF16)
            wub[...] = wu_ref[0, 0].astype(BF16)
            wdb[...] = wd_ref[0, 0].astype(BF16)

        prev_tile = jnp.where(i >= 1, i - 1, n_tiles)
        rows_per_group = GMM_TILE // GMM_ISSUE_GROUPS

        def issue(group):
            for r in range(group * rows_per_group, (group + 1) * rows_per_group):
                gather_copy(i + 1, r, other).start()
                scatter_copy(prev_tile, r, other).start()

        x = _rmsnorm(xbuf[slot], g_ref[...]).astype(BF16)
        half = D_EXPERT // 2
        quarter = D_MODEL // 4
        a0 = _dot(x, wgb[:, 0:half])
        issue(0)
        a1 = _dot(x, wgb[:, half:D_EXPERT])
        issue(1)
        b0 = _dot(x, wub[:, 0:half])
        issue(2)
        b1 = _dot(x, wub[:, half:D_EXPERT])
        issue(3)
        he = jnp.concatenate([(a0 * _sigmoid(a0)) * b0, (a1 * _sigmoid(a1)) * b1], axis=1).astype(BF16)
        for n in range(4):
            ybuf[slot, :, n * quarter:(n + 1) * quarter] = _dot(he, wdb[:, n * quarter:(n + 1) * quarter])
            issue(4 + n)

    @pl.when(i == nv)
    def _():
        wait_tile(gather_copy, slot)

        def body(r, c):
            scatter_copy(nv - 1, r, other).start()
            return c

        lax.fori_loop(0, GMM_TILE, body, 0, unroll=8)
        wait_tile(scatter_copy, other)


def _gmm(tile_expert, n_valid, tok, dst, h, g, w_gate, w_up, w_down, layer):
    n_rows = h.shape[0]
    n_tiles = tile_expert.shape[0] - 1

    def w_idx(i, te, nv, tok, dst):
        return (layer, te[jnp.maximum(jnp.minimum(i, nv[0] - 1), 0)], 0, 0)

    return pl.pallas_call(
        _gmm_kernel,
        out_shape=jax.ShapeDtypeStruct((2 * n_rows + 2 * GMM_TILE, D_MODEL), F32),
        grid_spec=pltpu.PrefetchScalarGridSpec(
            num_scalar_prefetch=4,
            grid=(n_tiles + 1,),
            in_specs=[
                pl.BlockSpec(memory_space=pl.ANY),
                pl.BlockSpec((1, D_MODEL), lambda i, te, nv, tok, dst: (0, 0)),
                pl.BlockSpec((1, 1, D_MODEL, D_EXPERT), w_idx),
                pl.BlockSpec((1, 1, D_MODEL, D_EXPERT), w_idx),
                pl.BlockSpec((1, 1, D_EXPERT, D_MODEL), w_idx),
            ],
            out_specs=pl.BlockSpec(memory_space=pl.ANY),
            scratch_shapes=[
                pltpu.VMEM((2, GMM_TILE, D_MODEL), F32),
                pltpu.VMEM((2, GMM_TILE, D_MODEL), F32),
                pltpu.VMEM((D_MODEL, D_EXPERT), BF16),
                pltpu.VMEM((D_MODEL, D_EXPERT), BF16),
                pltpu.VMEM((D_EXPERT, D_MODEL), BF16),
                pltpu.SemaphoreType.DMA((2,)),
                pltpu.SemaphoreType.DMA((2,)),
            ],
        ),
        compiler_params=pltpu.CompilerParams(dimension_semantics=("arbitrary",), vmem_limit_bytes=VMEM_LIMIT),
        name="moe_gmm",
    )(tile_expert, n_valid, tok, dst, h, g, w_gate, w_up, w_down)


def _combine_kernel(*refs, final):
    if final:
        h_ref, mw_ref, y0_ref, y1_ref, gfin_ref, out_ref = refs
    else:
        h_ref, mw_ref, y0_ref, y1_ref, out_ref = refs
    mw = mw_ref[...]
    hn = h_ref[...] + (mw[:, 0:1] * y0_ref[...] + mw[:, 1:2] * y1_ref[...])
    out_ref[...] = _rmsnorm(hn, gfin_ref[...]) if final else hn


def _combine(h, mw, yp, gfin, *, block0, n_blocks):
    final = gfin is not None
    total_blocks = h.shape[0] // ROW_BLOCK
    in_specs = [
        pl.BlockSpec((ROW_BLOCK, D_MODEL), lambda i: (block0 + i, 0)),
        pl.BlockSpec((ROW_BLOCK, LANES), lambda i: (block0 + i, 0)),
        pl.BlockSpec((ROW_BLOCK, D_MODEL), lambda i: (block0 + i, 0)),
        pl.BlockSpec((ROW_BLOCK, D_MODEL), lambda i: (total_blocks + block0 + i, 0)),
    ]
    args = [h, mw, yp, yp]
    if final:
        in_specs.append(pl.BlockSpec((1, D_MODEL), lambda i: (0, 0)))
        args.append(gfin)
    return pl.pallas_call(
        functools.partial(_combine_kernel, final=final),
        out_shape=jax.ShapeDtypeStruct((n_blocks * ROW_BLOCK, D_MODEL), F32),
        grid=(n_blocks,),
        in_specs=in_specs,
        out_specs=pl.BlockSpec((ROW_BLOCK, D_MODEL), lambda i: (i, 0)),
        compiler_params=pltpu.CompilerParams(dimension_semantics=("arbitrary",), vmem_limit_bytes=VMEM_LIMIT),
        name="moe_combine_final" if final else "moe_combine",
    )(*args)


def _moe_tiles(n_rows):
    return (2 * n_rows + N_EXPERTS * (GMM_TILE - 1) + GMM_TILE - 1) // GMM_TILE


def _hier_moe(h, g, w_rg, b_rg, w_re, b_re, w_gate, w_up, w_down, layer, gfin, splits):
    n_rows = h.shape[0]
    n_blocks = n_rows // ROW_BLOCK
    n_re = N_GROUPS * EXPERTS_PER_GROUP
    wr = jnp.zeros((D_MODEL, LANES), F32)
    wr = wr.at[:, :N_GROUPS].set(w_rg).at[:, N_GROUPS:N_GROUPS + n_re].set(w_re.reshape(D_MODEL, n_re))
    br = jnp.zeros((1, LANES), F32)
    br = br.at[0, :N_GROUPS].set(b_rg).at[0, N_GROUPS:N_GROUPS + n_re].set(b_re.reshape(n_re))
    g2 = g.reshape(1, D_MODEL)
    meta_i, meta_w, counts = _route(h, g2, wr, br, n_blocks)

    n_tiles = _moe_tiles(n_rows)
    cnt = counts[0, :N_EXPERTS]
    padded = ((cnt + GMM_TILE - 1) // GMM_TILE) * GMM_TILE
    ends = jnp.cumsum(padded)
    offs = ends - padded
    tile_start = jnp.arange(n_tiles + 1, dtype=jnp.int32) * GMM_TILE
    tile_expert = jnp.minimum(jnp.sum((tile_start[:, None] >= ends[None, :]).astype(jnp.int32), axis=1), N_EXPERTS - 1)
    n_valid = (ends[-1:] // GMM_TILE).astype(jnp.int32)
    eid = meta_i[:, 0:2, :]
    seg_start = jnp.sum(jnp.where(eid[..., None] == jnp.arange(N_EXPERTS, dtype=jnp.int32), offs, 0), axis=-1)
    dest = (seg_start + meta_i[:, 2:4, :]).astype(jnp.int32).reshape(-1)
    slots = jnp.arange((n_tiles + 1) * GMM_TILE, dtype=jnp.int32)
    tok, dst = _slot_index(dest, jnp.zeros_like(slots), 2 * n_rows + slots % (2 * GMM_TILE), n_blocks, n_rows)

    yp = _gmm(tile_expert.astype(jnp.int32), n_valid, tok, dst, h, g2, w_gate, w_up, w_down, layer)
    gf = None if gfin is None else gfin.reshape(1, D_MODEL)
    return [_combine(h, meta_w, yp, gf, block0=b0, n_blocks=nb) for (b0, nb) in splits]


def _qkv_kernel(h_ref, g_ref, w_ref, q_ref, k_ref, v_ref):
    xn = _rmsnorm(h_ref[...], g_ref[...]).astype(BF16)
    width = ATT_HEADS * ATT_DH
    q_ref[...] = (_dot(xn, w_ref[:, 0:width]) * (ATT_DH ** -0.5)).astype(BF16)
    k_ref[...] = _dot(xn, w_ref[:, width:2 * width])
    v_ref[...] = _dot(xn, w_ref[:, 2 * width:3 * width])


def _qkv(h, g, w_bf16, n_blocks):
    n_rows = n_blocks * ROW_BLOCK
    width = ATT_HEADS * ATT_DH
    row_blk = lambda i: (i, 0)
    return pl.pallas_call(
        _qkv_kernel,
        out_shape=(
            jax.ShapeDtypeStruct((n_rows, width), BF16),
            jax.ShapeDtypeStruct((n_rows, width), F32),
            jax.ShapeDtypeStruct((n_rows, width), F32),
        ),
        grid=(n_blocks,),
        in_specs=[
            pl.BlockSpec((ROW_BLOCK, D_MODEL), row_blk),
            pl.BlockSpec((1, D_MODEL), lambda i: (0, 0)),
            pl.BlockSpec((D_MODEL, 3 * width), lambda i: (0, 0), pipeline_mode=pl.Buffered(1)),
        ],
        out_specs=(
            pl.BlockSpec((ROW_BLOCK, width), row_blk),
            pl.BlockSpec((ROW_BLOCK, width), row_blk),
            pl.BlockSpec((ROW_BLOCK, width), row_blk),
        ),
        compiler_params=pltpu.CompilerParams(dimension_semantics=("arbitrary",), vmem_limit_bytes=VMEM_LIMIT),
        name="qkv_proj",
    )(h, g, w_bf16)


def _softmax_parts(parts):
    m = functools.reduce(jnp.maximum, [jnp.max(s, axis=-1, keepdims=True) for s in parts])
    ps = [jnp.exp(s - m) for s in parts]
    inv = 1.0 / functools.reduce(jnp.add, [jnp.sum(p, axis=-1, keepdims=True) for p in ps])
    return ps, inv


def _band_prompt_kernel(h_ref, q_ref, ka_ref, kb_ref, kc_ref, va_ref, vb_ref, vc_ref, bm_ref, wout_ref,
                        out_ref, kw_ref, vw_ref, o_ref):
    i = pl.program_id(1)
    for b, (k_ref, v_ref) in enumerate(((ka_ref, va_ref), (kb_ref, vb_ref), (kc_ref, vc_ref))):
        kw_ref[pl.ds(b * ATT_QB, ATT_QB), :] = k_ref[...].astype(BF16)
        vw_ref[pl.ds(b * ATT_QB, ATT_QB), :] = v_ref[...].astype(BF16)
    col = lax.broadcasted_iota(jnp.int32, (1, ATT_WIN), 1)
    first_valid = (2 - jnp.minimum(i, 2)) * ATT_QB
    col_mask = jnp.where(col >= first_valid, 0.0, NEG_BIG)
    for h in range(ATT_HEADS):
        hs = slice(h * ATT_DH, (h + 1) * ATT_DH)
        s = _dot_nt(q_ref[:, hs], kw_ref[:, hs]) + bm_ref[h] + col_mask
        (p,), inv = _softmax_parts([s])
        o_ref[:, hs] = (_dot(p.astype(BF16), vw_ref[:, hs]) * inv).astype(BF16)
    out_ref[...] = _dot(o_ref[...], wout_ref[...]) + h_ref[...]


def _band_prompt(h, q, k, v, bm, wout, *, n_rows_total, n_seq, blocks_per_seq):
    width = ATT_HEADS * ATT_DH
    blk = lambda s, i: (s * blocks_per_seq + i, 0)
    back = lambda n: (lambda s, i: (s * blocks_per_seq + jnp.maximum(i - n, 0), 0))
    kv_spec = lambda n: pl.BlockSpec((ATT_QB, width), back(n))
    return pl.pallas_call(
        _band_prompt_kernel,
        out_shape=jax.ShapeDtypeStruct((n_rows_total, D_MODEL), F32),
        grid=(n_seq, blocks_per_seq),
        in_specs=[
            pl.BlockSpec((ATT_QB, D_MODEL), blk),
            pl.BlockSpec((ATT_QB, width), blk),
            kv_spec(2), kv_spec(1), kv_spec(0),
            kv_spec(2), kv_spec(1), kv_spec(0),
            pl.BlockSpec((ATT_HEADS, ATT_QB, ATT_WIN), lambda s, i: (0, 0, 0), pipeline_mode=pl.Buffered(1)),
            pl.BlockSpec((width, D_MODEL), lambda s, i: (0, 0), pipeline_mode=pl.Buffered(1)),
        ],
        out_specs=pl.BlockSpec((ATT_QB, D_MODEL), blk),
        scratch_shapes=[
            pltpu.VMEM((ATT_WIN, width), BF16),
            pltpu.VMEM((ATT_WIN, width), BF16),
            pltpu.VMEM((ATT_QB, width), BF16),
        ],
        compiler_params=pltpu.CompilerParams(dimension_semantics=("arbitrary", "arbitrary"), vmem_limit_bytes=VMEM_LIMIT),
        name="band_attn_prompt",
    )(h, q, k, k, k, v, v, v, bm, wout)


def _band_sample_kernel(h_ref, q_ref, kn_ref, vn_ref, ck_ref, cv_ref, b1_ref, b2_ref, wout_ref, hbuf_ref,
                        out_ref, o_ref):
    del hbuf_ref
    for h in range(ATT_HEADS):
        hs = slice(h * ATT_DH, (h + 1) * ATT_DH)
        qh = q_ref[:, hs]
        s1 = _dot(qh, ck_ref[0, h].astype(BF16)) + b1_ref[h]
        s2 = _dot_nt(qh, kn_ref[:, hs].astype(BF16)) + b2_ref[h]
        (p1, p2), inv = _softmax_parts([s1, s2])
        o = _dot_nt(p1.astype(BF16), cv_ref[0, h].astype(BF16)) + _dot(p2.astype(BF16), vn_ref[:, hs].astype(BF16))
        o_ref[:, hs] = (o * inv).astype(BF16)
    out_ref[...] = _dot(o_ref[...], wout_ref[...]) + h_ref[...]


def _band_sample(h, q, k, v, cache_k, cache_v, b1, b2, wout, hbuf, *, row0, n_seq, seq_len):
    width = ATT_HEADS * ATT_DH
    past = cache_k.shape[3]
    blk0 = row0 // seq_len
    blk = lambda s: (blk0 + s, 0)
    const3 = lambda s: (0, 0, 0)
    return pl.pallas_call(
        _band_sample_kernel,
        out_shape=jax.ShapeDtypeStruct(hbuf.shape, F32),
        grid=(n_seq,),
        in_specs=[
            pl.BlockSpec((seq_len, D_MODEL), blk),
            pl.BlockSpec((seq_len, width), blk),
            pl.BlockSpec((seq_len, width), blk),
            pl.BlockSpec((seq_len, width), blk),
            pl.BlockSpec((1, ATT_HEADS, ATT_DH, past), lambda s: (s, 0, 0, 0)),
            pl.BlockSpec((1, ATT_HEADS, ATT_DH, past), lambda s: (s, 0, 0, 0)),
            pl.BlockSpec((ATT_HEADS, seq_len, past), const3),
            pl.BlockSpec((ATT_HEADS, seq_len, seq_len), const3),
            pl.BlockSpec((width, D_MODEL), lambda s: (0, 0)),
            pl.BlockSpec(memory_space=pl.ANY),
        ],
        out_specs=pl.BlockSpec((seq_len, D_MODEL), blk),
        scratch_shapes=[pltpu.VMEM((seq_len, width), BF16)],
        input_output_aliases={9: 0},
        compiler_params=pltpu.CompilerParams(dimension_semantics=("arbitrary",), vmem_limit_bytes=VMEM_LIMIT),
        name="band_attn_sample",
    )(h, q, k, v, cache_k, cache_v, b1, b2, wout, hbuf)


def _band_bias_table(rb):
    n_top = BAND_PAST - REL_CLIP + 1
    n_var = REL_CLIP + CHUNK - 2
    n_bot = ATT_WIN - n_top - n_var
    n = ATT_WIN + ATT_QB
    heads = rb.shape[0]
    top = jnp.broadcast_to(rb[:, n_var + 1:n_var + 2], (heads, n_top))
    by_d = jnp.concatenate([top, rb[:, 1:n_var + 1][:, ::-1], jnp.broadcast_to(rb[:, 0:1], (heads, n_bot)),
                            jnp.broadcast_to(rb[:, n_var + 1:n_var + 2], (heads, ATT_QB))], axis=1)
    shifted = jnp.tile(by_d, (1, ATT_QB))[:, :ATT_QB * (n - 1)].reshape(heads, ATT_QB, n - 1)
    return shifted[:, :, :ATT_WIN]


def kernel(x_prompt, x_sample, state_gla, state_conv, cache_band_k, cache_band_v, norm_mix_g, norm_ffn_g, norm_final_g, w_in_even, w_alpha_up, b_alpha, gla_norm_g, conv_w, conv_b, conv_ln_g, conv_ln_b, w_out_even, w_qkv_odd, rel_bias, w_out_odd, w_router_grp, b_router_grp, w_router_exp, b_router_exp, w_exp_gate, w_exp_up, w_exp_down):
    batch, seq, _ = x_prompt.shape
    dec_batch, dec_seq, _ = x_sample.shape
    n_prompt = batch * seq
    n_sample = dec_batch * dec_seq
    assert seq % ROW_BLOCK == 0 and n_sample == ROW_BLOCK and seq % ATT_QB == 0
    n_rows = n_prompt + n_sample
    prompt_blocks = n_prompt // ROW_BLOCK
    n_blocks = n_rows // ROW_BLOCK
    width = ATT_HEADS * ATT_DH
    past = cache_band_k.shape[2]

    w_in = w_in_even[0]
    c0, c1, c2, c3, c4, c5 = (QK_W, 2 * QK_W, 2 * QK_W + V_W, 2 * QK_W + 2 * V_W,
                              2 * QK_W + 2 * V_W + GLA_LOWRANK, 2 * QK_W + 2 * V_W + GLA_LOWRANK + CONV_CH)
    w_in_p = jnp.concatenate(
        [w_in[:, :c3], w_in[:, c4:], w_in[:, c3:c4], jnp.zeros((D_MODEL, LANES - GLA_LOWRANK), F32)], axis=1).astype(BF16)
    wau_p = jnp.concatenate([w_alpha_up[0], jnp.zeros((LANES - GLA_LOWRANK, QK_W), F32)], axis=0).astype(BF16)
    even_w = (norm_mix_g[0].reshape(1, D_MODEL), w_in_p, wau_p, b_alpha[0].reshape(1, QK_W),
              gla_norm_g[0].reshape(1, V_W), conv_w[0], conv_b[0].reshape(1, CONV_CH),
              conv_ln_g[0].reshape(1, CONV_CH), conv_ln_b[0].reshape(1, CONV_CH), w_out_even[0].astype(BF16))
    h1, gla_p, conv_p = _even_mixer(
        x_prompt.reshape(n_prompt, D_MODEL), None, even_w,
        jnp.zeros((batch, GLA_HEADS, GLA_DK, GLA_DV), F32), jnp.zeros((batch, HIST, CONV_CH), F32),
        n_rows_total=n_rows, block0=0, n_seq=batch, blocks_per_seq=seq // ROW_BLOCK,
        chunk=CHUNK, cps=ROW_BLOCK // CHUNK, spb=1)
    h1, gla_s, conv_s = _even_mixer(
        x_sample.reshape(n_sample, D_MODEL), h1, even_w, state_gla[0], state_conv[0],
        n_rows_total=n_rows, block0=prompt_blocks, n_seq=1, blocks_per_seq=1,
        chunk=dec_seq, cps=1, spb=dec_batch)

    def moe(h, layer, gfin, splits):
        return _hier_moe(h, norm_ffn_g[layer], w_router_grp[layer], b_router_grp[layer], w_router_exp[layer],
                         b_router_exp[layer], w_exp_gate, w_exp_up, w_exp_down, layer, gfin, splits)

    (h2,) = moe(h1, 0, None, [(0, n_blocks)])

    q, k, v = _qkv(h2, norm_mix_g[1].reshape(1, D_MODEL), w_qkv_odd[0].astype(BF16), n_blocks)
    toe = _band_bias_table(rel_bias[0])
    r = jnp.arange(ATT_QB)[:, None]
    jb = jnp.arange(ATT_WIN)[None, :] - CHUNK * (r // CHUNK)
    in_band = jnp.logical_and(jb >= 0, jb < BAND_PAST + CHUNK)
    bm = jnp.where(in_band[None], toe, NEG_BIG)
    wout_odd = w_out_odd[0].astype(BF16)
    h3 = _band_prompt(h2, q, k, v, bm, wout_odd, n_rows_total=n_rows, n_seq=batch, blocks_per_seq=seq // ATT_QB)
    assert past == BAND_PAST and dec_seq <= CHUNK
    b1 = toe[:, :dec_seq, :past]
    b2 = toe[:, :dec_seq, past:past + dec_seq]
    h3 = _band_sample(h2, q, k, v, jnp.transpose(cache_band_k[0], (0, 2, 3, 1)),
                      jnp.transpose(cache_band_v[0], (0, 2, 3, 1)), b1, b2, wout_odd, h3,
                      row0=n_prompt, n_seq=dec_batch, seq_len=dec_seq)

    y_prompt, y_sample = moe(h3, 1, norm_final_g, [(0, prompt_blocks), (prompt_blocks, n_blocks - prompt_blocks)])

    rows = min(BAND_PAST, seq)
    tail = lambda a: jnp.stack([a[b * seq + seq - rows:(b + 1) * seq] for b in range(batch)]).reshape(
        batch, rows, ATT_HEADS, ATT_DH)
    kp, vp = tail(k), tail(v)
    ks = k[n_prompt:].reshape(dec_batch, dec_seq, ATT_HEADS, ATT_DH)
    vs = v[n_prompt:].reshape(dec_batch, dec_seq, ATT_HEADS, ATT_DH)
    return (y_prompt.reshape(batch, seq, D_MODEL), y_sample.reshape(dec_batch, dec_seq, D_MODEL),
            gla_p[None], gla_s[None], conv_p[None], conv_s[None], kp[None], vp[None], ks[None], vs[None])
```

```python
import functools

import jax
import jax.numpy as jnp
from jax import lax
from jax.experimental import pallas as pl
from jax.experimental.pallas import tpu as pltpu

F32 = jnp.float32
BF16 = jnp.bfloat16
HIGHEST = lax.Precision.HIGHEST

D_MODEL = 1024
CHUNK = 64
EPS = 1e-6
GLA_HEADS = 4
GLA_DK = 64
GLA_DV = 128
GLA_LOWRANK = 16
GLA_TAU = 16.0
CONV_CH = 512
CONV_WIDTH = 31
HIST = CONV_WIDTH - 1
ATT_HEADS = 16
ATT_DH = 64
BAND_CHUNKS_PAST = 8
BAND_PAST = BAND_CHUNKS_PAST * CHUNK
REL_CLIP = 256
N_GROUPS = 4
EXPERTS_PER_GROUP = 8
N_EXPERTS = N_GROUPS * EXPERTS_PER_GROUP
D_EXPERT = 512

LANES = 128
SUBLANES = 8
ROW_BLOCK = 512
CONV_TILE = 32
GMM_TILE = 256
ATT_QB = 256
ATT_WIN = ATT_QB + BAND_PAST
NEG_BIG = -1e30
VMEM_LIMIT = 56 * 1024 * 1024

QK_W = GLA_HEADS * GLA_DK
V_W = GLA_HEADS * GLA_DV
COL_Q = 0
COL_K = COL_Q + QK_W
COL_V = COL_K + QK_W
COL_GATE = COL_V + V_W
COL_CVAL = COL_GATE + V_W
COL_CGATE = COL_CVAL + CONV_CH
COL_ALR = COL_CGATE + CONV_CH
EVEN_COLS = COL_ALR + LANES


def _rmsnorm(x, g):
    return x * lax.rsqrt(jnp.mean(x * x, axis=-1, keepdims=True) + EPS) * g


def _sigmoid(x):
    return 1.0 / (1.0 + jnp.exp(-x))


def _dot(a, b):
    return jnp.dot(a, b, preferred_element_type=F32)


def _dot_nt(a, b):
    return lax.dot_general(a, b, (((1,), (1,)), ((), ())), preferred_element_type=F32)


def _dot_tn(a, b, precision=None):
    return lax.dot_general(a, b, (((0,), (0,)), ((), ())), preferred_element_type=F32, precision=precision)


def _even_mixer_kernel(*refs, chunk, cps, spb, carry, aliased):
    (x_ref, g_ref, win_ref, wau_ref, bal_ref, gng_ref, cw_ref, cb_ref, lng_ref, lnb_ref, wout_ref,
     s0_ref, c0_ref) = refs[:13]
    refs = refs[13 + (1 if aliased else 0):]
    h_ref, sfin_ref, cfin_ref, proj_ref, lg_ref, cum_ref, mix_ref, s_ref, ubuf_ref = refs
    j = pl.program_id(1)
    nj = pl.num_programs(1)
    seg = cps * chunk

    x = x_ref[...]
    xn = _rmsnorm(x, g_ref[...])
    proj_ref[...] = _dot(xn.astype(BF16), win_ref[...])
    alr = proj_ref[:, COL_ALR:COL_ALR + LANES]
    xa = _dot(alr.astype(BF16), wau_ref[...]) + bal_ref[...]
    lg_ref[...] = (jnp.minimum(xa, 0.0) - jnp.log1p(jnp.exp(-jnp.abs(xa)))) * (1.0 / GLA_TAU)

    row_i = lax.broadcasted_iota(jnp.int32, (chunk, QK_W), 0)
    tri_r = lax.broadcasted_iota(jnp.int32, (chunk, chunk), 0)
    tri_c = lax.broadcasted_iota(jnp.int32, (chunk, chunk), 1)
    causal = tri_r >= tri_c
    ones_cols = jnp.ones((chunk, LANES), F32)

    def chunk_body(ci, c):
        r0 = pl.multiple_of(ci * chunk, chunk)
        sq = ci // cps if spb > 1 else 0
        first = (ci % cps) == 0
        last = (ci % cps) == (cps - 1)
        if carry:
            first = jnp.logical_and(first, j == 0)
            last = jnp.logical_and(last, j == nj - 1)

        @pl.when(first)
        def _():
            s_ref[...] = s0_ref[sq]

        lg = lg_ref[pl.ds(r0, chunk), :]
        cum = lg
        shift = 1
        while shift < chunk:
            cum = cum + jnp.where(row_i >= shift, pltpu.roll(cum, shift, 0), 0.0)
            shift *= 2
        cum_ref[...] = cum
        tot = cum_ref[pl.ds(chunk - 1, 1), :]
        mid = cum_ref[pl.ds(chunk // 2 - 1, 1), :]
        tot_col = jnp.exp(_dot_tn(lg, ones_cols, precision=HIGHEST))
        e_in = jnp.exp(cum)
        e_q = jnp.exp(cum - mid)
        e_k = jnp.exp(mid - cum)
        e_s = jnp.exp(tot - cum)
        for h in range(GLA_HEADS):
            ks = slice(h * GLA_DK, (h + 1) * GLA_DK)
            vs = slice(h * GLA_DV, (h + 1) * GLA_DV)
            q = proj_ref[pl.ds(r0, chunk), COL_Q + h * GLA_DK:COL_Q + (h + 1) * GLA_DK] * (GLA_DK ** -0.5)
            k = proj_ref[pl.ds(r0, chunk), COL_K + h * GLA_DK:COL_K + (h + 1) * GLA_DK]
            v = proj_ref[pl.ds(r0, chunk), COL_V + h * GLA_DV:COL_V + (h + 1) * GLA_DV].astype(BF16)
            gate = proj_ref[pl.ds(r0, chunk), COL_GATE + h * GLA_DV:COL_GATE + (h + 1) * GLA_DV]
            s_old = s_ref[h]
            scores = _dot_nt((q * e_q[:, ks]).astype(BF16), (k * e_k[:, ks]).astype(BF16))
            scores = jnp.where(causal, scores, 0.0)
            o = _dot((q * e_in[:, ks]).astype(BF16), s_old.astype(BF16)) + _dot(scores.astype(BF16), v)
            s_ref[h] = tot_col[h * GLA_DK:(h + 1) * GLA_DK, :] * s_old + _dot_tn((k * e_s[:, ks]).astype(BF16), v)
            o = o * lax.rsqrt(jnp.mean(o * o, axis=-1, keepdims=True) + EPS)
            o = o * gng_ref[:, vs] * (gate * _sigmoid(gate))
            mix_ref[pl.ds(r0, chunk), vs] = o.astype(BF16)

        @pl.when(last)
        def _():
            sfin_ref[sq] = s_ref[...]

        return c

    lax.fori_loop(0, spb * cps, chunk_body, 0, unroll=2)

    tile = CONV_TILE

    def conv_tile(t0, out_r0):
        wv = ubuf_ref[pl.ds(t0, 2 * tile), :]
        acc = jnp.zeros((tile, CONV_CH), F32)
        for b in range(SUBLANES):
            sb = wv if b == 0 else pltpu.roll(wv, 2 * tile - b, 0)
            for a in range(tile // SUBLANES + 1):
                off = SUBLANES * a + b
                if 2 <= off <= HIST + 2:
                    acc = acc + sb[SUBLANES * a:SUBLANES * a + tile, :] * cw_ref[pl.ds(off - 2, 1), :]
        proj_ref[pl.ds(out_r0, tile), COL_CVAL:COL_CVAL + CONV_CH] = acc + cb_ref[...]

    def conv_norm(r0):
        cv = proj_ref[pl.ds(r0, seg), COL_CVAL:COL_CVAL + CONV_CH]
        mu = jnp.mean(cv, axis=-1, keepdims=True)
        var = jnp.mean(jnp.square(cv - mu), axis=-1, keepdims=True)
        y = (cv - mu) * lax.rsqrt(var + EPS) * lng_ref[...] + lnb_ref[...]
        mix_ref[pl.ds(r0, seg), V_W:V_W + CONV_CH] = (y * _sigmoid(y)).astype(BF16)

    def conv_seg(sq, c):
        r0 = pl.multiple_of(sq * seg, seg) if spb > 1 else 0
        cval = proj_ref[pl.ds(r0, seg), COL_CVAL:COL_CVAL + CONV_CH]
        cgate = proj_ref[pl.ds(r0, seg), COL_CGATE:COL_CGATE + CONV_CH]

        def load_history():
            ubuf_ref[pl.ds(0, 8), :] = jnp.zeros((8, CONV_CH), F32)
            ubuf_ref[pl.ds(2, HIST), :] = c0_ref[sq]

        if carry:
            pl.when(j == 0)(load_history)
        else:
            load_history()
        ubuf_ref[pl.ds(CONV_TILE, seg), :] = cval * _sigmoid(cgate)
        if seg == tile:
            conv_tile(0, r0)
        else:
            def tile_body(t, cc):
                t0 = pl.multiple_of(t * tile, tile)
                conv_tile(t0, r0 + t0)
                return cc
            lax.fori_loop(0, seg // tile, tile_body, 0)
        conv_norm(r0)
        hist = ubuf_ref[pl.ds(seg + 2, HIST), :]
        if carry:
            @pl.when(j == nj - 1)
            def _():
                cfin_ref[sq] = hist
            ubuf_ref[pl.ds(2, HIST), :] = hist
        else:
            cfin_ref[sq] = hist
        return c

    if spb > 1:
        lax.fori_loop(0, spb, conv_seg, 0)
    else:
        conv_seg(0, 0)

    h_ref[...] = _dot(mix_ref[...], wout_ref[...]) + x


def _even_mixer(x2d, hbuf, weights, s0, c0, *, n_rows_total, block0, n_seq, blocks_per_seq, chunk, cps, spb):
    carry = spb == 1
    aliased = hbuf is not None
    const = lambda s, j: (0, 0)
    seq_blk = (lambda s, j: (s, 0, 0, 0)) if carry else (lambda s, j: (0, 0, 0, 0))
    seq_blk3 = (lambda s, j: (s, 0, 0)) if carry else (lambda s, j: (0, 0, 0))
    n_state = 1 if carry else spb
    wspec = lambda shape: pl.BlockSpec(shape, const, pipeline_mode=pl.Buffered(1))
    in_specs = [
        pl.BlockSpec((ROW_BLOCK, D_MODEL), lambda s, j: (s * blocks_per_seq + j, 0)),
        wspec((1, D_MODEL)),
        wspec((D_MODEL, EVEN_COLS)),
        wspec((LANES, QK_W)),
        wspec((1, QK_W)),
        wspec((1, V_W)),
        wspec((CONV_WIDTH, CONV_CH)),
        wspec((1, CONV_CH)),
        wspec((1, CONV_CH)),
        wspec((1, CONV_CH)),
        wspec((V_W + CONV_CH, D_MODEL)),
        pl.BlockSpec((n_state, GLA_HEADS, GLA_DK, GLA_DV), seq_blk),
        pl.BlockSpec((n_state, HIST, CONV_CH), seq_blk3),
    ]
    args = [x2d, *weights, s0, c0]
    aliases = {}
    if aliased:
        in_specs.append(pl.BlockSpec(memory_space=pl.ANY))
        args.append(hbuf)
        aliases = {len(args) - 1: 0}
    n_all = s0.shape[0]
    out_shape = (
        jax.ShapeDtypeStruct((n_rows_total, D_MODEL), F32),
        jax.ShapeDtypeStruct((n_all, GLA_HEADS, GLA_DK, GLA_DV), F32),
        jax.ShapeDtypeStruct((n_all, HIST, CONV_CH), F32),
    )
    out_specs = (
        pl.BlockSpec((ROW_BLOCK, D_MODEL), lambda s, j: (block0 + s * blocks_per_seq + j, 0)),
        pl.BlockSpec((n_state, GLA_HEADS, GLA_DK, GLA_DV), seq_blk),
        pl.BlockSpec((n_state, HIST, CONV_CH), seq_blk3),
    )
    seg = cps * chunk
    scratch = [
        pltpu.VMEM((ROW_BLOCK, EVEN_COLS), F32),
        pltpu.VMEM((ROW_BLOCK, QK_W), F32),
        pltpu.VMEM((chunk, QK_W), F32),
        pltpu.VMEM((ROW_BLOCK, V_W + CONV_CH), BF16),
        pltpu.VMEM((GLA_HEADS, GLA_DK, GLA_DV), F32),
        pltpu.VMEM((CONV_TILE + seg, CONV_CH), F32),
    ]
    kern = functools.partial(_even_mixer_kernel, chunk=chunk, cps=cps, spb=spb, carry=carry, aliased=aliased)
    return pl.pallas_call(
        kern, out_shape=out_shape, grid=(n_seq, blocks_per_seq), in_specs=in_specs, out_specs=out_specs,
        scratch_shapes=scratch, input_output_aliases=aliases,
        compiler_params=pltpu.CompilerParams(dimension_semantics=("arbitrary", "arbitrary"), vmem_limit_bytes=VMEM_LIMIT),
        name="even_mixer_carry" if carry else "even_mixer_step",
    )(*args)


def _to_row_tiles(ref, x):
    rows = x.shape[0]
    for c in range(x.shape[1] // LANES):
        ref[pl.ds(c, rows, stride=SUBLANES), :] = x[:, c * LANES:(c + 1) * LANES]


def _from_row_tiles(ref, rows, width=D_MODEL):
    return jnp.concatenate([ref[pl.ds(c, rows, stride=SUBLANES), :] for c in range(width // LANES)], axis=1)


def _route_kernel(h_ref, g_ref, wr_ref, br_ref, xn_ref, mi_ref, mw_ref, cnt_ref, base_ref):
    i = pl.program_id(0)

    @pl.when(i == 0)
    def _():
        base_ref[...] = jnp.zeros_like(base_ref)

    xn = _rmsnorm(h_ref[...], g_ref[...])
    _to_row_tiles(xn_ref, xn)
    logits = jnp.dot(xn, wr_ref[...], precision=HIGHEST, preferred_element_type=F32) + br_ref[...]
    lane = lax.broadcasted_iota(jnp.int32, logits.shape, 1).astype(F32)
    far = float(1 << 20)
    gl = jnp.where(lane < N_GROUPS, logits, -jnp.inf)
    gmax = jnp.max(gl, axis=-1, keepdims=True)
    gidx = jnp.min(jnp.where(gl == gmax, lane, far), axis=-1, keepdims=True)
    gw = 1.0 / jnp.sum(jnp.exp(gl - gmax), axis=-1, keepdims=True)
    lo = N_GROUPS + gidx * EXPERTS_PER_GROUP
    el = jnp.where(jnp.logical_and(lane >= lo, lane < lo + EXPERTS_PER_GROUP), logits, -jnp.inf)
    m1 = jnp.max(el, axis=-1, keepdims=True)
    i1 = jnp.min(jnp.where(el == m1, lane, far), axis=-1, keepdims=True)
    el2 = jnp.where(lane == i1, -jnp.inf, el)
    m2 = jnp.max(el2, axis=-1, keepdims=True)
    i2 = jnp.min(jnp.where(el2 == m2, lane, far), axis=-1, keepdims=True)
    e2 = jnp.exp(m2 - m1)
    den = 1.0 + e2
    w0 = (1.0 / den) * gw
    w1 = (e2 / den) * gw
    id0 = i1 - N_GROUPS
    id1 = i2 - N_GROUPS
    oh0 = jnp.where(lane == id0, 1.0, 0.0)
    oh1 = jnp.where(lane == id1, 1.0, 0.0)
    rr = lax.broadcasted_iota(jnp.int32, (ROW_BLOCK, ROW_BLOCK), 0)
    cc = lax.broadcasted_iota(jnp.int32, (ROW_BLOCK, ROW_BLOCK), 1)
    below = jnp.where(rr > cc, 1.0, 0.0).astype(BF16)
    p0 = _dot(below, oh0.astype(BF16))
    p1 = _dot(below, oh1.astype(BF16))
    cnt0 = jnp.sum(oh0, axis=0, keepdims=True)
    cnt1 = jnp.sum(oh1, axis=0, keepdims=True)
    base = base_ref[...]
    rank0 = jnp.sum(oh0 * (p0 + base), axis=-1, keepdims=True)
    rank1 = jnp.sum(oh1 * (p1 + base + cnt0), axis=-1, keepdims=True)
    new_base = base + cnt0 + cnt1
    base_ref[...] = new_base
    cnt_ref[...] = new_base.astype(jnp.int32)
    meta = jnp.where(lane == 0, id0, jnp.where(lane == 1, id1, jnp.where(lane == 2, rank0, jnp.where(lane == 3, rank1, 0.0))))
    mi_ref[0] = meta.T[0:8, :].astype(jnp.int32)
    mw_ref[...] = jnp.where(lane == 0, w0, jnp.where(lane == 1, w1, 0.0))


def _route(h, g, wr, br, n_blocks):
    n_rows = n_blocks * ROW_BLOCK
    const = lambda i: (0, 0)
    return pl.pallas_call(
        _route_kernel,
        out_shape=(
            jax.ShapeDtypeStruct((n_rows * SUBLANES, LANES), F32),
            jax.ShapeDtypeStruct((n_blocks, 8, ROW_BLOCK), jnp.int32),
            jax.ShapeDtypeStruct((n_rows, LANES), F32),
            jax.ShapeDtypeStruct((1, LANES), jnp.int32),
        ),
        grid=(n_blocks,),
        in_specs=[
            pl.BlockSpec((ROW_BLOCK, D_MODEL), lambda i: (i, 0)),
            pl.BlockSpec((1, D_MODEL), const),
            pl.BlockSpec((D_MODEL, LANES), const),
            pl.BlockSpec((1, LANES), const),
        ],
        out_specs=(
            pl.BlockSpec((ROW_BLOCK * SUBLANES, LANES), lambda i: (i, 0)),
            pl.BlockSpec((1, 8, ROW_BLOCK), lambda i: (i, 0, 0)),
            pl.BlockSpec((ROW_BLOCK, LANES), lambda i: (i, 0)),
            pl.BlockSpec((1, LANES), const),
        ),
        scratch_shapes=[pltpu.VMEM((1, LANES), F32)],
        compiler_params=pltpu.CompilerParams(dimension_semantics=("arbitrary",), vmem_limit_bytes=VMEM_LIMIT),
        name="moe_route",
    )(h, g, wr, br)


def _slot_index_kernel(dest_ref, tok0_ref, dst0_ref, tok_ref, dst_ref, sem, *, n_blocks, n_rows):
    init_tok = pltpu.make_async_copy(tok0_ref, tok_ref, sem.at[0])
    init_dst = pltpu.make_async_copy(dst0_ref, dst_ref, sem.at[1])
    init_tok.start()
    init_dst.start()
    init_tok.wait()
    init_dst.wait()

    def block(b, c):
        for k in range(2):
            def body(j, cc):
                d = dest_ref[b * (2 * ROW_BLOCK) + k * ROW_BLOCK + j]
                t = b * ROW_BLOCK + j
                tok_ref[d] = t * SUBLANES
                dst_ref[d] = (k * n_rows + t) * SUBLANES
                return cc

            lax.fori_loop(0, ROW_BLOCK, body, 0, unroll=8)
        return c

    lax.fori_loop(0, n_blocks, block, 0)


def _slot_index(dest_flat, tok0, dst0, n_blocks, n_rows):
    n_slots = tok0.shape[0]
    return pl.pallas_call(
        functools.partial(_slot_index_kernel, n_blocks=n_blocks, n_rows=n_rows),
        out_shape=(jax.ShapeDtypeStruct((n_slots,), jnp.int32), jax.ShapeDtypeStruct((n_slots,), jnp.int32)),
        in_specs=[pl.BlockSpec(memory_space=pltpu.SMEM), pl.BlockSpec(memory_space=pl.ANY),
                  pl.BlockSpec(memory_space=pl.ANY)],
        out_specs=(pl.BlockSpec(memory_space=pltpu.SMEM), pl.BlockSpec(memory_space=pltpu.SMEM)),
        scratch_shapes=[pltpu.SemaphoreType.DMA((2,))],
        name="moe_slot_index",
    )(dest_flat, tok0, dst0)


GMM_ISSUE_GROUPS = 8


def _gmm_kernel(te_ref, nv_ref, tok_ref, dst_ref, xn_ref, wg_ref, wu_ref, wd_ref, yp_ref,
                xbuf, ybuf, wgb, wub, wdb, gsem, ssem):
    i = pl.program_id(0)
    nv = nv_ref[0]
    n_tiles = pl.num_programs(0) - 1
    slot = i % 2
    other = 1 - slot

    def gather_copy(tile, r, buf):
        src = pl.multiple_of(tok_ref[tile * GMM_TILE + r], SUBLANES)
        return pltpu.make_async_copy(xn_ref.at[pl.ds(src, SUBLANES), :],
                                     xbuf.at[buf, pl.ds(r * SUBLANES, SUBLANES), :], gsem.at[buf])

    def scatter_copy(tile, r, buf):
        d = pl.multiple_of(dst_ref[tile * GMM_TILE + r], SUBLANES)
        return pltpu.make_async_copy(ybuf.at[buf, pl.ds(r * SUBLANES, SUBLANES), :],
                                     yp_ref.at[pl.ds(d, SUBLANES), :], ssem.at[buf])

    def wait_tile(copy_of_row, buf):
        def body(r, c):
            copy_of_row(0, 0, buf).wait()
            return c

        lax.fori_loop(0, GMM_TILE, body, 0, unroll=16)

    @pl.when(i == 0)
    def _():
        ybuf[...] = jnp.zeros_like(ybuf)

        def body(r, c):
            gather_copy(0, r, 0).start()
            return c

        lax.fori_loop(0, GMM_TILE, body, 0, unroll=8)

    @pl.when(jnp.logical_and(i >= 1, i <= nv))
    def _():
        wait_tile(scatter_copy, slot)

    @pl.when(i < nv)
    def _():
        wait_tile(gather_copy, slot)
        prev = te_ref[jnp.maximum(i - 1, 0)]

        @pl.when(jnp.logical_or(i == 0, te_ref[i] != prev))
        def _():
            wgb[...] = wg_ref[0, 0].astype(BF16)
            wub[...] = wu_ref[0, 0].astype(BF16)
            wdb[...] = wd_ref[0, 0].astype(BF16)

        prev_tile = jnp.where(i >= 1, i - 1, n_tiles)
        rows_per_group = GMM_TILE // GMM_ISSUE_GROUPS

        def issue(group):
            for r in range(group * rows_per_group, (group + 1) * rows_per_group):
                gather_copy(i + 1, r, other).start()
                scatter_copy(prev_tile, r, other).start()

        x = _from_row_tiles(xbuf.at[slot], GMM_TILE).astype(BF16)
        half = D_EXPERT // 2
        quarter = D_MODEL // 4
        a0 = _dot(x, wgb[:, 0:half])
        issue(0)
        a1 = _dot(x, wgb[:, half:D_EXPERT])
        issue(1)
        b0 = _dot(x, wub[:, 0:half])
        issue(2)
        b1 = _dot(x, wub[:, half:D_EXPERT])
        issue(3)
        he = jnp.concatenate([(a0 * _sigmoid(a0)) * b0, (a1 * _sigmoid(a1)) * b1], axis=1).astype(BF16)
        for n in range(4):
            yn = _dot(he, wdb[:, n * quarter:(n + 1) * quarter])
            for c in range(quarter // LANES):
                ybuf[slot, pl.ds(n * (quarter // LANES) + c, GMM_TILE, stride=SUBLANES), :] = yn[:, c * LANES:(c + 1) * LANES]
            issue(4 + n)

    @pl.when(i == nv)
    def _():
        wait_tile(gather_copy, slot)

        def body(r, c):
            scatter_copy(nv - 1, r, other).start()
            return c

        lax.fori_loop(0, GMM_TILE, body, 0, unroll=8)
        wait_tile(scatter_copy, other)


def _gmm(tile_expert, n_valid, tok, dst, xn_tiles, w_gate, w_up, w_down, layer):
    n_rows = xn_tiles.shape[0] // SUBLANES
    n_tiles = tile_expert.shape[0] - 1

    def w_idx(i, te, nv, tok, dst):
        return (layer, te[jnp.maximum(jnp.minimum(i, nv[0] - 1), 0)], 0, 0)

    return pl.pallas_call(
        _gmm_kernel,
        out_shape=jax.ShapeDtypeStruct(((2 * n_rows + 2 * GMM_TILE) * SUBLANES, LANES), F32),
        grid_spec=pltpu.PrefetchScalarGridSpec(
            num_scalar_prefetch=4,
            grid=(n_tiles + 1,),
            in_specs=[
                pl.BlockSpec(memory_space=pl.ANY),
                pl.BlockSpec((1, 1, D_MODEL, D_EXPERT), w_idx),
                pl.BlockSpec((1, 1, D_MODEL, D_EXPERT), w_idx),
                pl.BlockSpec((1, 1, D_EXPERT, D_MODEL), w_idx),
            ],
            out_specs=pl.BlockSpec(memory_space=pl.ANY),
            scratch_shapes=[
                pltpu.VMEM((2, GMM_TILE * SUBLANES, LANES), F32),
                pltpu.VMEM((2, GMM_TILE * SUBLANES, LANES), F32),
                pltpu.VMEM((D_MODEL, D_EXPERT), BF16),
                pltpu.VMEM((D_MODEL, D_EXPERT), BF16),
                pltpu.VMEM((D_EXPERT, D_MODEL), BF16),
                pltpu.SemaphoreType.DMA((2,)),
                pltpu.SemaphoreType.DMA((2,)),
            ],
        ),
        compiler_params=pltpu.CompilerParams(dimension_semantics=("arbitrary",), vmem_limit_bytes=VMEM_LIMIT),
        name="moe_gmm",
    )(tile_expert, n_valid, tok, dst, xn_tiles, w_gate, w_up, w_down)


def _combine_kernel(*refs, final):
    if final:
        h_ref, mw_ref, y0_ref, y1_ref, gfin_ref, out_ref = refs
    else:
        h_ref, mw_ref, y0_ref, y1_ref, out_ref = refs
    mw = mw_ref[...]
    y0 = _from_row_tiles(y0_ref, ROW_BLOCK)
    y1 = _from_row_tiles(y1_ref, ROW_BLOCK)
    hn = h_ref[...] + (mw[:, 0:1] * y0 + mw[:, 1:2] * y1)
    out_ref[...] = _rmsnorm(hn, gfin_ref[...]) if final else hn


def _combine(h, mw, yp, gfin, *, block0, n_blocks):
    final = gfin is not None
    total_blocks = h.shape[0] // ROW_BLOCK
    in_specs = [
        pl.BlockSpec((ROW_BLOCK, D_MODEL), lambda i: (block0 + i, 0)),
        pl.BlockSpec((ROW_BLOCK, LANES), lambda i: (block0 + i, 0)),
        pl.BlockSpec((ROW_BLOCK * SUBLANES, LANES), lambda i: (block0 + i, 0)),
        pl.BlockSpec((ROW_BLOCK * SUBLANES, LANES), lambda i: (total_blocks + block0 + i, 0)),
    ]
    args = [h, mw, yp, yp]
    if final:
        in_specs.append(pl.BlockSpec((1, D_MODEL), lambda i: (0, 0)))
        args.append(gfin)
    return pl.pallas_call(
        functools.partial(_combine_kernel, final=final),
        out_shape=jax.ShapeDtypeStruct((n_blocks * ROW_BLOCK, D_MODEL), F32),
        grid=(n_blocks,),
        in_specs=in_specs,
        out_specs=pl.BlockSpec((ROW_BLOCK, D_MODEL), lambda i: (i, 0)),
        compiler_params=pltpu.CompilerParams(dimension_semantics=("arbitrary",), vmem_limit_bytes=VMEM_LIMIT),
        name="moe_combine_final" if final else "moe_combine",
    )(*args)


def _moe_tiles(n_rows):
    return (2 * n_rows + N_EXPERTS * (GMM_TILE - 1) + GMM_TILE - 1) // GMM_TILE


def _hier_moe(h, g, w_rg, b_rg, w_re, b_re, w_gate, w_up, w_down, layer, gfin, splits):
    n_rows = h.shape[0]
    n_blocks = n_rows // ROW_BLOCK
    n_re = N_GROUPS * EXPERTS_PER_GROUP
    wr = jnp.zeros((D_MODEL, LANES), F32)
    wr = wr.at[:, :N_GROUPS].set(w_rg).at[:, N_GROUPS:N_GROUPS + n_re].set(w_re.reshape(D_MODEL, n_re))
    br = jnp.zeros((1, LANES), F32)
    br = br.at[0, :N_GROUPS].set(b_rg).at[0, N_GROUPS:N_GROUPS + n_re].set(b_re.reshape(n_re))
    xn_tiles, meta_i, meta_w, counts = _route(h, g.reshape(1, D_MODEL), wr, br, n_blocks)

    n_tiles = _moe_tiles(n_rows)
    cnt = counts[0, :N_EXPERTS]
    padded = ((cnt + GMM_TILE - 1) // GMM_TILE) * GMM_TILE
    ends = jnp.cumsum(padded)
    offs = ends - padded
    tile_start = jnp.arange(n_tiles + 1, dtype=jnp.int32) * GMM_TILE
    tile_expert = jnp.minimum(jnp.sum((tile_start[:, None] >= ends[None, :]).astype(jnp.int32), axis=1), N_EXPERTS - 1)
    n_valid = (ends[-1:] // GMM_TILE).astype(jnp.int32)
    eid = meta_i[:, 0:2, :]
    seg_start = jnp.sum(jnp.where(eid[..., None] == jnp.arange(N_EXPERTS, dtype=jnp.int32), offs, 0), axis=-1)
    dest = (seg_start + meta_i[:, 2:4, :]).astype(jnp.int32).reshape(-1)
    slots = jnp.arange((n_tiles + 1) * GMM_TILE, dtype=jnp.int32)
    scratch_rows = 2 * n_rows + slots % (2 * GMM_TILE)
    tok, dst = _slot_index(dest, jnp.zeros_like(slots), scratch_rows * SUBLANES, n_blocks, n_rows)

    yp = _gmm(tile_expert.astype(jnp.int32), n_valid, tok, dst, xn_tiles, w_gate, w_up, w_down, layer)
    gf = None if gfin is None else gfin.reshape(1, D_MODEL)
    return [_combine(h, meta_w, yp, gf, block0=b0, n_blocks=nb) for (b0, nb) in splits]


def _qkv_kernel(h_ref, g_ref, w_ref, q_ref, k_ref, v_ref):
    xn = _rmsnorm(h_ref[...], g_ref[...]).astype(BF16)
    width = ATT_HEADS * ATT_DH
    q_ref[...] = (_dot(xn, w_ref[:, 0:width]) * (ATT_DH ** -0.5)).astype(BF16)
    k_ref[...] = _dot(xn, w_ref[:, width:2 * width])
    v_ref[...] = _dot(xn, w_ref[:, 2 * width:3 * width])


def _qkv(h, g, w_bf16, n_blocks):
    n_rows = n_blocks * ROW_BLOCK
    width = ATT_HEADS * ATT_DH
    row_blk = lambda i: (i, 0)
    return pl.pallas_call(
        _qkv_kernel,
        out_shape=(
            jax.ShapeDtypeStruct((n_rows, width), BF16),
            jax.ShapeDtypeStruct((n_rows, width), F32),
            jax.ShapeDtypeStruct((n_rows, width), F32),
        ),
        grid=(n_blocks,),
        in_specs=[
            pl.BlockSpec((ROW_BLOCK, D_MODEL), row_blk),
            pl.BlockSpec((1, D_MODEL), lambda i: (0, 0)),
            pl.BlockSpec((D_MODEL, 3 * width), lambda i: (0, 0), pipeline_mode=pl.Buffered(1)),
        ],
        out_specs=(
            pl.BlockSpec((ROW_BLOCK, width), row_blk),
            pl.BlockSpec((ROW_BLOCK, width), row_blk),
            pl.BlockSpec((ROW_BLOCK, width), row_blk),
        ),
        compiler_params=pltpu.CompilerParams(dimension_semantics=("arbitrary",), vmem_limit_bytes=VMEM_LIMIT),
        name="qkv_proj",
    )(h, g, w_bf16)


def _softmax_parts(parts):
    m = functools.reduce(jnp.maximum, [jnp.max(s, axis=-1, keepdims=True) for s in parts])
    ps = [jnp.exp(s - m) for s in parts]
    inv = 1.0 / functools.reduce(jnp.add, [jnp.sum(p, axis=-1, keepdims=True) for p in ps])
    return ps, inv


def _band_prompt_kernel(h_ref, q_ref, ka_ref, kb_ref, kc_ref, va_ref, vb_ref, vc_ref, bm_ref, wout_ref,
                        out_ref, kw_ref, vw_ref, o_ref):
    i = pl.program_id(1)
    for b, (k_ref, v_ref) in enumerate(((ka_ref, va_ref), (kb_ref, vb_ref), (kc_ref, vc_ref))):
        kw_ref[pl.ds(b * ATT_QB, ATT_QB), :] = k_ref[...].astype(BF16)
        vw_ref[pl.ds(b * ATT_QB, ATT_QB), :] = v_ref[...].astype(BF16)
    col = lax.broadcasted_iota(jnp.int32, (1, ATT_WIN), 1)
    first_valid = (2 - jnp.minimum(i, 2)) * ATT_QB
    col_mask = jnp.where(col >= first_valid, 0.0, NEG_BIG)
    for h in range(ATT_HEADS):
        hs = slice(h * ATT_DH, (h + 1) * ATT_DH)
        s = _dot_nt(q_ref[:, hs], kw_ref[:, hs]) + bm_ref[h] + col_mask
        (p,), inv = _softmax_parts([s])
        o_ref[:, hs] = (_dot(p.astype(BF16), vw_ref[:, hs]) * inv).astype(BF16)
    out_ref[...] = _dot(o_ref[...], wout_ref[...]) + h_ref[...]


def _band_prompt(h, q, k, v, bm, wout, *, n_rows_total, n_seq, blocks_per_seq):
    width = ATT_HEADS * ATT_DH
    blk = lambda s, i: (s * blocks_per_seq + i, 0)
    back = lambda n: (lambda s, i: (s * blocks_per_seq + jnp.maximum(i - n, 0), 0))
    kv_spec = lambda n: pl.BlockSpec((ATT_QB, width), back(n))
    return pl.pallas_call(
        _band_prompt_kernel,
        out_shape=jax.ShapeDtypeStruct((n_rows_total, D_MODEL), F32),
        grid=(n_seq, blocks_per_seq),
        in_specs=[
            pl.BlockSpec((ATT_QB, D_MODEL), blk),
            pl.BlockSpec((ATT_QB, width), blk),
            kv_spec(2), kv_spec(1), kv_spec(0),
            kv_spec(2), kv_spec(1), kv_spec(0),
            pl.BlockSpec((ATT_HEADS, ATT_QB, ATT_WIN), lambda s, i: (0, 0, 0), pipeline_mode=pl.Buffered(1)),
            pl.BlockSpec((width, D_MODEL), lambda s, i: (0, 0), pipeline_mode=pl.Buffered(1)),
        ],
        out_specs=pl.BlockSpec((ATT_QB, D_MODEL), blk),
        scratch_shapes=[
            pltpu.VMEM((ATT_WIN, width), BF16),
            pltpu.VMEM((ATT_WIN, width), BF16),
            pltpu.VMEM((ATT_QB, width), BF16),
        ],
        compiler_params=pltpu.CompilerParams(dimension_semantics=("arbitrary", "arbitrary"), vmem_limit_bytes=VMEM_LIMIT),
        name="band_attn_prompt",
    )(h, q, k, k, k, v, v, v, bm, wout)


def _band_sample_kernel(h_ref, q_ref, kn_ref, vn_ref, ck_ref, cv_ref, b1_ref, b2_ref, wout_ref, hbuf_ref,
                        out_ref, o_ref):
    del hbuf_ref
    for h in range(ATT_HEADS):
        hs = slice(h * ATT_DH, (h + 1) * ATT_DH)
        qh = q_ref[:, hs]
        s1 = _dot(qh, ck_ref[0, h].astype(BF16)) + b1_ref[h]
        s2 = _dot_nt(qh, kn_ref[:, hs].astype(BF16)) + b2_ref[h]
        (p1, p2), inv = _softmax_parts([s1, s2])
        o = _dot_nt(p1.astype(BF16), cv_ref[0, h].astype(BF16)) + _dot(p2.astype(BF16), vn_ref[:, hs].astype(BF16))
        o_ref[:, hs] = (o * inv).astype(BF16)
    out_ref[...] = _dot(o_ref[...], wout_ref[...]) + h_ref[...]


def _band_sample(h, q, k, v, cache_k, cache_v, b1, b2, wout, hbuf, *, row0, n_seq, seq_len):
    width = ATT_HEADS * ATT_DH
    past = cache_k.shape[3]
    blk0 = row0 // seq_len
    blk = lambda s: (blk0 + s, 0)
    const3 = lambda s: (0, 0, 0)
    return pl.pallas_call(
        _band_sample_kernel,
        out_shape=jax.ShapeDtypeStruct(hbuf.shape, F32),
        grid=(n_seq,),
        in_specs=[
            pl.BlockSpec((seq_len, D_MODEL), blk),
            pl.BlockSpec((seq_len, width), blk),
            pl.BlockSpec((seq_len, width), blk),
            pl.BlockSpec((seq_len, width), blk),
            pl.BlockSpec((1, ATT_HEADS, ATT_DH, past), lambda s: (s, 0, 0, 0)),
            pl.BlockSpec((1, ATT_HEADS, ATT_DH, past), lambda s: (s, 0, 0, 0)),
            pl.BlockSpec((ATT_HEADS, seq_len, past), const3),
            pl.BlockSpec((ATT_HEADS, seq_len, seq_len), const3),
            pl.BlockSpec((width, D_MODEL), lambda s: (0, 0)),
            pl.BlockSpec(memory_space=pl.ANY),
        ],
        out_specs=pl.BlockSpec((seq_len, D_MODEL), blk),
        scratch_shapes=[pltpu.VMEM((seq_len, width), BF16)],
        input_output_aliases={9: 0},
        compiler_params=pltpu.CompilerParams(dimension_semantics=("arbitrary",), vmem_limit_bytes=VMEM_LIMIT),
        name="band_attn_sample",
    )(h, q, k, v, cache_k, cache_v, b1, b2, wout, hbuf)


def _band_bias_table(rb):
    n_top = BAND_PAST - REL_CLIP + 1
    n_var = REL_CLIP + CHUNK - 2
    n_bot = ATT_WIN - n_top - n_var
    n = ATT_WIN + ATT_QB
    heads = rb.shape[0]
    top = jnp.broadcast_to(rb[:, n_var + 1:n_var + 2], (heads, n_top))
    by_d = jnp.concatenate([top, rb[:, 1:n_var + 1][:, ::-1], jnp.broadcast_to(rb[:, 0:1], (heads, n_bot)),
                            jnp.broadcast_to(rb[:, n_var + 1:n_var + 2], (heads, ATT_QB))], axis=1)
    shifted = jnp.tile(by_d, (1, ATT_QB))[:, :ATT_QB * (n - 1)].reshape(heads, ATT_QB, n - 1)
    return shifted[:, :, :ATT_WIN]


def kernel(x_prompt, x_sample, state_gla, state_conv, cache_band_k, cache_band_v, norm_mix_g, norm_ffn_g, norm_final_g, w_in_even, w_alpha_up, b_alpha, gla_norm_g, conv_w, conv_b, conv_ln_g, conv_ln_b, w_out_even, w_qkv_odd, rel_bias, w_out_odd, w_router_grp, b_router_grp, w_router_exp, b_router_exp, w_exp_gate, w_exp_up, w_exp_down):
    batch, seq, _ = x_prompt.shape
    dec_batch, dec_seq, _ = x_sample.shape
    n_prompt = batch * seq
    n_sample = dec_batch * dec_seq
    assert seq % ROW_BLOCK == 0 and n_sample == ROW_BLOCK and seq % ATT_QB == 0
    n_rows = n_prompt + n_sample
    prompt_blocks = n_prompt // ROW_BLOCK
    n_blocks = n_rows // ROW_BLOCK
    width = ATT_HEADS * ATT_DH
    past = cache_band_k.shape[2]

    w_in = w_in_even[0]
    c0, c1, c2, c3, c4, c5 = (QK_W, 2 * QK_W, 2 * QK_W + V_W, 2 * QK_W + 2 * V_W,
                              2 * QK_W + 2 * V_W + GLA_LOWRANK, 2 * QK_W + 2 * V_W + GLA_LOWRANK + CONV_CH)
    w_in_p = jnp.concatenate(
        [w_in[:, :c3], w_in[:, c4:], w_in[:, c3:c4], jnp.zeros((D_MODEL, LANES - GLA_LOWRANK), F32)], axis=1).astype(BF16)
    wau_p = jnp.concatenate([w_alpha_up[0], jnp.zeros((LANES - GLA_LOWRANK, QK_W), F32)], axis=0).astype(BF16)
    even_w = (norm_mix_g[0].reshape(1, D_MODEL), w_in_p, wau_p, b_alpha[0].reshape(1, QK_W),
              gla_norm_g[0].reshape(1, V_W), conv_w[0], conv_b[0].reshape(1, CONV_CH),
              conv_ln_g[0].reshape(1, CONV_CH), conv_ln_b[0].reshape(1, CONV_CH), w_out_even[0].astype(BF16))
    h1, gla_p, conv_p = _even_mixer(
        x_prompt.reshape(n_prompt, D_MODEL), None, even_w,
        jnp.zeros((batch, GLA_HEADS, GLA_DK, GLA_DV), F32), jnp.zeros((batch, HIST, CONV_CH), F32),
        n_rows_total=n_rows, block0=0, n_seq=batch, blocks_per_seq=seq // ROW_BLOCK,
        chunk=CHUNK, cps=ROW_BLOCK // CHUNK, spb=1)
    h1, gla_s, conv_s = _even_mixer(
        x_sample.reshape(n_sample, D_MODEL), h1, even_w, state_gla[0], state_conv[0],
        n_rows_total=n_rows, block0=prompt_blocks, n_seq=1, blocks_per_seq=1,
        chunk=dec_seq, cps=1, spb=dec_batch)

    def moe(h, layer, gfin, splits):
        return _hier_moe(h, norm_ffn_g[layer], w_router_grp[layer], b_router_grp[layer], w_router_exp[layer],
                         b_router_exp[layer], w_exp_gate, w_exp_up, w_exp_down, layer, gfin, splits)

    (h2,) = moe(h1, 0, None, [(0, n_blocks)])

    q, k, v = _qkv(h2, norm_mix_g[1].reshape(1, D_MODEL), w_qkv_odd[0].astype(BF16), n_blocks)
    toe = _band_bias_table(rel_bias[0])
    r = jnp.arange(ATT_QB)[:, None]
    jb = jnp.arange(ATT_WIN)[None, :] - CHUNK * (r // CHUNK)
    in_band = jnp.logical_and(jb >= 0, jb < BAND_PAST + CHUNK)
    bm = jnp.where(in_band[None], toe, NEG_BIG)
    wout_odd = w_out_odd[0].astype(BF16)
    h3 = _band_prompt(h2, q, k, v, bm, wout_odd, n_rows_total=n_rows, n_seq=batch, blocks_per_seq=seq // ATT_QB)
    assert past == BAND_PAST and dec_seq <= CHUNK
    b1 = toe[:, :dec_seq, :past]
    b2 = toe[:, :dec_seq, past:past + dec_seq]
    h3 = _band_sample(h2, q, k, v, jnp.transpose(cache_band_k[0], (0, 2, 3, 1)),
                      jnp.transpose(cache_band_v[0], (0, 2, 3, 1)), b1, b2, wout_odd, h3,
                      row0=n_prompt, n_seq=dec_batch, seq_len=dec_seq)

    y_prompt, y_sample = moe(h3, 1, norm_final_g, [(0, prompt_blocks), (prompt_blocks, n_blocks - prompt_blocks)])

    rows = min(BAND_PAST, seq)
    tail = lambda a: jnp.stack([a[b * seq + seq - rows:(b + 1) * seq] for b in range(batch)]).reshape(
        batch, rows, ATT_HEADS, ATT_DH)
    kp, vp = tail(k), tail(v)
    ks = k[n_prompt:].reshape(dec_batch, dec_seq, ATT_HEADS, ATT_DH)
    vs = v[n_prompt:].reshape(dec_batch, dec_seq, ATT_HEADS, ATT_DH)
    return (y_prompt.reshape(batch, seq, D_MODEL), y_sample.reshape(dec_batch, dec_seq, D_MODEL),
            gla_p[None], gla_s[None], conv_p[None], conv_s[None], kp[None], vp[None], ks[None], vs[None])
```

```python
import functools

import jax
import jax.numpy as jnp
from jax import lax
from jax.experimental import pallas as pl
from jax.experimental.pallas import tpu as pltpu

F32 = jnp.float32
BF16 = jnp.bfloat16
HIGHEST = lax.Precision.HIGHEST

D_MODEL = 1024
CHUNK = 64
EPS = 1e-6
GLA_HEADS = 4
GLA_DK = 64
GLA_DV = 128
GLA_LOWRANK = 16
GLA_TAU = 16.0
CONV_CH = 512
CONV_WIDTH = 31
HIST = CONV_WIDTH - 1
ATT_HEADS = 16
ATT_DH = 64
BAND_CHUNKS_PAST = 8
BAND_PAST = BAND_CHUNKS_PAST * CHUNK
REL_CLIP = 256
N_GROUPS = 4
EXPERTS_PER_GROUP = 8
N_EXPERTS = N_GROUPS * EXPERTS_PER_GROUP
D_EXPERT = 512

LANES = 128
SUBLANES = 8
ROW_BLOCK = 512
CONV_TILE = 32
GMM_TILE = 256
ATT_QB = 256
ATT_WIN = ATT_QB + BAND_PAST
NEG_BIG = -1e30
VMEM_LIMIT = 56 * 1024 * 1024

QK_W = GLA_HEADS * GLA_DK
V_W = GLA_HEADS * GLA_DV
COL_Q = 0
COL_K = COL_Q + QK_W
COL_V = COL_K + QK_W
COL_GATE = COL_V + V_W
COL_CVAL = COL_GATE + V_W
COL_CGATE = COL_CVAL + CONV_CH
COL_ALR = COL_CGATE + CONV_CH
EVEN_COLS = COL_ALR + LANES


def _rmsnorm(x, g):
    return x * lax.rsqrt(jnp.mean(x * x, axis=-1, keepdims=True) + EPS) * g


def _sigmoid(x):
    return 1.0 / (1.0 + jnp.exp(-x))


def _dot(a, b):
    return jnp.dot(a, b, preferred_element_type=F32)


def _dot_nt(a, b):
    return lax.dot_general(a, b, (((1,), (1,)), ((), ())), preferred_element_type=F32)


def _dot_tn(a, b, precision=None):
    return lax.dot_general(a, b, (((0,), (0,)), ((), ())), preferred_element_type=F32, precision=precision)


def _even_mixer_kernel(*refs, chunk, cps, spb, carry, aliased):
    (x_ref, g_ref, win_ref, wau_ref, bal_ref, gng_ref, cw_ref, cb_ref, lng_ref, lnb_ref, wout_ref,
     s0_ref, c0_ref) = refs[:13]
    refs = refs[13 + (1 if aliased else 0):]
    h_ref, sfin_ref, cfin_ref, proj_ref, lg_ref, cum_ref, mix_ref, s_ref, ubuf_ref = refs
    j = pl.program_id(1)
    nj = pl.num_programs(1)
    seg = cps * chunk

    x = x_ref[...]
    xn = _rmsnorm(x, g_ref[...])
    proj_ref[...] = _dot(xn.astype(BF16), win_ref[...])
    alr = proj_ref[:, COL_ALR:COL_ALR + LANES]
    xa = _dot(alr.astype(BF16), wau_ref[...]) + bal_ref[...]
    lg_ref[...] = (jnp.minimum(xa, 0.0) - jnp.log1p(jnp.exp(-jnp.abs(xa)))) * (1.0 / GLA_TAU)

    row_i = lax.broadcasted_iota(jnp.int32, (chunk, QK_W), 0)
    tri_r = lax.broadcasted_iota(jnp.int32, (chunk, chunk), 0)
    tri_c = lax.broadcasted_iota(jnp.int32, (chunk, chunk), 1)
    causal = tri_r >= tri_c
    ones_cols = jnp.ones((chunk, LANES), F32)

    def chunk_body(ci, c):
        r0 = pl.multiple_of(ci * chunk, chunk)
        sq = ci // cps if spb > 1 else 0
        first = (ci % cps) == 0
        last = (ci % cps) == (cps - 1)
        if carry:
            first = jnp.logical_and(first, j == 0)
            last = jnp.logical_and(last, j == nj - 1)

        @pl.when(first)
        def _():
            s_ref[...] = s0_ref[sq]

        lg = lg_ref[pl.ds(r0, chunk), :]
        cum = lg
        shift = 1
        while shift < chunk:
            cum = cum + jnp.where(row_i >= shift, pltpu.roll(cum, shift, 0), 0.0)
            shift *= 2
        cum_ref[...] = cum
        tot = cum_ref[pl.ds(chunk - 1, 1), :]
        mid = cum_ref[pl.ds(chunk // 2 - 1, 1), :]
        tot_col = jnp.exp(_dot_tn(lg, ones_cols, precision=HIGHEST))
        e_in = jnp.exp(cum)
        e_q = jnp.exp(cum - mid)
        e_k = jnp.exp(mid - cum)
        e_s = jnp.exp(tot - cum)
        for h in range(GLA_HEADS):
            ks = slice(h * GLA_DK, (h + 1) * GLA_DK)
            vs = slice(h * GLA_DV, (h + 1) * GLA_DV)
            q = proj_ref[pl.ds(r0, chunk), COL_Q + h * GLA_DK:COL_Q + (h + 1) * GLA_DK] * (GLA_DK ** -0.5)
            k = proj_ref[pl.ds(r0, chunk), COL_K + h * GLA_DK:COL_K + (h + 1) * GLA_DK]
            v = proj_ref[pl.ds(r0, chunk), COL_V + h * GLA_DV:COL_V + (h + 1) * GLA_DV].astype(BF16)
            gate = proj_ref[pl.ds(r0, chunk), COL_GATE + h * GLA_DV:COL_GATE + (h + 1) * GLA_DV]
            s_old = s_ref[h]
            scores = _dot_nt((q * e_q[:, ks]).astype(BF16), (k * e_k[:, ks]).astype(BF16))
            scores = jnp.where(causal, scores, 0.0)
            o = _dot((q * e_in[:, ks]).astype(BF16), s_old.astype(BF16)) + _dot(scores.astype(BF16), v)
            s_ref[h] = tot_col[h * GLA_DK:(h + 1) * GLA_DK, :] * s_old + _dot_tn((k * e_s[:, ks]).astype(BF16), v)
            o = o * lax.rsqrt(jnp.mean(o * o, axis=-1, keepdims=True) + EPS)
            o = o * gng_ref[:, vs] * (gate * _sigmoid(gate))
            mix_ref[pl.ds(r0, chunk), vs] = o.astype(BF16)

        @pl.when(last)
        def _():
            sfin_ref[sq] = s_ref[...]

        return c

    lax.fori_loop(0, spb * cps, chunk_body, 0, unroll=2)

    tile = CONV_TILE

    def conv_tile(t0, out_r0):
        wv = ubuf_ref[pl.ds(t0, 2 * tile), :]
        acc = jnp.zeros((tile, CONV_CH), F32)
        for b in range(SUBLANES):
            sb = wv if b == 0 else pltpu.roll(wv, 2 * tile - b, 0)
            for a in range(tile // SUBLANES + 1):
                off = SUBLANES * a + b
                if 2 <= off <= HIST + 2:
                    acc = acc + sb[SUBLANES * a:SUBLANES * a + tile, :] * cw_ref[pl.ds(off - 2, 1), :]
        proj_ref[pl.ds(out_r0, tile), COL_CVAL:COL_CVAL + CONV_CH] = acc + cb_ref[...]

    def conv_norm(r0):
        cv = proj_ref[pl.ds(r0, seg), COL_CVAL:COL_CVAL + CONV_CH]
        mu = jnp.mean(cv, axis=-1, keepdims=True)
        var = jnp.mean(jnp.square(cv - mu), axis=-1, keepdims=True)
        y = (cv - mu) * lax.rsqrt(var + EPS) * lng_ref[...] + lnb_ref[...]
        mix_ref[pl.ds(r0, seg), V_W:V_W + CONV_CH] = (y * _sigmoid(y)).astype(BF16)

    def conv_seg(sq, c):
        r0 = pl.multiple_of(sq * seg, seg) if spb > 1 else 0
        cval = proj_ref[pl.ds(r0, seg), COL_CVAL:COL_CVAL + CONV_CH]
        cgate = proj_ref[pl.ds(r0, seg), COL_CGATE:COL_CGATE + CONV_CH]

        def load_history():
            ubuf_ref[pl.ds(0, 8), :] = jnp.zeros((8, CONV_CH), F32)
            ubuf_ref[pl.ds(2, HIST), :] = c0_ref[sq]

        if carry:
            pl.when(j == 0)(load_history)
        else:
            load_history()
        ubuf_ref[pl.ds(CONV_TILE, seg), :] = cval * _sigmoid(cgate)
        if seg == tile:
            conv_tile(0, r0)
        else:
            def tile_body(t, cc):
                t0 = pl.multiple_of(t * tile, tile)
                conv_tile(t0, r0 + t0)
                return cc
            lax.fori_loop(0, seg // tile, tile_body, 0)
        conv_norm(r0)
        hist = ubuf_ref[pl.ds(seg + 2, HIST), :]
        if carry:
            @pl.when(j == nj - 1)
            def _():
                cfin_ref[sq] = hist
            ubuf_ref[pl.ds(2, HIST), :] = hist
        else:
            cfin_ref[sq] = hist
        return c

    if spb > 1:
        lax.fori_loop(0, spb, conv_seg, 0)
    else:
        conv_seg(0, 0)

    h_ref[...] = _dot(mix_ref[...], wout_ref[...]) + x


def _even_mixer(x2d, hbuf, weights, s0, c0, *, n_rows_total, block0, n_seq, blocks_per_seq, chunk, cps, spb):
    carry = spb == 1
    aliased = hbuf is not None
    const = lambda s, j: (0, 0)
    seq_blk = (lambda s, j: (s, 0, 0, 0)) if carry else (lambda s, j: (0, 0, 0, 0))
    seq_blk3 = (lambda s, j: (s, 0, 0)) if carry else (lambda s, j: (0, 0, 0))
    n_state = 1 if carry else spb
    wspec = lambda shape: pl.BlockSpec(shape, const, pipeline_mode=pl.Buffered(1))
    in_specs = [
        pl.BlockSpec((ROW_BLOCK, D_MODEL), lambda s, j: (s * blocks_per_seq + j, 0)),
        wspec((1, D_MODEL)),
        wspec((D_MODEL, EVEN_COLS)),
        wspec((LANES, QK_W)),
        wspec((1, QK_W)),
        wspec((1, V_W)),
        wspec((CONV_WIDTH, CONV_CH)),
        wspec((1, CONV_CH)),
        wspec((1, CONV_CH)),
        wspec((1, CONV_CH)),
        wspec((V_W + CONV_CH, D_MODEL)),
        pl.BlockSpec((n_state, GLA_HEADS, GLA_DK, GLA_DV), seq_blk),
        pl.BlockSpec((n_state, HIST, CONV_CH), seq_blk3),
    ]
    args = [x2d, *weights, s0, c0]
    aliases = {}
    if aliased:
        in_specs.append(pl.BlockSpec(memory_space=pl.ANY))
        args.append(hbuf)
        aliases = {len(args) - 1: 0}
    n_all = s0.shape[0]
    out_shape = (
        jax.ShapeDtypeStruct((n_rows_total, D_MODEL), F32),
        jax.ShapeDtypeStruct((n_all, GLA_HEADS, GLA_DK, GLA_DV), F32),
        jax.ShapeDtypeStruct((n_all, HIST, CONV_CH), F32),
    )
    out_specs = (
        pl.BlockSpec((ROW_BLOCK, D_MODEL), lambda s, j: (block0 + s * blocks_per_seq + j, 0)),
        pl.BlockSpec((n_state, GLA_HEADS, GLA_DK, GLA_DV), seq_blk),
        pl.BlockSpec((n_state, HIST, CONV_CH), seq_blk3),
    )
    seg = cps * chunk
    scratch = [
        pltpu.VMEM((ROW_BLOCK, EVEN_COLS), F32),
        pltpu.VMEM((ROW_BLOCK, QK_W), F32),
        pltpu.VMEM((chunk, QK_W), F32),
        pltpu.VMEM((ROW_BLOCK, V_W + CONV_CH), BF16),
        pltpu.VMEM((GLA_HEADS, GLA_DK, GLA_DV), F32),
        pltpu.VMEM((CONV_TILE + seg, CONV_CH), F32),
    ]
    kern = functools.partial(_even_mixer_kernel, chunk=chunk, cps=cps, spb=spb, carry=carry, aliased=aliased)
    return pl.pallas_call(
        kern, out_shape=out_shape, grid=(n_seq, blocks_per_seq), in_specs=in_specs, out_specs=out_specs,
        scratch_shapes=scratch, input_output_aliases=aliases,
        compiler_params=pltpu.CompilerParams(dimension_semantics=("arbitrary", "arbitrary"), vmem_limit_bytes=VMEM_LIMIT),
        name="even_mixer_carry" if carry else "even_mixer_step",
    )(*args)


def _to_row_tiles(ref, x):
    rows = x.shape[0]
    for c in range(x.shape[1] // LANES):
        ref[pl.ds(c, rows, stride=SUBLANES), :] = x[:, c * LANES:(c + 1) * LANES]


def _from_row_tiles(ref, rows, width=D_MODEL):
    return jnp.concatenate([ref[pl.ds(c, rows, stride=SUBLANES), :] for c in range(width // LANES)], axis=1)


def _route_kernel(h_ref, g_ref, wr_ref, br_ref, xn_ref, mi_ref, mw_ref, cnt_ref, base_ref):
    i = pl.program_id(0)

    @pl.when(i == 0)
    def _():
        base_ref[...] = jnp.zeros_like(base_ref)

    xn = _rmsnorm(h_ref[...], g_ref[...])
    _to_row_tiles(xn_ref, xn)
    logits = jnp.dot(xn, wr_ref[...], precision=HIGHEST, preferred_element_type=F32) + br_ref[...]
    lane = lax.broadcasted_iota(jnp.int32, logits.shape, 1).astype(F32)
    far = float(1 << 20)
    gl = jnp.where(lane < N_GROUPS, logits, -jnp.inf)
    gmax = jnp.max(gl, axis=-1, keepdims=True)
    gidx = jnp.min(jnp.where(gl == gmax, lane, far), axis=-1, keepdims=True)
    gw = 1.0 / jnp.sum(jnp.exp(gl - gmax), axis=-1, keepdims=True)
    lo = N_GROUPS + gidx * EXPERTS_PER_GROUP
    el = jnp.where(jnp.logical_and(lane >= lo, lane < lo + EXPERTS_PER_GROUP), logits, -jnp.inf)
    m1 = jnp.max(el, axis=-1, keepdims=True)
    i1 = jnp.min(jnp.where(el == m1, lane, far), axis=-1, keepdims=True)
    el2 = jnp.where(lane == i1, -jnp.inf, el)
    m2 = jnp.max(el2, axis=-1, keepdims=True)
    i2 = jnp.min(jnp.where(el2 == m2, lane, far), axis=-1, keepdims=True)
    e2 = jnp.exp(m2 - m1)
    den = 1.0 + e2
    w0 = (1.0 / den) * gw
    w1 = (e2 / den) * gw
    id0 = i1 - N_GROUPS
    id1 = i2 - N_GROUPS
    oh0 = jnp.where(lane == id0, 1.0, 0.0)
    oh1 = jnp.where(lane == id1, 1.0, 0.0)
    rr = lax.broadcasted_iota(jnp.int32, (ROW_BLOCK, ROW_BLOCK), 0)
    cc = lax.broadcasted_iota(jnp.int32, (ROW_BLOCK, ROW_BLOCK), 1)
    below = jnp.where(rr > cc, 1.0, 0.0).astype(BF16)
    p0 = _dot(below, oh0.astype(BF16))
    p1 = _dot(below, oh1.astype(BF16))
    cnt0 = jnp.sum(oh0, axis=0, keepdims=True)
    cnt1 = jnp.sum(oh1, axis=0, keepdims=True)
    base = base_ref[...]
    rank0 = jnp.sum(oh0 * (p0 + base), axis=-1, keepdims=True)
    rank1 = jnp.sum(oh1 * (p1 + base + cnt0), axis=-1, keepdims=True)
    new_base = base + cnt0 + cnt1
    base_ref[...] = new_base
    cnt_ref[...] = new_base.astype(jnp.int32)
    meta = jnp.where(lane == 0, id0, jnp.where(lane == 1, id1, jnp.where(lane == 2, rank0, jnp.where(lane == 3, rank1, 0.0))))
    mi_ref[0] = meta.T[0:8, :].astype(jnp.int32)
    mw_ref[...] = jnp.where(lane == 0, w0, jnp.where(lane == 1, w1, 0.0))


def _route(h, g, wr, br, n_blocks):
    n_rows = n_blocks * ROW_BLOCK
    const = lambda i: (0, 0)
    return pl.pallas_call(
        _route_kernel,
        out_shape=(
            jax.ShapeDtypeStruct((n_rows * SUBLANES, LANES), F32),
            jax.ShapeDtypeStruct((n_blocks, 8, ROW_BLOCK), jnp.int32),
            jax.ShapeDtypeStruct((n_rows, LANES), F32),
            jax.ShapeDtypeStruct((1, LANES), jnp.int32),
        ),
        grid=(n_blocks,),
        in_specs=[
            pl.BlockSpec((ROW_BLOCK, D_MODEL), lambda i: (i, 0)),
            pl.BlockSpec((1, D_MODEL), const),
            pl.BlockSpec((D_MODEL, LANES), const),
            pl.BlockSpec((1, LANES), const),
        ],
        out_specs=(
            pl.BlockSpec((ROW_BLOCK * SUBLANES, LANES), lambda i: (i, 0)),
            pl.BlockSpec((1, 8, ROW_BLOCK), lambda i: (i, 0, 0)),
            pl.BlockSpec((ROW_BLOCK, LANES), lambda i: (i, 0)),
            pl.BlockSpec((1, LANES), const),
        ),
        scratch_shapes=[pltpu.VMEM((1, LANES), F32)],
        compiler_params=pltpu.CompilerParams(dimension_semantics=("arbitrary",), vmem_limit_bytes=VMEM_LIMIT),
        name="moe_route",
    )(h, g, wr, br)


def _slot_index_kernel(dest_ref, tok0_ref, dst0_ref, tok_ref, dst_ref, sem, *, n_blocks, n_rows):
    init_tok = pltpu.make_async_copy(tok0_ref, tok_ref, sem.at[0])
    init_dst = pltpu.make_async_copy(dst0_ref, dst_ref, sem.at[1])
    init_tok.start()
    init_dst.start()
    init_tok.wait()
    init_dst.wait()

    def block(b, c):
        for k in range(2):
            def body(j, cc):
                d = dest_ref[b * (2 * ROW_BLOCK) + k * ROW_BLOCK + j]
                t = b * ROW_BLOCK + j
                tok_ref[d] = t * SUBLANES
                dst_ref[d] = (k * n_rows + t) * SUBLANES
                return cc

            lax.fori_loop(0, ROW_BLOCK, body, 0, unroll=8)
        return c

    lax.fori_loop(0, n_blocks, block, 0)


def _slot_index(dest_flat, tok0, dst0, n_blocks, n_rows):
    n_slots = tok0.shape[0]
    return pl.pallas_call(
        functools.partial(_slot_index_kernel, n_blocks=n_blocks, n_rows=n_rows),
        out_shape=(jax.ShapeDtypeStruct((n_slots,), jnp.int32), jax.ShapeDtypeStruct((n_slots,), jnp.int32)),
        in_specs=[pl.BlockSpec(memory_space=pltpu.SMEM), pl.BlockSpec(memory_space=pl.ANY),
                  pl.BlockSpec(memory_space=pl.ANY)],
        out_specs=(pl.BlockSpec(memory_space=pltpu.SMEM), pl.BlockSpec(memory_space=pltpu.SMEM)),
        scratch_shapes=[pltpu.SemaphoreType.DMA((2,))],
        name="moe_slot_index",
    )(dest_flat, tok0, dst0)


GMM_ISSUE_GROUPS = 8
GMM_BUFFERS = 3
GATHER_PRIORITY = 0
SCATTER_PRIORITY = 1


def _gmm_kernel(te_ref, nv_ref, tok_ref, dst_ref, xn_ref, wg_ref, wu_ref, wd_ref, yp_ref,
                xbuf, ybuf, wgb, wub, wdb, gsem, ssem):
    i = pl.program_id(0)
    nv = nv_ref[0]
    n_tiles = pl.num_programs(0) - 1
    slot = i % GMM_BUFFERS
    nxt = (i + 1) % GMM_BUFFERS
    prv = (i + 2) % GMM_BUFFERS

    def gather_copy(tile, r, buf):
        src = pl.multiple_of(tok_ref[tile * GMM_TILE + r], SUBLANES)
        return pltpu.make_async_copy(xn_ref.at[pl.ds(src, SUBLANES), :],
                                     xbuf.at[buf, pl.ds(r * SUBLANES, SUBLANES), :], gsem.at[buf])

    def scatter_copy(tile, r, buf):
        d = pl.multiple_of(dst_ref[tile * GMM_TILE + r], SUBLANES)
        return pltpu.make_async_copy(ybuf.at[buf, pl.ds(r * SUBLANES, SUBLANES), :],
                                     yp_ref.at[pl.ds(d, SUBLANES), :], ssem.at[buf])

    def wait_tile(copy_of_row, buf):
        def body(r, c):
            copy_of_row(0, 0, buf).wait()
            return c

        lax.fori_loop(0, GMM_TILE, body, 0, unroll=16)

    @pl.when(i == 0)
    def _():
        ybuf[...] = jnp.zeros_like(ybuf)

        def body(r, c):
            gather_copy(0, r, 0).start(priority=GATHER_PRIORITY)
            gather_copy(1, r, 1).start(priority=GATHER_PRIORITY)
            return c

        lax.fori_loop(0, GMM_TILE, body, 0, unroll=8)

    @pl.when(jnp.logical_and(i >= 2, i <= nv))
    def _():
        wait_tile(scatter_copy, slot)

    @pl.when(i < nv)
    def _():
        wait_tile(gather_copy, slot)
        prev = te_ref[jnp.maximum(i - 1, 0)]

        @pl.when(jnp.logical_or(i == 0, te_ref[i] != prev))
        def _():
            wgb[...] = wg_ref[0, 0].astype(BF16)
            wub[...] = wu_ref[0, 0].astype(BF16)
            wdb[...] = wd_ref[0, 0].astype(BF16)

        prev_tile = jnp.where(i >= 1, i - 1, n_tiles)
        rows_per_group = GMM_TILE // GMM_ISSUE_GROUPS

        def issue(group):
            for r in range(group * rows_per_group, (group + 1) * rows_per_group):
                gather_copy(i + 2, r, prv).start(priority=GATHER_PRIORITY)
                scatter_copy(prev_tile, r, prv).start(priority=SCATTER_PRIORITY)

        x = _from_row_tiles(xbuf.at[slot], GMM_TILE).astype(BF16)
        half = D_EXPERT // 2
        quarter = D_MODEL // 4
        a0 = _dot(x, wgb[:, 0:half])
        issue(0)
        a1 = _dot(x, wgb[:, half:D_EXPERT])
        issue(1)
        b0 = _dot(x, wub[:, 0:half])
        issue(2)
        b1 = _dot(x, wub[:, half:D_EXPERT])
        issue(3)
        he = jnp.concatenate([(a0 * _sigmoid(a0)) * b0, (a1 * _sigmoid(a1)) * b1], axis=1).astype(BF16)
        for n in range(4):
            yn = _dot(he, wdb[:, n * quarter:(n + 1) * quarter])
            for c in range(quarter // LANES):
                ybuf[slot, pl.ds(n * (quarter // LANES) + c, GMM_TILE, stride=SUBLANES), :] = yn[:, c * LANES:(c + 1) * LANES]
            issue(4 + n)

    @pl.when(i == nv)
    def _():
        wait_tile(gather_copy, slot)
        wait_tile(gather_copy, nxt)
        wait_tile(scatter_copy, nxt)

        def body(r, c):
            scatter_copy(nv - 1, r, prv).start(priority=SCATTER_PRIORITY)
            return c

        lax.fori_loop(0, GMM_TILE, body, 0, unroll=8)
        wait_tile(scatter_copy, prv)


def _gmm(tile_expert, n_valid, tok, dst, xn_tiles, w_gate, w_up, w_down, layer):
    n_rows = xn_tiles.shape[0] // SUBLANES
    n_tiles = tile_expert.shape[0] - 1

    def w_idx(i, te, nv, tok, dst):
        return (layer, te[jnp.maximum(jnp.minimum(i, nv[0] - 1), 0)], 0, 0)

    return pl.pallas_call(
        _gmm_kernel,
        out_shape=jax.ShapeDtypeStruct(((2 * n_rows + GMM_BUFFERS * GMM_TILE) * SUBLANES, LANES), F32),
        grid_spec=pltpu.PrefetchScalarGridSpec(
            num_scalar_prefetch=4,
            grid=(n_tiles + 1,),
            in_specs=[
                pl.BlockSpec(memory_space=pl.ANY),
                pl.BlockSpec((1, 1, D_MODEL, D_EXPERT), w_idx),
                pl.BlockSpec((1, 1, D_MODEL, D_EXPERT), w_idx),
                pl.BlockSpec((1, 1, D_EXPERT, D_MODEL), w_idx),
            ],
            out_specs=pl.BlockSpec(memory_space=pl.ANY),
            scratch_shapes=[
                pltpu.VMEM((GMM_BUFFERS, GMM_TILE * SUBLANES, LANES), F32),
                pltpu.VMEM((GMM_BUFFERS, GMM_TILE * SUBLANES, LANES), F32),
                pltpu.VMEM((D_MODEL, D_EXPERT), BF16),
                pltpu.VMEM((D_MODEL, D_EXPERT), BF16),
                pltpu.VMEM((D_EXPERT, D_MODEL), BF16),
                pltpu.SemaphoreType.DMA((GMM_BUFFERS,)),
                pltpu.SemaphoreType.DMA((GMM_BUFFERS,)),
            ],
        ),
        compiler_params=pltpu.CompilerParams(dimension_semantics=("arbitrary",), vmem_limit_bytes=VMEM_LIMIT),
        name="moe_gmm",
    )(tile_expert, n_valid, tok, dst, xn_tiles, w_gate, w_up, w_down)


def _combine_kernel(*refs, final):
    if final:
        h_ref, mw_ref, y0_ref, y1_ref, gfin_ref, out_ref = refs
    else:
        h_ref, mw_ref, y0_ref, y1_ref, out_ref = refs
    mw = mw_ref[...]
    y0 = _from_row_tiles(y0_ref, ROW_BLOCK)
    y1 = _from_row_tiles(y1_ref, ROW_BLOCK)
    hn = h_ref[...] + (mw[:, 0:1] * y0 + mw[:, 1:2] * y1)
    out_ref[...] = _rmsnorm(hn, gfin_ref[...]) if final else hn


def _combine(h, mw, yp, gfin, *, block0, n_blocks):
    final = gfin is not None
    total_blocks = h.shape[0] // ROW_BLOCK
    in_specs = [
        pl.BlockSpec((ROW_BLOCK, D_MODEL), lambda i: (block0 + i, 0)),
        pl.BlockSpec((ROW_BLOCK, LANES), lambda i: (block0 + i, 0)),
        pl.BlockSpec((ROW_BLOCK * SUBLANES, LANES), lambda i: (block0 + i, 0)),
        pl.BlockSpec((ROW_BLOCK * SUBLANES, LANES), lambda i: (total_blocks + block0 + i, 0)),
    ]
    args = [h, mw, yp, yp]
    if final:
        in_specs.append(pl.BlockSpec((1, D_MODEL), lambda i: (0, 0)))
        args.append(gfin)
    return pl.pallas_call(
        functools.partial(_combine_kernel, final=final),
        out_shape=jax.ShapeDtypeStruct((n_blocks * ROW_BLOCK, D_MODEL), F32),
        grid=(n_blocks,),
        in_specs=in_specs,
        out_specs=pl.BlockSpec((ROW_BLOCK, D_MODEL), lambda i: (i, 0)),
        compiler_params=pltpu.CompilerParams(dimension_semantics=("arbitrary",), vmem_limit_bytes=VMEM_LIMIT),
        name="moe_combine_final" if final else "moe_combine",
    )(*args)


def _moe_tiles(n_rows):
    return (2 * n_rows + N_EXPERTS * (GMM_TILE - 1) + GMM_TILE - 1) // GMM_TILE


def _hier_moe(h, g, w_rg, b_rg, w_re, b_re, w_gate, w_up, w_down, layer, gfin, splits):
    n_rows = h.shape[0]
    n_blocks = n_rows // ROW_BLOCK
    n_re = N_GROUPS * EXPERTS_PER_GROUP
    wr = jnp.zeros((D_MODEL, LANES), F32)
    wr = wr.at[:, :N_GROUPS].set(w_rg).at[:, N_GROUPS:N_GROUPS + n_re].set(w_re.reshape(D_MODEL, n_re))
    br = jnp.zeros((1, LANES), F32)
    br = br.at[0, :N_GROUPS].set(b_rg).at[0, N_GROUPS:N_GROUPS + n_re].set(b_re.reshape(n_re))
    xn_tiles, meta_i, meta_w, counts = _route(h, g.reshape(1, D_MODEL), wr, br, n_blocks)

    n_tiles = _moe_tiles(n_rows)
    cnt = counts[0, :N_EXPERTS]
    padded = ((cnt + GMM_TILE - 1) // GMM_TILE) * GMM_TILE
    ends = jnp.cumsum(padded)
    offs = ends - padded
    tile_start = jnp.arange(n_tiles + 1, dtype=jnp.int32) * GMM_TILE
    tile_expert = jnp.minimum(jnp.sum((tile_start[:, None] >= ends[None, :]).astype(jnp.int32), axis=1), N_EXPERTS - 1)
    n_valid = (ends[-1:] // GMM_TILE).astype(jnp.int32)
    eid = meta_i[:, 0:2, :]
    seg_start = jnp.sum(jnp.where(eid[..., None] == jnp.arange(N_EXPERTS, dtype=jnp.int32), offs, 0), axis=-1)
    dest = (seg_start + meta_i[:, 2:4, :]).astype(jnp.int32).reshape(-1)
    slots = jnp.arange((n_tiles + GMM_BUFFERS - 1) * GMM_TILE, dtype=jnp.int32)
    scratch_rows = 2 * n_rows + slots % (GMM_BUFFERS * GMM_TILE)
    tok, dst = _slot_index(dest, jnp.zeros_like(slots), scratch_rows * SUBLANES, n_blocks, n_rows)

    yp = _gmm(tile_expert.astype(jnp.int32), n_valid, tok, dst, xn_tiles, w_gate, w_up, w_down, layer)
    gf = None if gfin is None else gfin.reshape(1, D_MODEL)
    return [_combine(h, meta_w, yp, gf, block0=b0, n_blocks=nb) for (b0, nb) in splits]


def _qkv_kernel(h_ref, g_ref, w_ref, q_ref, k_ref, v_ref):
    xn = _rmsnorm(h_ref[...], g_ref[...]).astype(BF16)
    width = ATT_HEADS * ATT_DH
    q_ref[...] = (_dot(xn, w_ref[:, 0:width]) * (ATT_DH ** -0.5)).astype(BF16)
    k_ref[...] = _dot(xn, w_ref[:, width:2 * width])
    v_ref[...] = _dot(xn, w_ref[:, 2 * width:3 * width])


def _qkv(h, g, w_bf16, n_blocks):
    n_rows = n_blocks * ROW_BLOCK
    width = ATT_HEADS * ATT_DH
    row_blk = lambda i: (i, 0)
    return pl.pallas_call(
        _qkv_kernel,
        out_shape=(
            jax.ShapeDtypeStruct((n_rows, width), BF16),
            jax.ShapeDtypeStruct((n_rows, width), F32),
            jax.ShapeDtypeStruct((n_rows, width), F32),
        ),
        grid=(n_blocks,),
        in_specs=[
            pl.BlockSpec((ROW_BLOCK, D_MODEL), row_blk),
            pl.BlockSpec((1, D_MODEL), lambda i: (0, 0)),
            pl.BlockSpec((D_MODEL, 3 * width), lambda i: (0, 0), pipeline_mode=pl.Buffered(1)),
        ],
        out_specs=(
            pl.BlockSpec((ROW_BLOCK, width), row_blk),
            pl.BlockSpec((ROW_BLOCK, width), row_blk),
            pl.BlockSpec((ROW_BLOCK, width), row_blk),
        ),
        compiler_params=pltpu.CompilerParams(dimension_semantics=("arbitrary",), vmem_limit_bytes=VMEM_LIMIT),
        name="qkv_proj",
    )(h, g, w_bf16)


def _softmax_parts(parts):
    m = functools.reduce(jnp.maximum, [jnp.max(s, axis=-1, keepdims=True) for s in parts])
    ps = [jnp.exp(s - m) for s in parts]
    inv = 1.0 / functools.reduce(jnp.add, [jnp.sum(p, axis=-1, keepdims=True) for p in ps])
    return ps, inv


def _band_prompt_kernel(h_ref, q_ref, ka_ref, kb_ref, kc_ref, va_ref, vb_ref, vc_ref, bm_ref, wout_ref,
                        out_ref, kw_ref, vw_ref, o_ref):
    i = pl.program_id(1)
    for b, (k_ref, v_ref) in enumerate(((ka_ref, va_ref), (kb_ref, vb_ref), (kc_ref, vc_ref))):
        kw_ref[pl.ds(b * ATT_QB, ATT_QB), :] = k_ref[...].astype(BF16)
        vw_ref[pl.ds(b * ATT_QB, ATT_QB), :] = v_ref[...].astype(BF16)
    col = lax.broadcasted_iota(jnp.int32, (1, ATT_WIN), 1)
    first_valid = (2 - jnp.minimum(i, 2)) * ATT_QB
    col_mask = jnp.where(col >= first_valid, 0.0, NEG_BIG)
    for h in range(ATT_HEADS):
        hs = slice(h * ATT_DH, (h + 1) * ATT_DH)
        s = _dot_nt(q_ref[:, hs], kw_ref[:, hs]) + bm_ref[h] + col_mask
        (p,), inv = _softmax_parts([s])
        o_ref[:, hs] = (_dot(p.astype(BF16), vw_ref[:, hs]) * inv).astype(BF16)
    out_ref[...] = _dot(o_ref[...], wout_ref[...]) + h_ref[...]


def _band_prompt(h, q, k, v, bm, wout, *, n_rows_total, n_seq, blocks_per_seq):
    width = ATT_HEADS * ATT_DH
    blk = lambda s, i: (s * blocks_per_seq + i, 0)
    back = lambda n: (lambda s, i: (s * blocks_per_seq + jnp.maximum(i - n, 0), 0))
    kv_spec = lambda n: pl.BlockSpec((ATT_QB, width), back(n))
    return pl.pallas_call(
        _band_prompt_kernel,
        out_shape=jax.ShapeDtypeStruct((n_rows_total, D_MODEL), F32),
        grid=(n_seq, blocks_per_seq),
        in_specs=[
            pl.BlockSpec((ATT_QB, D_MODEL), blk),
            pl.BlockSpec((ATT_QB, width), blk),
            kv_spec(2), kv_spec(1), kv_spec(0),
            kv_spec(2), kv_spec(1), kv_spec(0),
            pl.BlockSpec((ATT_HEADS, ATT_QB, ATT_WIN), lambda s, i: (0, 0, 0), pipeline_mode=pl.Buffered(1)),
            pl.BlockSpec((width, D_MODEL), lambda s, i: (0, 0), pipeline_mode=pl.Buffered(1)),
        ],
        out_specs=pl.BlockSpec((ATT_QB, D_MODEL), blk),
        scratch_shapes=[
            pltpu.VMEM((ATT_WIN, width), BF16),
            pltpu.VMEM((ATT_WIN, width), BF16),
            pltpu.VMEM((ATT_QB, width), BF16),
        ],
        compiler_params=pltpu.CompilerParams(dimension_semantics=("arbitrary", "arbitrary"), vmem_limit_bytes=VMEM_LIMIT),
        name="band_attn_prompt",
    )(h, q, k, k, k, v, v, v, bm, wout)


def _band_sample_kernel(h_ref, q_ref, kn_ref, vn_ref, ck_ref, cv_ref, b1_ref, b2_ref, wout_ref, hbuf_ref,
                        out_ref, o_ref):
    del hbuf_ref
    for h in range(ATT_HEADS):
        hs = slice(h * ATT_DH, (h + 1) * ATT_DH)
        qh = q_ref[:, hs]
        s1 = _dot(qh, ck_ref[0, h].astype(BF16)) + b1_ref[h]
        s2 = _dot_nt(qh, kn_ref[:, hs].astype(BF16)) + b2_ref[h]
        (p1, p2), inv = _softmax_parts([s1, s2])
        o = _dot_nt(p1.astype(BF16), cv_ref[0, h].astype(BF16)) + _dot(p2.astype(BF16), vn_ref[:, hs].astype(BF16))
        o_ref[:, hs] = (o * inv).astype(BF16)
    out_ref[...] = _dot(o_ref[...], wout_ref[...]) + h_ref[...]


def _band_sample(h, q, k, v, cache_k, cache_v, b1, b2, wout, hbuf, *, row0, n_seq, seq_len):
    width = ATT_HEADS * ATT_DH
    past = cache_k.shape[3]
    blk0 = row0 // seq_len
    blk = lambda s: (blk0 + s, 0)
    const3 = lambda s: (0, 0, 0)
    return pl.pallas_call(
        _band_sample_kernel,
        out_shape=jax.ShapeDtypeStruct(hbuf.shape, F32),
        grid=(n_seq,),
        in_specs=[
            pl.BlockSpec((seq_len, D_MODEL), blk),
            pl.BlockSpec((seq_len, width), blk),
            pl.BlockSpec((seq_len, width), blk),
            pl.BlockSpec((seq_len, width), blk),
            pl.BlockSpec((1, ATT_HEADS, ATT_DH, past), lambda s: (s, 0, 0, 0)),
            pl.BlockSpec((1, ATT_HEADS, ATT_DH, past), lambda s: (s, 0, 0, 0)),
            pl.BlockSpec((ATT_HEADS, seq_len, past), const3),
            pl.BlockSpec((ATT_HEADS, seq_len, seq_len), const3),
            pl.BlockSpec((width, D_MODEL), lambda s: (0, 0)),
            pl.BlockSpec(memory_space=pl.ANY),
        ],
        out_specs=pl.BlockSpec((seq_len, D_MODEL), blk),
        scratch_shapes=[pltpu.VMEM((seq_len, width), BF16)],
        input_output_aliases={9: 0},
        compiler_params=pltpu.CompilerParams(dimension_semantics=("arbitrary",), vmem_limit_bytes=VMEM_LIMIT),
        name="band_attn_sample",
    )(h, q, k, v, cache_k, cache_v, b1, b2, wout, hbuf)


def _band_bias_table(rb):
    n_top = BAND_PAST - REL_CLIP + 1
    n_var = REL_CLIP + CHUNK - 2
    n_bot = ATT_WIN - n_top - n_var
    n = ATT_WIN + ATT_QB
    heads = rb.shape[0]
    top = jnp.broadcast_to(rb[:, n_var + 1:n_var + 2], (heads, n_top))
    by_d = jnp.concatenate([top, rb[:, 1:n_var + 1][:, ::-1], jnp.broadcast_to(rb[:, 0:1], (heads, n_bot)),
                            jnp.broadcast_to(rb[:, n_var + 1:n_var + 2], (heads, ATT_QB))], axis=1)
    shifted = jnp.tile(by_d, (1, ATT_QB))[:, :ATT_QB * (n - 1)].reshape(heads, ATT_QB, n - 1)
    return shifted[:, :, :ATT_WIN]


def kernel(x_prompt, x_sample, state_gla, state_conv, cache_band_k, cache_band_v, norm_mix_g, norm_ffn_g, norm_final_g, w_in_even, w_alpha_up, b_alpha, gla_norm_g, conv_w, conv_b, conv_ln_g, conv_ln_b, w_out_even, w_qkv_odd, rel_bias, w_out_odd, w_router_grp, b_router_grp, w_router_exp, b_router_exp, w_exp_gate, w_exp_up, w_exp_down):
    batch, seq, _ = x_prompt.shape
    dec_batch, dec_seq, _ = x_sample.shape
    n_prompt = batch * seq
    n_sample = dec_batch * dec_seq
    assert seq % ROW_BLOCK == 0 and n_sample == ROW_BLOCK and seq % ATT_QB == 0
    n_rows = n_prompt + n_sample
    prompt_blocks = n_prompt // ROW_BLOCK
    n_blocks = n_rows // ROW_BLOCK
    width = ATT_HEADS * ATT_DH
    past = cache_band_k.shape[2]

    w_in = w_in_even[0]
    c0, c1, c2, c3, c4, c5 = (QK_W, 2 * QK_W, 2 * QK_W + V_W, 2 * QK_W + 2 * V_W,
                              2 * QK_W + 2 * V_W + GLA_LOWRANK, 2 * QK_W + 2 * V_W + GLA_LOWRANK + CONV_CH)
    w_in_p = jnp.concatenate(
        [w_in[:, :c3], w_in[:, c4:], w_in[:, c3:c4], jnp.zeros((D_MODEL, LANES - GLA_LOWRANK), F32)], axis=1).astype(BF16)
    wau_p = jnp.concatenate([w_alpha_up[0], jnp.zeros((LANES - GLA_LOWRANK, QK_W), F32)], axis=0).astype(BF16)
    even_w = (norm_mix_g[0].reshape(1, D_MODEL), w_in_p, wau_p, b_alpha[0].reshape(1, QK_W),
              gla_norm_g[0].reshape(1, V_W), conv_w[0], conv_b[0].reshape(1, CONV_CH),
              conv_ln_g[0].reshape(1, CONV_CH), conv_ln_b[0].reshape(1, CONV_CH), w_out_even[0].astype(BF16))
    h1, gla_p, conv_p = _even_mixer(
        x_prompt.reshape(n_prompt, D_MODEL), None, even_w,
        jnp.zeros((batch, GLA_HEADS, GLA_DK, GLA_DV), F32), jnp.zeros((batch, HIST, CONV_CH), F32),
        n_rows_total=n_rows, block0=0, n_seq=batch, blocks_per_seq=seq // ROW_BLOCK,
        chunk=CHUNK, cps=ROW_BLOCK // CHUNK, spb=1)
    h1, gla_s, conv_s = _even_mixer(
        x_sample.reshape(n_sample, D_MODEL), h1, even_w, state_gla[0], state_conv[0],
        n_rows_total=n_rows, block0=prompt_blocks, n_seq=1, blocks_per_seq=1,
        chunk=dec_seq, cps=1, spb=dec_batch)

    def moe(h, layer, gfin, splits):
        return _hier_moe(h, norm_ffn_g[layer], w_router_grp[layer], b_router_grp[layer], w_router_exp[layer],
                         b_router_exp[layer], w_exp_gate, w_exp_up, w_exp_down, layer, gfin, splits)

    (h2,) = moe(h1, 0, None, [(0, n_blocks)])

    q, k, v = _qkv(h2, norm_mix_g[1].reshape(1, D_MODEL), w_qkv_odd[0].astype(BF16), n_blocks)
    toe = _band_bias_table(rel_bias[0])
    r = jnp.arange(ATT_QB)[:, None]
    jb = jnp.arange(ATT_WIN)[None, :] - CHUNK * (r // CHUNK)
    in_band = jnp.logical_and(jb >= 0, jb < BAND_PAST + CHUNK)
    bm = jnp.where(in_band[None], toe, NEG_BIG)
    wout_odd = w_out_odd[0].astype(BF16)
    h3 = _band_prompt(h2, q, k, v, bm, wout_odd, n_rows_total=n_rows, n_seq=batch, blocks_per_seq=seq // ATT_QB)
    assert past == BAND_PAST and dec_seq <= CHUNK
    b1 = toe[:, :dec_seq, :past]
    b2 = toe[:, :dec_seq, past:past + dec_seq]
    h3 = _band_sample(h2, q, k, v, jnp.transpose(cache_band_k[0], (0, 2, 3, 1)),
                      jnp.transpose(cache_band_v[0], (0, 2, 3, 1)), b1, b2, wout_odd, h3,
                      row0=n_prompt, n_seq=dec_batch, seq_len=dec_seq)

    y_prompt, y_sample = moe(h3, 1, norm_final_g, [(0, prompt_blocks), (prompt_blocks, n_blocks - prompt_blocks)])

    rows = min(BAND_PAST, seq)
    tail = lambda a: jnp.stack([a[b * seq + seq - rows:(b + 1) * seq] for b in range(batch)]).reshape(
        batch, rows, ATT_HEADS, ATT_DH)
    kp, vp = tail(k), tail(v)
    ks = k[n_prompt:].reshape(dec_batch, dec_seq, ATT_HEADS, ATT_DH)
    vs = v[n_prompt:].reshape(dec_batch, dec_seq, ATT_HEADS, ATT_DH)
    return (y_prompt.reshape(batch, seq, D_MODEL), y_sample.reshape(dec_batch, dec_seq, D_MODEL),
            gla_p[None], gla_s[None], conv_p[None], conv_s[None], kp[None], vp[None], ks[None], vs[None])
```

```python
import functools

import jax
import jax.numpy as jnp
from jax import lax
from jax.experimental import pallas as pl
from jax.experimental.pallas import tpu as pltpu

F32 = jnp.float32
BF16 = jnp.bfloat16
HIGHEST = lax.Precision.HIGHEST

D_MODEL = 1024
CHUNK = 64
EPS = 1e-6
GLA_HEADS = 4
GLA_DK = 64
GLA_DV = 128
GLA_LOWRANK = 16
GLA_TAU = 16.0
GLA_SEPARABLE_MAX_EXPONENT = 40.0
CONV_CH = 512
CONV_WIDTH = 31
HIST = CONV_WIDTH - 1
ATT_HEADS = 16
ATT_DH = 64
BAND_CHUNKS_PAST = 8
BAND_PAST = BAND_CHUNKS_PAST * CHUNK
REL_CLIP = 256
N_GROUPS = 4
EXPERTS_PER_GROUP = 8
N_EXPERTS = N_GROUPS * EXPERTS_PER_GROUP
D_EXPERT = 512

LANES = 128
SUBLANES = 8
ROW_BLOCK = 512
CONV_TILE = 32
GMM_TILE = 256
ATT_QB = 256
ATT_WIN = ATT_QB + BAND_PAST
NEG_BIG = -1e30
VMEM_LIMIT = 56 * 1024 * 1024

QK_W = GLA_HEADS * GLA_DK
V_W = GLA_HEADS * GLA_DV
COL_Q = 0
COL_K = COL_Q + QK_W
COL_V = COL_K + QK_W
COL_GATE = COL_V + V_W
COL_CVAL = COL_GATE + V_W
COL_CGATE = COL_CVAL + CONV_CH
COL_ALR = COL_CGATE + CONV_CH
EVEN_COLS = COL_ALR + LANES


def _rmsnorm(x, g):
    return x * lax.rsqrt(jnp.mean(x * x, axis=-1, keepdims=True) + EPS) * g


def _sigmoid(x):
    return 1.0 / (1.0 + jnp.exp(-x))


def _dot(a, b):
    return jnp.dot(a, b, preferred_element_type=F32)


def _dot_nt(a, b):
    return lax.dot_general(a, b, (((1,), (1,)), ((), ())), preferred_element_type=F32)


def _dot_tn(a, b, precision=None):
    return lax.dot_general(a, b, (((0,), (0,)), ((), ())), preferred_element_type=F32, precision=precision)


def _even_mixer_kernel(*refs, chunk, cps, spb, carry, aliased):
    (x_ref, g_ref, win_ref, wau_ref, bal_ref, gng_ref, cw_ref, cb_ref, lng_ref, lnb_ref, wout_ref,
     s0_ref, c0_ref) = refs[:13]
    refs = refs[13 + (1 if aliased else 0):]
    h_ref, sfin_ref, cfin_ref, proj_ref, lg_ref, cum_ref, intra_ref, mix_ref, s_ref, ubuf_ref = refs
    j = pl.program_id(1)
    nj = pl.num_programs(1)
    seg = cps * chunk

    x = x_ref[...]
    xn = _rmsnorm(x, g_ref[...])
    proj_ref[...] = _dot(xn.astype(BF16), win_ref[...])
    alr = proj_ref[:, COL_ALR:COL_ALR + LANES]
    xa = _dot(alr.astype(BF16), wau_ref[...]) + bal_ref[...]
    lg_ref[...] = (jnp.minimum(xa, 0.0) - jnp.log1p(jnp.exp(-jnp.abs(xa)))) * (1.0 / GLA_TAU)

    row_i = lax.broadcasted_iota(jnp.int32, (chunk, QK_W), 0)
    tri_r = lax.broadcasted_iota(jnp.int32, (chunk, chunk), 0)
    tri_c = lax.broadcasted_iota(jnp.int32, (chunk, chunk), 1)
    causal = tri_r >= tri_c
    ones_cols = jnp.ones((chunk, LANES), F32)

    def intra_exact(r0):
        intra_ref[...] = jnp.zeros_like(intra_ref)
        k_all = proj_ref[pl.ds(r0, chunk), COL_K:COL_K + QK_W]
        sub_q = lax.broadcasted_iota(jnp.int32, (SUBLANES, QK_W), 0)
        sub_o = lax.broadcasted_iota(jnp.int32, (SUBLANES, V_W), 0)

        def row_body(i, c):
            grp = pl.multiple_of((i // SUBLANES) * SUBLANES, SUBLANES)
            pick = sub_q == (i % SUBLANES)
            cum_i = jnp.sum(jnp.where(pick, cum_ref[pl.ds(grp, SUBLANES), :], 0.0), axis=0, keepdims=True)
            q_rows = proj_ref[pl.ds(pl.multiple_of(r0 + grp, SUBLANES), SUBLANES), COL_Q:COL_Q + QK_W]
            q_i = jnp.sum(jnp.where(pick, q_rows, 0.0), axis=0, keepdims=True) * (GLA_DK ** -0.5)
            seen = row_i <= i
            p = jnp.where(seen, jnp.exp(jnp.where(seen, cum_i - cum_ref[...], 0.0)), 0.0) * k_all * q_i
            outs = []
            for h in range(GLA_HEADS):
                s_h = jnp.sum(p[:, h * GLA_DK:(h + 1) * GLA_DK], axis=-1, keepdims=True)
                v_h = proj_ref[pl.ds(r0, chunk), COL_V + h * GLA_DV:COL_V + (h + 1) * GLA_DV]
                outs.append(jnp.sum(s_h * v_h, axis=0, keepdims=True))
            o_row = jnp.concatenate(outs, axis=1)
            old = intra_ref[pl.ds(grp, SUBLANES), :]
            intra_ref[pl.ds(grp, SUBLANES), :] = jnp.where(sub_o == (i % SUBLANES), o_row, old)
            return c

        lax.fori_loop(0, chunk, row_body, 0)

    def chunk_body(ci, c, *, separable):
        r0 = pl.multiple_of(ci * chunk, chunk)
        sq = ci // cps if spb > 1 else 0
        first = (ci % cps) == 0
        last = (ci % cps) == (cps - 1)
        if carry:
            first = jnp.logical_and(first, j == 0)
            last = jnp.logical_and(last, j == nj - 1)

        @pl.when(first)
        def _():
            s_ref[...] = s0_ref[sq]

        lg = lg_ref[pl.ds(r0, chunk), :]
        cum = lg
        shift = 1
        while shift < chunk:
            cum = cum + jnp.where(row_i >= shift, pltpu.roll(cum, shift, 0), 0.0)
            shift *= 2
        cum_ref[...] = cum
        tot = cum_ref[pl.ds(chunk - 1, 1), :]
        tot_col = jnp.exp(_dot_tn(lg, ones_cols, precision=HIGHEST))
        e_in = jnp.exp(cum)
        e_s = jnp.exp(tot - cum)
        if separable:
            mid = cum_ref[pl.ds(chunk // 2 - 1, 1), :]
            e_q = jnp.exp(cum - mid)
            e_k = jnp.exp(mid - cum)
        else:
            intra_exact(r0)
        for h in range(GLA_HEADS):
            ks = slice(h * GLA_DK, (h + 1) * GLA_DK)
            vs = slice(h * GLA_DV, (h + 1) * GLA_DV)
            q = proj_ref[pl.ds(r0, chunk), COL_Q + h * GLA_DK:COL_Q + (h + 1) * GLA_DK] * (GLA_DK ** -0.5)
            k = proj_ref[pl.ds(r0, chunk), COL_K + h * GLA_DK:COL_K + (h + 1) * GLA_DK]
            v = proj_ref[pl.ds(r0, chunk), COL_V + h * GLA_DV:COL_V + (h + 1) * GLA_DV].astype(BF16)
            gate = proj_ref[pl.ds(r0, chunk), COL_GATE + h * GLA_DV:COL_GATE + (h + 1) * GLA_DV]
            s_old = s_ref[h]
            if separable:
                scores = _dot_nt((q * e_q[:, ks]).astype(BF16), (k * e_k[:, ks]).astype(BF16))
                intra = _dot(jnp.where(causal, scores, 0.0).astype(BF16), v)
            else:
                intra = intra_ref[:, vs]
            o = _dot((q * e_in[:, ks]).astype(BF16), s_old.astype(BF16)) + intra
            s_ref[h] = tot_col[h * GLA_DK:(h + 1) * GLA_DK, :] * s_old + _dot_tn((k * e_s[:, ks]).astype(BF16), v)
            o = o * lax.rsqrt(jnp.mean(o * o, axis=-1, keepdims=True) + EPS)
            o = o * gng_ref[:, vs] * (gate * _sigmoid(gate))
            mix_ref[pl.ds(r0, chunk), vs] = o.astype(BF16)

        @pl.when(last)
        def _():
            sfin_ref[sq] = s_ref[...]

        return c

    in_range = jnp.max(-lg_ref[...]) * (chunk // 2) < GLA_SEPARABLE_MAX_EXPONENT

    @pl.when(in_range)
    def _():
        lax.fori_loop(0, spb * cps, functools.partial(chunk_body, separable=True), 0, unroll=2)

    @pl.when(jnp.logical_not(in_range))
    def _():
        lax.fori_loop(0, spb * cps, functools.partial(chunk_body, separable=False), 0)

    tile = CONV_TILE

    def conv_tile(t0, out_r0):
        wv = ubuf_ref[pl.ds(t0, 2 * tile), :]
        acc = jnp.zeros((tile, CONV_CH), F32)
        for b in range(SUBLANES):
            sb = wv if b == 0 else pltpu.roll(wv, 2 * tile - b, 0)
            for a in range(tile // SUBLANES + 1):
                off = SUBLANES * a + b
                if 2 <= off <= HIST + 2:
                    acc = acc + sb[SUBLANES * a:SUBLANES * a + tile, :] * cw_ref[pl.ds(off - 2, 1), :]
        proj_ref[pl.ds(out_r0, tile), COL_CVAL:COL_CVAL + CONV_CH] = acc + cb_ref[...]

    def conv_norm(r0):
        cv = proj_ref[pl.ds(r0, seg), COL_CVAL:COL_CVAL + CONV_CH]
        mu = jnp.mean(cv, axis=-1, keepdims=True)
        var = jnp.mean(jnp.square(cv - mu), axis=-1, keepdims=True)
        y = (cv - mu) * lax.rsqrt(var + EPS) * lng_ref[...] + lnb_ref[...]
        mix_ref[pl.ds(r0, seg), V_W:V_W + CONV_CH] = (y * _sigmoid(y)).astype(BF16)

    def conv_seg(sq, c):
        r0 = pl.multiple_of(sq * seg, seg) if spb > 1 else 0
        cval = proj_ref[pl.ds(r0, seg), COL_CVAL:COL_CVAL + CONV_CH]
        cgate = proj_ref[pl.ds(r0, seg), COL_CGATE:COL_CGATE + CONV_CH]

        def load_history():
            ubuf_ref[pl.ds(0, 8), :] = jnp.zeros((8, CONV_CH), F32)
            ubuf_ref[pl.ds(2, HIST), :] = c0_ref[sq]

        if carry:
            pl.when(j == 0)(load_history)
        else:
            load_history()
        ubuf_ref[pl.ds(CONV_TILE, seg), :] = cval * _sigmoid(cgate)
        if seg == tile:
            conv_tile(0, r0)
        else:
            def tile_body(t, cc):
                t0 = pl.multiple_of(t * tile, tile)
                conv_tile(t0, r0 + t0)
                return cc
            lax.fori_loop(0, seg // tile, tile_body, 0)
        conv_norm(r0)
        hist = ubuf_ref[pl.ds(seg + 2, HIST), :]
        if carry:
            @pl.when(j == nj - 1)
            def _():
                cfin_ref[sq] = hist
            ubuf_ref[pl.ds(2, HIST), :] = hist
        else:
            cfin_ref[sq] = hist
        return c

    if spb > 1:
        lax.fori_loop(0, spb, conv_seg, 0)
    else:
        conv_seg(0, 0)

    h_ref[...] = _dot(mix_ref[...], wout_ref[...]) + x


def _even_mixer(x2d, hbuf, weights, s0, c0, *, n_rows_total, block0, n_seq, blocks_per_seq, chunk, cps, spb):
    carry = spb == 1
    aliased = hbuf is not None
    const = lambda s, j: (0, 0)
    seq_blk = (lambda s, j: (s, 0, 0, 0)) if carry else (lambda s, j: (0, 0, 0, 0))
    seq_blk3 = (lambda s, j: (s, 0, 0)) if carry else (lambda s, j: (0, 0, 0))
    n_state = 1 if carry else spb
    wspec = lambda shape: pl.BlockSpec(shape, const, pipeline_mode=pl.Buffered(1))
    in_specs = [
        pl.BlockSpec((ROW_BLOCK, D_MODEL), lambda s, j: (s * blocks_per_seq + j, 0)),
        wspec((1, D_MODEL)),
        wspec((D_MODEL, EVEN_COLS)),
        wspec((LANES, QK_W)),
        wspec((1, QK_W)),
        wspec((1, V_W)),
        wspec((CONV_WIDTH, CONV_CH)),
        wspec((1, CONV_CH)),
        wspec((1, CONV_CH)),
        wspec((1, CONV_CH)),
        wspec((V_W + CONV_CH, D_MODEL)),
        pl.BlockSpec((n_state, GLA_HEADS, GLA_DK, GLA_DV), seq_blk),
        pl.BlockSpec((n_state, HIST, CONV_CH), seq_blk3),
    ]
    args = [x2d, *weights, s0, c0]
    aliases = {}
    if aliased:
        in_specs.append(pl.BlockSpec(memory_space=pl.ANY))
        args.append(hbuf)
        aliases = {len(args) - 1: 0}
    n_all = s0.shape[0]
    out_shape = (
        jax.ShapeDtypeStruct((n_rows_total, D_MODEL), F32),
        jax.ShapeDtypeStruct((n_all, GLA_HEADS, GLA_DK, GLA_DV), F32),
        jax.ShapeDtypeStruct((n_all, HIST, CONV_CH), F32),
    )
    out_specs = (
        pl.BlockSpec((ROW_BLOCK, D_MODEL), lambda s, j: (block0 + s * blocks_per_seq + j, 0)),
        pl.BlockSpec((n_state, GLA_HEADS, GLA_DK, GLA_DV), seq_blk),
        pl.BlockSpec((n_state, HIST, CONV_CH), seq_blk3),
    )
    seg = cps * chunk
    scratch = [
        pltpu.VMEM((ROW_BLOCK, EVEN_COLS), F32),
        pltpu.VMEM((ROW_BLOCK, QK_W), F32),
        pltpu.VMEM((chunk, QK_W), F32),
        pltpu.VMEM((chunk, V_W), F32),
        pltpu.VMEM((ROW_BLOCK, V_W + CONV_CH), BF16),
        pltpu.VMEM((GLA_HEADS, GLA_DK, GLA_DV), F32),
        pltpu.VMEM((CONV_TILE + seg, CONV_CH), F32),
    ]
    kern = functools.partial(_even_mixer_kernel, chunk=chunk, cps=cps, spb=spb, carry=carry, aliased=aliased)
    return pl.pallas_call(
        kern, out_shape=out_shape, grid=(n_seq, blocks_per_seq), in_specs=in_specs, out_specs=out_specs,
        scratch_shapes=scratch, input_output_aliases=aliases,
        compiler_params=pltpu.CompilerParams(dimension_semantics=("arbitrary", "arbitrary"), vmem_limit_bytes=VMEM_LIMIT),
        name="even_mixer_carry" if carry else "even_mixer_step",
    )(*args)


def _to_row_tiles(ref, x):
    rows = x.shape[0]
    for c in range(x.shape[1] // LANES):
        ref[pl.ds(c, rows, stride=SUBLANES), :] = x[:, c * LANES:(c + 1) * LANES]


def _from_row_tiles(ref, rows, width=D_MODEL):
    return jnp.concatenate([ref[pl.ds(c, rows, stride=SUBLANES), :] for c in range(width // LANES)], axis=1)


def _route_kernel(h_ref, g_ref, wr_ref, br_ref, xn_ref, mi_ref, mw_ref, cnt_ref, base_ref):
    i = pl.program_id(0)

    @pl.when(i == 0)
    def _():
        base_ref[...] = jnp.zeros_like(base_ref)

    xn = _rmsnorm(h_ref[...], g_ref[...])
    _to_row_tiles(xn_ref, xn)
    logits = jnp.dot(xn, wr_ref[...], precision=HIGHEST, preferred_element_type=F32) + br_ref[...]
    lane = lax.broadcasted_iota(jnp.int32, logits.shape, 1).astype(F32)
    far = float(1 << 20)
    gl = jnp.where(lane < N_GROUPS, logits, -jnp.inf)
    gmax = jnp.max(gl, axis=-1, keepdims=True)
    gidx = jnp.min(jnp.where(gl == gmax, lane, far), axis=-1, keepdims=True)
    gw = 1.0 / jnp.sum(jnp.exp(gl - gmax), axis=-1, keepdims=True)
    lo = N_GROUPS + gidx * EXPERTS_PER_GROUP
    el = jnp.where(jnp.logical_and(lane >= lo, lane < lo + EXPERTS_PER_GROUP), logits, -jnp.inf)
    m1 = jnp.max(el, axis=-1, keepdims=True)
    i1 = jnp.min(jnp.where(el == m1, lane, far), axis=-1, keepdims=True)
    el2 = jnp.where(lane == i1, -jnp.inf, el)
    m2 = jnp.max(el2, axis=-1, keepdims=True)
    i2 = jnp.min(jnp.where(el2 == m2, lane, far), axis=-1, keepdims=True)
    e2 = jnp.exp(m2 - m1)
    den = 1.0 + e2
    w0 = (1.0 / den) * gw
    w1 = (e2 / den) * gw
    id0 = i1 - N_GROUPS
    id1 = i2 - N_GROUPS
    oh0 = jnp.where(lane == id0, 1.0, 0.0)
    oh1 = jnp.where(lane == id1, 1.0, 0.0)
    rr = lax.broadcasted_iota(jnp.int32, (ROW_BLOCK, ROW_BLOCK), 0)
    cc = lax.broadcasted_iota(jnp.int32, (ROW_BLOCK, ROW_BLOCK), 1)
    below = jnp.where(rr > cc, 1.0, 0.0).astype(BF16)
    p0 = _dot(below, oh0.astype(BF16))
    p1 = _dot(below, oh1.astype(BF16))
    cnt0 = jnp.sum(oh0, axis=0, keepdims=True)
    cnt1 = jnp.sum(oh1, axis=0, keepdims=True)
    base = base_ref[...]
    rank0 = jnp.sum(oh0 * (p0 + base), axis=-1, keepdims=True)
    rank1 = jnp.sum(oh1 * (p1 + base + cnt0), axis=-1, keepdims=True)
    new_base = base + cnt0 + cnt1
    base_ref[...] = new_base
    cnt_ref[...] = new_base.astype(jnp.int32)
    meta = jnp.where(lane == 0, id0, jnp.where(lane == 1, id1, jnp.where(lane == 2, rank0, jnp.where(lane == 3, rank1, 0.0))))
    mi_ref[0] = meta.T[0:8, :].astype(jnp.int32)
    mw_ref[...] = jnp.where(lane == 0, w0, jnp.where(lane == 1, w1, 0.0))


def _route(h, g, wr, br, n_blocks):
    n_rows = n_blocks * ROW_BLOCK
    const = lambda i: (0, 0)
    return pl.pallas_call(
        _route_kernel,
        out_shape=(
            jax.ShapeDtypeStruct((n_rows * SUBLANES, LANES), F32),
            jax.ShapeDtypeStruct((n_blocks, 8, ROW_BLOCK), jnp.int32),
            jax.ShapeDtypeStruct((n_rows, LANES), F32),
            jax.ShapeDtypeStruct((1, LANES), jnp.int32),
        ),
        grid=(n_blocks,),
        in_specs=[
            pl.BlockSpec((ROW_BLOCK, D_MODEL), lambda i: (i, 0)),
            pl.BlockSpec((1, D_MODEL), const),
            pl.BlockSpec((D_MODEL, LANES), const),
            pl.BlockSpec((1, LANES), const),
        ],
        out_specs=(
            pl.BlockSpec((ROW_BLOCK * SUBLANES, LANES), lambda i: (i, 0)),
            pl.BlockSpec((1, 8, ROW_BLOCK), lambda i: (i, 0, 0)),
            pl.BlockSpec((ROW_BLOCK, LANES), lambda i: (i, 0)),
            pl.BlockSpec((1, LANES), const),
        ),
        scratch_shapes=[pltpu.VMEM((1, LANES), F32)],
        compiler_params=pltpu.CompilerParams(dimension_semantics=("arbitrary",), vmem_limit_bytes=VMEM_LIMIT),
        name="moe_route",
    )(h, g, wr, br)


def _slot_index_kernel(dest_ref, tok0_ref, dst0_ref, tok_ref, dst_ref, sem, *, n_blocks, n_rows):
    init_tok = pltpu.make_async_copy(tok0_ref, tok_ref, sem.at[0])
    init_dst = pltpu.make_async_copy(dst0_ref, dst_ref, sem.at[1])
    init_tok.start()
    init_dst.start()
    init_tok.wait()
    init_dst.wait()

    def block(b, c):
        for k in range(2):
            def body(j, cc):
                d = dest_ref[b * (2 * ROW_BLOCK) + k * ROW_BLOCK + j]
                t = b * ROW_BLOCK + j
                tok_ref[d] = t * SUBLANES
                dst_ref[d] = (k * n_rows + t) * SUBLANES
                return cc

            lax.fori_loop(0, ROW_BLOCK, body, 0, unroll=8)
        return c

    lax.fori_loop(0, n_blocks, block, 0)


def _slot_index(dest_flat, tok0, dst0, n_blocks, n_rows):
    n_slots = tok0.shape[0]
    return pl.pallas_call(
        functools.partial(_slot_index_kernel, n_blocks=n_blocks, n_rows=n_rows),
        out_shape=(jax.ShapeDtypeStruct((n_slots,), jnp.int32), jax.ShapeDtypeStruct((n_slots,), jnp.int32)),
        in_specs=[pl.BlockSpec(memory_space=pltpu.SMEM), pl.BlockSpec(memory_space=pl.ANY),
                  pl.BlockSpec(memory_space=pl.ANY)],
        out_specs=(pl.BlockSpec(memory_space=pltpu.SMEM), pl.BlockSpec(memory_space=pltpu.SMEM)),
        scratch_shapes=[pltpu.SemaphoreType.DMA((2,))],
        name="moe_slot_index",
    )(dest_flat, tok0, dst0)


GMM_ISSUE_GROUPS = 8
GMM_BUFFERS = 3
GATHER_PRIORITY = 0
SCATTER_PRIORITY = 1


def _gmm_kernel(te_ref, nv_ref, tok_ref, dst_ref, xn_ref, wg_ref, wu_ref, wd_ref, yp_ref,
                xbuf, ybuf, wgb, wub, wdb, gsem, ssem):
    i = pl.program_id(0)
    nv = nv_ref[0]
    n_tiles = pl.num_programs(0) - 1
    slot = i % GMM_BUFFERS
    nxt = (i + 1) % GMM_BUFFERS
    prv = (i + 2) % GMM_BUFFERS

    def gather_copy(tile, r, buf):
        src = pl.multiple_of(tok_ref[tile * GMM_TILE + r], SUBLANES)
        return pltpu.make_async_copy(xn_ref.at[pl.ds(src, SUBLANES), :],
                                     xbuf.at[buf, pl.ds(r * SUBLANES, SUBLANES), :], gsem.at[buf])

    def scatter_copy(tile, r, buf):
        d = pl.multiple_of(dst_ref[tile * GMM_TILE + r], SUBLANES)
        return pltpu.make_async_copy(ybuf.at[buf, pl.ds(r * SUBLANES, SUBLANES), :],
                                     yp_ref.at[pl.ds(d, SUBLANES), :], ssem.at[buf])

    def wait_tile(copy_of_row, buf):
        def body(r, c):
            copy_of_row(0, 0, buf).wait()
            return c

        lax.fori_loop(0, GMM_TILE, body, 0, unroll=16)

    @pl.when(i == 0)
    def _():
        ybuf[...] = jnp.zeros_like(ybuf)

        def body(r, c):
            gather_copy(0, r, 0).start(priority=GATHER_PRIORITY)
            gather_copy(1, r, 1).start(priority=GATHER_PRIORITY)
            return c

        lax.fori_loop(0, GMM_TILE, body, 0, unroll=8)

    @pl.when(jnp.logical_and(i >= 2, i <= nv))
    def _():
        wait_tile(scatter_copy, slot)

    @pl.when(i < nv)
    def _():
        wait_tile(gather_copy, slot)
        prev = te_ref[jnp.maximum(i - 1, 0)]

        @pl.when(jnp.logical_or(i == 0, te_ref[i] != prev))
        def _():
            wgb[...] = wg_ref[0, 0].astype(BF16)
            wub[...] = wu_ref[0, 0].astype(BF16)
            wdb[...] = wd_ref[0, 0].astype(BF16)

        prev_tile = jnp.where(i >= 1, i - 1, n_tiles)
        rows_per_group = GMM_TILE // GMM_ISSUE_GROUPS

        def issue(group):
            for r in range(group * rows_per_group, (group + 1) * rows_per_group):
                gather_copy(i + 2, r, prv).start(priority=GATHER_PRIORITY)
                scatter_copy(prev_tile, r, prv).start(priority=SCATTER_PRIORITY)

        x = _from_row_tiles(xbuf.at[slot], GMM_TILE).astype(BF16)
        half = D_EXPERT // 2
        quarter = D_MODEL // 4
        a0 = _dot(x, wgb[:, 0:half])
        issue(0)
        a1 = _dot(x, wgb[:, half:D_EXPERT])
        issue(1)
        b0 = _dot(x, wub[:, 0:half])
        issue(2)
        b1 = _dot(x, wub[:, half:D_EXPERT])
        issue(3)
        he = jnp.concatenate([(a0 * _sigmoid(a0)) * b0, (a1 * _sigmoid(a1)) * b1], axis=1).astype(BF16)
        for n in range(4):
            yn = _dot(he, wdb[:, n * quarter:(n + 1) * quarter])
            for c in range(quarter // LANES):
                ybuf[slot, pl.ds(n * (quarter // LANES) + c, GMM_TILE, stride=SUBLANES), :] = yn[:, c * LANES:(c + 1) * LANES]
            issue(4 + n)

    @pl.when(i == nv)
    def _():
        wait_tile(gather_copy, slot)
        wait_tile(gather_copy, nxt)
        wait_tile(scatter_copy, nxt)

        def body(r, c):
            scatter_copy(nv - 1, r, prv).start(priority=SCATTER_PRIORITY)
            return c

        lax.fori_loop(0, GMM_TILE, body, 0, unroll=8)
        wait_tile(scatter_copy, prv)


def _gmm(tile_expert, n_valid, tok, dst, xn_tiles, w_gate, w_up, w_down, layer):
    n_rows = xn_tiles.shape[0] // SUBLANES
    n_tiles = tile_expert.shape[0] - 1

    def w_idx(i, te, nv, tok, dst):
        return (layer, te[jnp.maximum(jnp.minimum(i, nv[0] - 1), 0)], 0, 0)

    return pl.pallas_call(
        _gmm_kernel,
        out_shape=jax.ShapeDtypeStruct(((2 * n_rows + GMM_BUFFERS * GMM_TILE) * SUBLANES, LANES), F32),
        grid_spec=pltpu.PrefetchScalarGridSpec(
            num_scalar_prefetch=4,
            grid=(n_tiles + 1,),
            in_specs=[
                pl.BlockSpec(memory_space=pl.ANY),
                pl.BlockSpec((1, 1, D_MODEL, D_EXPERT), w_idx),
                pl.BlockSpec((1, 1, D_MODEL, D_EXPERT), w_idx),
                pl.BlockSpec((1, 1, D_EXPERT, D_MODEL), w_idx),
            ],
            out_specs=pl.BlockSpec(memory_space=pl.ANY),
            scratch_shapes=[
                pltpu.VMEM((GMM_BUFFERS, GMM_TILE * SUBLANES, LANES), F32),
                pltpu.VMEM((GMM_BUFFERS, GMM_TILE * SUBLANES, LANES), F32),
                pltpu.VMEM((D_MODEL, D_EXPERT), BF16),
                pltpu.VMEM((D_MODEL, D_EXPERT), BF16),
                pltpu.VMEM((D_EXPERT, D_MODEL), BF16),
                pltpu.SemaphoreType.DMA((GMM_BUFFERS,)),
                pltpu.SemaphoreType.DMA((GMM_BUFFERS,)),
            ],
        ),
        compiler_params=pltpu.CompilerParams(dimension_semantics=("arbitrary",), vmem_limit_bytes=VMEM_LIMIT),
        name="moe_gmm",
    )(tile_expert, n_valid, tok, dst, xn_tiles, w_gate, w_up, w_down)


def _combine_kernel(*refs, final):
    if final:
        h_ref, mw_ref, y0_ref, y1_ref, gfin_ref, out_ref = refs
    else:
        h_ref, mw_ref, y0_ref, y1_ref, out_ref = refs
    mw = mw_ref[...]
    y0 = _from_row_tiles(y0_ref, ROW_BLOCK)
    y1 = _from_row_tiles(y1_ref, ROW_BLOCK)
    hn = h_ref[...] + (mw[:, 0:1] * y0 + mw[:, 1:2] * y1)
    out_ref[...] = _rmsnorm(hn, gfin_ref[...]) if final else hn


def _combine(h, mw, yp, gfin, *, block0, n_blocks):
    final = gfin is not None
    total_blocks = h.shape[0] // ROW_BLOCK
    in_specs = [
        pl.BlockSpec((ROW_BLOCK, D_MODEL), lambda i: (block0 + i, 0)),
        pl.BlockSpec((ROW_BLOCK, LANES), lambda i: (block0 + i, 0)),
        pl.BlockSpec((ROW_BLOCK * SUBLANES, LANES), lambda i: (block0 + i, 0)),
        pl.BlockSpec((ROW_BLOCK * SUBLANES, LANES), lambda i: (total_blocks + block0 + i, 0)),
    ]
    args = [h, mw, yp, yp]
    if final:
        in_specs.append(pl.BlockSpec((1, D_MODEL), lambda i: (0, 0)))
        args.append(gfin)
    return pl.pallas_call(
        functools.partial(_combine_kernel, final=final),
        out_shape=jax.ShapeDtypeStruct((n_blocks * ROW_BLOCK, D_MODEL), F32),
        grid=(n_blocks,),
        in_specs=in_specs,
        out_specs=pl.BlockSpec((ROW_BLOCK, D_MODEL), lambda i: (i, 0)),
        compiler_params=pltpu.CompilerParams(dimension_semantics=("arbitrary",), vmem_limit_bytes=VMEM_LIMIT),
        name="moe_combine_final" if final else "moe_combine",
    )(*args)


def _moe_tiles(n_rows):
    return (2 * n_rows + N_EXPERTS * (GMM_TILE - 1) + GMM_TILE - 1) // GMM_TILE


def _hier_moe(h, g, w_rg, b_rg, w_re, b_re, w_gate, w_up, w_down, layer, gfin, splits):
    n_rows = h.shape[0]
    n_blocks = n_rows // ROW_BLOCK
    n_re = N_GROUPS * EXPERTS_PER_GROUP
    wr = jnp.zeros((D_MODEL, LANES), F32)
    wr = wr.at[:, :N_GROUPS].set(w_rg).at[:, N_GROUPS:N_GROUPS + n_re].set(w_re.reshape(D_MODEL, n_re))
    br = jnp.zeros((1, LANES), F32)
    br = br.at[0, :N_GROUPS].set(b_rg).at[0, N_GROUPS:N_GROUPS + n_re].set(b_re.reshape(n_re))
    xn_tiles, meta_i, meta_w, counts = _route(h, g.reshape(1, D_MODEL), wr, br, n_blocks)

    n_tiles = _moe_tiles(n_rows)
    cnt = counts[0, :N_EXPERTS]
    padded = ((cnt + GMM_TILE - 1) // GMM_TILE) * GMM_TILE
    ends = jnp.cumsum(padded)
    offs = ends - padded
    tile_start = jnp.arange(n_tiles + 1, dtype=jnp.int32) * GMM_TILE
    tile_expert = jnp.minimum(jnp.sum((tile_start[:, None] >= ends[None, :]).astype(jnp.int32), axis=1), N_EXPERTS - 1)
    n_valid = (ends[-1:] // GMM_TILE).astype(jnp.int32)
    eid = meta_i[:, 0:2, :]
    seg_start = jnp.sum(jnp.where(eid[..., None] == jnp.arange(N_EXPERTS, dtype=jnp.int32), offs, 0), axis=-1)
    dest = (seg_start + meta_i[:, 2:4, :]).astype(jnp.int32).reshape(-1)
    slots = jnp.arange((n_tiles + GMM_BUFFERS - 1) * GMM_TILE, dtype=jnp.int32)
    scratch_rows = 2 * n_rows + slots % (GMM_BUFFERS * GMM_TILE)
    tok, dst = _slot_index(dest, jnp.zeros_like(slots), scratch_rows * SUBLANES, n_blocks, n_rows)

    yp = _gmm(tile_expert.astype(jnp.int32), n_valid, tok, dst, xn_tiles, w_gate, w_up, w_down, layer)
    gf = None if gfin is None else gfin.reshape(1, D_MODEL)
    return [_combine(h, meta_w, yp, gf, block0=b0, n_blocks=nb) for (b0, nb) in splits]


def _qkv_kernel(h_ref, g_ref, w_ref, q_ref, k_ref, v_ref):
    xn = _rmsnorm(h_ref[...], g_ref[...]).astype(BF16)
    width = ATT_HEADS * ATT_DH
    q_ref[...] = (_dot(xn, w_ref[:, 0:width]) * (ATT_DH ** -0.5)).astype(BF16)
    k_ref[...] = _dot(xn, w_ref[:, width:2 * width])
    v_ref[...] = _dot(xn, w_ref[:, 2 * width:3 * width])


def _qkv(h, g, w_bf16, n_blocks):
    n_rows = n_blocks * ROW_BLOCK
    width = ATT_HEADS * ATT_DH
    row_blk = lambda i: (i, 0)
    return pl.pallas_call(
        _qkv_kernel,
        out_shape=(
            jax.ShapeDtypeStruct((n_rows, width), BF16),
            jax.ShapeDtypeStruct((n_rows, width), F32),
            jax.ShapeDtypeStruct((n_rows, width), F32),
        ),
        grid=(n_blocks,),
        in_specs=[
            pl.BlockSpec((ROW_BLOCK, D_MODEL), row_blk),
            pl.BlockSpec((1, D_MODEL), lambda i: (0, 0)),
            pl.BlockSpec((D_MODEL, 3 * width), lambda i: (0, 0), pipeline_mode=pl.Buffered(1)),
        ],
        out_specs=(
            pl.BlockSpec((ROW_BLOCK, width), row_blk),
            pl.BlockSpec((ROW_BLOCK, width), row_blk),
            pl.BlockSpec((ROW_BLOCK, width), row_blk),
        ),
        compiler_params=pltpu.CompilerParams(dimension_semantics=("arbitrary",), vmem_limit_bytes=VMEM_LIMIT),
        name="qkv_proj",
    )(h, g, w_bf16)


def _softmax_parts(parts):
    m = functools.reduce(jnp.maximum, [jnp.max(s, axis=-1, keepdims=True) for s in parts])
    ps = [jnp.exp(s - m) for s in parts]
    inv = 1.0 / functools.reduce(jnp.add, [jnp.sum(p, axis=-1, keepdims=True) for p in ps])
    return ps, inv


def _band_prompt_kernel(h_ref, q_ref, ka_ref, kb_ref, kc_ref, va_ref, vb_ref, vc_ref, bm_ref, wout_ref,
                        out_ref, kw_ref, vw_ref, o_ref):
    i = pl.program_id(1)
    for b, (k_ref, v_ref) in enumerate(((ka_ref, va_ref), (kb_ref, vb_ref), (kc_ref, vc_ref))):
        kw_ref[pl.ds(b * ATT_QB, ATT_QB), :] = k_ref[...].astype(BF16)
        vw_ref[pl.ds(b * ATT_QB, ATT_QB), :] = v_ref[...].astype(BF16)
    col = lax.broadcasted_iota(jnp.int32, (1, ATT_WIN), 1)
    first_valid = (2 - jnp.minimum(i, 2)) * ATT_QB
    col_mask = jnp.where(col >= first_valid, 0.0, NEG_BIG)
    for h in range(ATT_HEADS):
        hs = slice(h * ATT_DH, (h + 1) * ATT_DH)
        s = _dot_nt(q_ref[:, hs], kw_ref[:, hs]) + bm_ref[h] + col_mask
        (p,), inv = _softmax_parts([s])
        o_ref[:, hs] = (_dot(p.astype(BF16), vw_ref[:, hs]) * inv).astype(BF16)
    out_ref[...] = _dot(o_ref[...], wout_ref[...]) + h_ref[...]


def _band_prompt(h, q, k, v, bm, wout, *, n_rows_total, n_seq, blocks_per_seq):
    width = ATT_HEADS * ATT_DH
    blk = lambda s, i: (s * blocks_per_seq + i, 0)
    back = lambda n: (lambda s, i: (s * blocks_per_seq + jnp.maximum(i - n, 0), 0))
    kv_spec = lambda n: pl.BlockSpec((ATT_QB, width), back(n))
    return pl.pallas_call(
        _band_prompt_kernel,
        out_shape=jax.ShapeDtypeStruct((n_rows_total, D_MODEL), F32),
        grid=(n_seq, blocks_per_seq),
        in_specs=[
            pl.BlockSpec((ATT_QB, D_MODEL), blk),
            pl.BlockSpec((ATT_QB, width), blk),
            kv_spec(2), kv_spec(1), kv_spec(0),
            kv_spec(2), kv_spec(1), kv_spec(0),
            pl.BlockSpec((ATT_HEADS, ATT_QB, ATT_WIN), lambda s, i: (0, 0, 0), pipeline_mode=pl.Buffered(1)),
            pl.BlockSpec((width, D_MODEL), lambda s, i: (0, 0), pipeline_mode=pl.Buffered(1)),
        ],
        out_specs=pl.BlockSpec((ATT_QB, D_MODEL), blk),
        scratch_shapes=[
            pltpu.VMEM((ATT_WIN, width), BF16),
            pltpu.VMEM((ATT_WIN, width), BF16),
            pltpu.VMEM((ATT_QB, width), BF16),
        ],
        compiler_params=pltpu.CompilerParams(dimension_semantics=("arbitrary", "arbitrary"), vmem_limit_bytes=VMEM_LIMIT),
        name="band_attn_prompt",
    )(h, q, k, k, k, v, v, v, bm, wout)


def _band_sample_kernel(h_ref, q_ref, kn_ref, vn_ref, ck_ref, cv_ref, b1_ref, b2_ref, wout_ref, hbuf_ref,
                        out_ref, o_ref):
    del hbuf_ref
    for h in range(ATT_HEADS):
        hs = slice(h * ATT_DH, (h + 1) * ATT_DH)
        qh = q_ref[:, hs]
        s1 = _dot(qh, ck_ref[0, h].astype(BF16)) + b1_ref[h]
        s2 = _dot_nt(qh, kn_ref[:, hs].astype(BF16)) + b2_ref[h]
        (p1, p2), inv = _softmax_parts([s1, s2])
        o = _dot_nt(p1.astype(BF16), cv_ref[0, h].astype(BF16)) + _dot(p2.astype(BF16), vn_ref[:, hs].astype(BF16))
        o_ref[:, hs] = (o * inv).astype(BF16)
    out_ref[...] = _dot(o_ref[...], wout_ref[...]) + h_ref[...]


def _band_sample(h, q, k, v, cache_k, cache_v, b1, b2, wout, hbuf, *, row0, n_seq, seq_len):
    width = ATT_HEADS * ATT_DH
    past = cache_k.shape[3]
    blk0 = row0 // seq_len
    blk = lambda s: (blk0 + s, 0)
    const3 = lambda s: (0, 0, 0)
    return pl.pallas_call(
        _band_sample_kernel,
        out_shape=jax.ShapeDtypeStruct(hbuf.shape, F32),
        grid=(n_seq,),
        in_specs=[
            pl.BlockSpec((seq_len, D_MODEL), blk),
            pl.BlockSpec((seq_len, width), blk),
            pl.BlockSpec((seq_len, width), blk),
            pl.BlockSpec((seq_len, width), blk),
            pl.BlockSpec((1, ATT_HEADS, ATT_DH, past), lambda s: (s, 0, 0, 0)),
            pl.BlockSpec((1, ATT_HEADS, ATT_DH, past), lambda s: (s, 0, 0, 0)),
            pl.BlockSpec((ATT_HEADS, seq_len, past), const3),
            pl.BlockSpec((ATT_HEADS, seq_len, seq_len), const3),
            pl.BlockSpec((width, D_MODEL), lambda s: (0, 0)),
            pl.BlockSpec(memory_space=pl.ANY),
        ],
        out_specs=pl.BlockSpec((seq_len, D_MODEL), blk),
        scratch_shapes=[pltpu.VMEM((seq_len, width), BF16)],
        input_output_aliases={9: 0},
        compiler_params=pltpu.CompilerParams(dimension_semantics=("arbitrary",), vmem_limit_bytes=VMEM_LIMIT),
        name="band_attn_sample",
    )(h, q, k, v, cache_k, cache_v, b1, b2, wout, hbuf)


def _band_bias_table(rb):
    n_top = BAND_PAST - REL_CLIP + 1
    n_var = REL_CLIP + CHUNK - 2
    n_bot = ATT_WIN - n_top - n_var
    n = ATT_WIN + ATT_QB
    heads = rb.shape[0]
    top = jnp.broadcast_to(rb[:, n_var + 1:n_var + 2], (heads, n_top))
    by_d = jnp.concatenate([top, rb[:, 1:n_var + 1][:, ::-1], jnp.broadcast_to(rb[:, 0:1], (heads, n_bot)),
                            jnp.broadcast_to(rb[:, n_var + 1:n_var + 2], (heads, ATT_QB))], axis=1)
    shifted = jnp.tile(by_d, (1, ATT_QB))[:, :ATT_QB * (n - 1)].reshape(heads, ATT_QB, n - 1)
    return shifted[:, :, :ATT_WIN]


def kernel(x_prompt, x_sample, state_gla, state_conv, cache_band_k, cache_band_v, norm_mix_g, norm_ffn_g, norm_final_g, w_in_even, w_alpha_up, b_alpha, gla_norm_g, conv_w, conv_b, conv_ln_g, conv_ln_b, w_out_even, w_qkv_odd, rel_bias, w_out_odd, w_router_grp, b_router_grp, w_router_exp, b_router_exp, w_exp_gate, w_exp_up, w_exp_down):
    batch, seq, _ = x_prompt.shape
    dec_batch, dec_seq, _ = x_sample.shape
    n_prompt = batch * seq
    n_sample = dec_batch * dec_seq
    assert seq % ROW_BLOCK == 0 and n_sample == ROW_BLOCK and seq % ATT_QB == 0
    n_rows = n_prompt + n_sample
    prompt_blocks = n_prompt // ROW_BLOCK
    n_blocks = n_rows // ROW_BLOCK
    width = ATT_HEADS * ATT_DH
    past = cache_band_k.shape[2]

    w_in = w_in_even[0]
    c0, c1, c2, c3, c4, c5 = (QK_W, 2 * QK_W, 2 * QK_W + V_W, 2 * QK_W + 2 * V_W,
                              2 * QK_W + 2 * V_W + GLA_LOWRANK, 2 * QK_W + 2 * V_W + GLA_LOWRANK + CONV_CH)
    w_in_p = jnp.concatenate(
        [w_in[:, :c3], w_in[:, c4:], w_in[:, c3:c4], jnp.zeros((D_MODEL, LANES - GLA_LOWRANK), F32)], axis=1).astype(BF16)
    wau_p = jnp.concatenate([w_alpha_up[0], jnp.zeros((LANES - GLA_LOWRANK, QK_W), F32)], axis=0).astype(BF16)
    even_w = (norm_mix_g[0].reshape(1, D_MODEL), w_in_p, wau_p, b_alpha[0].reshape(1, QK_W),
              gla_norm_g[0].reshape(1, V_W), conv_w[0], conv_b[0].reshape(1, CONV_CH),
              conv_ln_g[0].reshape(1, CONV_CH), conv_ln_b[0].reshape(1, CONV_CH), w_out_even[0].astype(BF16))
    h1, gla_p, conv_p = _even_mixer(
        x_prompt.reshape(n_prompt, D_MODEL), None, even_w,
        jnp.zeros((batch, GLA_HEADS, GLA_DK, GLA_DV), F32), jnp.zeros((batch, HIST, CONV_CH), F32),
        n_rows_total=n_rows, block0=0, n_seq=batch, blocks_per_seq=seq // ROW_BLOCK,
        chunk=CHUNK, cps=ROW_BLOCK // CHUNK, spb=1)
    h1, gla_s, conv_s = _even_mixer(
        x_sample.reshape(n_sample, D_MODEL), h1, even_w, state_gla[0], state_conv[0],
        n_rows_total=n_rows, block0=prompt_blocks, n_seq=1, blocks_per_seq=1,
        chunk=dec_seq, cps=1, spb=dec_batch)

    def moe(h, layer, gfin, splits):
        return _hier_moe(h, norm_ffn_g[layer], w_router_grp[layer], b_router_grp[layer], w_router_exp[layer],
                         b_router_exp[layer], w_exp_gate, w_exp_up, w_exp_down, layer, gfin, splits)

    (h2,) = moe(h1, 0, None, [(0, n_blocks)])

    q, k, v = _qkv(h2, norm_mix_g[1].reshape(1, D_MODEL), w_qkv_odd[0].astype(BF16), n_blocks)
    toe = _band_bias_table(rel_bias[0])
    r = jnp.arange(ATT_QB)[:, None]
    jb = jnp.arange(ATT_WIN)[None, :] - CHUNK * (r // CHUNK)
    in_band = jnp.logical_and(jb >= 0, jb < BAND_PAST + CHUNK)
    bm = jnp.where(in_band[None], toe, NEG_BIG)
    wout_odd = w_out_odd[0].astype(BF16)
    h3 = _band_prompt(h2, q, k, v, bm, wout_odd, n_rows_total=n_rows, n_seq=batch, blocks_per_seq=seq // ATT_QB)
    assert past == BAND_PAST and dec_seq <= CHUNK
    b1 = toe[:, :dec_seq, :past]
    b2 = toe[:, :dec_seq, past:past + dec_seq]
    h3 = _band_sample(h2, q, k, v, jnp.transpose(cache_band_k[0], (0, 2, 3, 1)),
                      jnp.transpose(cache_band_v[0], (0, 2, 3, 1)), b1, b2, wout_odd, h3,
                      row0=n_prompt, n_seq=dec_batch, seq_len=dec_seq)

    y_prompt, y_sample = moe(h3, 1, norm_final_g, [(0, prompt_blocks), (prompt_blocks, n_blocks - prompt_blocks)])

    rows = min(BAND_PAST, seq)
    tail = lambda a: jnp.stack([a[b * seq + seq - rows:(b + 1) * seq] for b in range(batch)]).reshape(
        batch, rows, ATT_HEADS, ATT_DH)
    kp, vp = tail(k), tail(v)
    ks = k[n_prompt:].reshape(dec_batch, dec_seq, ATT_HEADS, ATT_DH)
    vs = v[n_prompt:].reshape(dec_batch, dec_seq, ATT_HEADS, ATT_DH)
    return (y_prompt.reshape(batch, seq, D_MODEL), y_sample.reshape(dec_batch, dec_seq, D_MODEL),
            gla_p[None], gla_s[None], conv_p[None], conv_s[None], kp[None], vp[None], ks[None], vs[None])
```

```python
import functools

import jax
import jax.numpy as jnp
from jax import lax
from jax.experimental import pallas as pl
from jax.experimental.pallas import tpu as pltpu

F32 = jnp.float32
BF16 = jnp.bfloat16
HIGHEST = lax.Precision.HIGHEST

D_MODEL = 1024
CHUNK = 64
EPS = 1e-6
GLA_HEADS = 4
GLA_DK = 64
GLA_DV = 128
GLA_LOWRANK = 16
GLA_TAU = 16.0
GLA_SEPARABLE_MAX_EXPONENT = 40.0
CONV_CH = 512
CONV_WIDTH = 31
HIST = CONV_WIDTH - 1
ATT_HEADS = 16
ATT_DH = 64
BAND_CHUNKS_PAST = 8
BAND_PAST = BAND_CHUNKS_PAST * CHUNK
REL_CLIP = 256
N_GROUPS = 4
EXPERTS_PER_GROUP = 8
N_EXPERTS = N_GROUPS * EXPERTS_PER_GROUP
D_EXPERT = 512

LANES = 128
SUBLANES = 8
ROW_BLOCK = 512
CONV_TILE = 32
GMM_TILE = 256
ATT_QB = 256
ATT_WIN = ATT_QB + BAND_PAST
NEG_BIG = -1e30
VMEM_LIMIT = 56 * 1024 * 1024

QK_W = GLA_HEADS * GLA_DK
V_W = GLA_HEADS * GLA_DV
COL_Q = 0
COL_K = COL_Q + QK_W
COL_V = COL_K + QK_W
COL_GATE = COL_V + V_W
COL_CVAL = COL_GATE + V_W
COL_CGATE = COL_CVAL + CONV_CH
COL_ALR = COL_CGATE + CONV_CH
EVEN_COLS = COL_ALR + LANES


def _rmsnorm(x, g):
    return x * lax.rsqrt(jnp.mean(x * x, axis=-1, keepdims=True) + EPS) * g


def _sigmoid(x):
    return 1.0 / (1.0 + jnp.exp(-x))


def _dot(a, b):
    return jnp.dot(a, b, preferred_element_type=F32)


def _dot_nt(a, b):
    return lax.dot_general(a, b, (((1,), (1,)), ((), ())), preferred_element_type=F32)


def _dot_tn(a, b, precision=None):
    return lax.dot_general(a, b, (((0,), (0,)), ((), ())), preferred_element_type=F32, precision=precision)


def _even_mixer_kernel(*refs, chunk, cps, spb, carry, aliased):
    (x_ref, g_ref, win_ref, wau_ref, bal_ref, gng_ref, cw_ref, cb_ref, lng_ref, lnb_ref, wout_ref,
     s0_ref, c0_ref) = refs[:13]
    refs = refs[13 + (1 if aliased else 0):]
    h_ref, sfin_ref, cfin_ref, proj_ref, lg_ref, cum_ref, intra_ref, mix_ref, s_ref, ubuf_ref = refs
    j = pl.program_id(1)
    nj = pl.num_programs(1)
    seg = cps * chunk

    x = x_ref[...]
    xn = _rmsnorm(x, g_ref[...])
    proj_ref[...] = _dot(xn.astype(BF16), win_ref[...])
    alr = proj_ref[:, COL_ALR:COL_ALR + LANES]
    xa = _dot(alr.astype(BF16), wau_ref[...]) + bal_ref[...]
    lg_ref[...] = (jnp.minimum(xa, 0.0) - jnp.log1p(jnp.exp(-jnp.abs(xa)))) * (1.0 / GLA_TAU)

    row_i = lax.broadcasted_iota(jnp.int32, (chunk, QK_W), 0)
    tri_r = lax.broadcasted_iota(jnp.int32, (chunk, chunk), 0)
    tri_c = lax.broadcasted_iota(jnp.int32, (chunk, chunk), 1)
    causal = tri_r >= tri_c
    ones_cols = jnp.ones((chunk, LANES), F32)

    def intra_exact(r0):
        intra_ref[...] = jnp.zeros_like(intra_ref)
        k_all = proj_ref[pl.ds(r0, chunk), COL_K:COL_K + QK_W]
        sub_q = lax.broadcasted_iota(jnp.int32, (SUBLANES, QK_W), 0)
        sub_o = lax.broadcasted_iota(jnp.int32, (SUBLANES, V_W), 0)

        def row_body(i, c):
            grp = pl.multiple_of((i // SUBLANES) * SUBLANES, SUBLANES)
            pick = sub_q == (i % SUBLANES)
            cum_i = jnp.sum(jnp.where(pick, cum_ref[pl.ds(grp, SUBLANES), :], 0.0), axis=0, keepdims=True)
            q_rows = proj_ref[pl.ds(pl.multiple_of(r0 + grp, SUBLANES), SUBLANES), COL_Q:COL_Q + QK_W]
            q_i = jnp.sum(jnp.where(pick, q_rows, 0.0), axis=0, keepdims=True) * (GLA_DK ** -0.5)
            seen = row_i <= i
            p = jnp.where(seen, jnp.exp(jnp.where(seen, cum_i - cum_ref[...], 0.0)), 0.0) * k_all * q_i
            outs = []
            for h in range(GLA_HEADS):
                s_h = jnp.sum(p[:, h * GLA_DK:(h + 1) * GLA_DK], axis=-1, keepdims=True)
                v_h = proj_ref[pl.ds(r0, chunk), COL_V + h * GLA_DV:COL_V + (h + 1) * GLA_DV]
                outs.append(jnp.sum(s_h * v_h, axis=0, keepdims=True))
            o_row = jnp.concatenate(outs, axis=1)
            old = intra_ref[pl.ds(grp, SUBLANES), :]
            intra_ref[pl.ds(grp, SUBLANES), :] = jnp.where(sub_o == (i % SUBLANES), o_row, old)
            return c

        lax.fori_loop(0, chunk, row_body, 0)

    def chunk_body(ci, c, *, separable):
        r0 = pl.multiple_of(ci * chunk, chunk)
        sq = ci // cps if spb > 1 else 0
        first = (ci % cps) == 0
        last = (ci % cps) == (cps - 1)
        if carry:
            first = jnp.logical_and(first, j == 0)
            last = jnp.logical_and(last, j == nj - 1)

        @pl.when(first)
        def _():
            s_ref[...] = s0_ref[sq]

        lg = lg_ref[pl.ds(r0, chunk), :]
        cum = lg
        shift = 1
        while shift < chunk:
            cum = cum + jnp.where(row_i >= shift, pltpu.roll(cum, shift, 0), 0.0)
            shift *= 2
        cum_ref[...] = cum
        tot = cum_ref[pl.ds(chunk - 1, 1), :]
        tot_col = jnp.exp(_dot_tn(lg, ones_cols, precision=HIGHEST))
        e_in = jnp.exp(cum)
        e_s = jnp.exp(tot - cum)
        if separable:
            mid = cum_ref[pl.ds(chunk // 2 - 1, 1), :]
            e_q = jnp.exp(cum - mid)
            e_k = jnp.exp(mid - cum)
        else:
            intra_exact(r0)
        for h in range(GLA_HEADS):
            ks = slice(h * GLA_DK, (h + 1) * GLA_DK)
            vs = slice(h * GLA_DV, (h + 1) * GLA_DV)
            q = proj_ref[pl.ds(r0, chunk), COL_Q + h * GLA_DK:COL_Q + (h + 1) * GLA_DK] * (GLA_DK ** -0.5)
            k = proj_ref[pl.ds(r0, chunk), COL_K + h * GLA_DK:COL_K + (h + 1) * GLA_DK]
            v = proj_ref[pl.ds(r0, chunk), COL_V + h * GLA_DV:COL_V + (h + 1) * GLA_DV].astype(BF16)
            gate = proj_ref[pl.ds(r0, chunk), COL_GATE + h * GLA_DV:COL_GATE + (h + 1) * GLA_DV]
            s_old = s_ref[h]
            if separable:
                scores = _dot_nt((q * e_q[:, ks]).astype(BF16), (k * e_k[:, ks]).astype(BF16))
                intra = _dot(jnp.where(causal, scores, 0.0).astype(BF16), v)
            else:
                intra = intra_ref[:, vs]
            o = _dot((q * e_in[:, ks]).astype(BF16), s_old.astype(BF16)) + intra
            s_ref[h] = tot_col[h * GLA_DK:(h + 1) * GLA_DK, :] * s_old + _dot_tn((k * e_s[:, ks]).astype(BF16), v)
            o = o * lax.rsqrt(jnp.mean(o * o, axis=-1, keepdims=True) + EPS)
            o = o * gng_ref[:, vs] * (gate * _sigmoid(gate))
            mix_ref[pl.ds(r0, chunk), vs] = o.astype(BF16)

        @pl.when(last)
        def _():
            sfin_ref[sq] = s_ref[...]

        return c

    in_range = jnp.max(-lg_ref[...]) * (chunk // 2) < GLA_SEPARABLE_MAX_EXPONENT

    @pl.when(in_range)
    def _():
        lax.fori_loop(0, spb * cps, functools.partial(chunk_body, separable=True), 0, unroll=2)

    @pl.when(jnp.logical_not(in_range))
    def _():
        lax.fori_loop(0, spb * cps, functools.partial(chunk_body, separable=False), 0)

    tile = CONV_TILE

    def conv_tile(t0, out_r0):
        wv = ubuf_ref[pl.ds(t0, 2 * tile), :]
        acc = jnp.zeros((tile, CONV_CH), F32)
        for b in range(SUBLANES):
            sb = wv if b == 0 else pltpu.roll(wv, 2 * tile - b, 0)
            for a in range(tile // SUBLANES + 1):
                off = SUBLANES * a + b
                if 2 <= off <= HIST + 2:
                    acc = acc + sb[SUBLANES * a:SUBLANES * a + tile, :] * cw_ref[pl.ds(off - 2, 1), :]
        proj_ref[pl.ds(out_r0, tile), COL_CVAL:COL_CVAL + CONV_CH] = acc + cb_ref[...]

    def conv_norm(r0):
        cv = proj_ref[pl.ds(r0, seg), COL_CVAL:COL_CVAL + CONV_CH]
        mu = jnp.mean(cv, axis=-1, keepdims=True)
        var = jnp.mean(jnp.square(cv - mu), axis=-1, keepdims=True)
        y = (cv - mu) * lax.rsqrt(var + EPS) * lng_ref[...] + lnb_ref[...]
        mix_ref[pl.ds(r0, seg), V_W:V_W + CONV_CH] = (y * _sigmoid(y)).astype(BF16)

    def conv_seg(sq, c):
        r0 = pl.multiple_of(sq * seg, seg) if spb > 1 else 0
        cval = proj_ref[pl.ds(r0, seg), COL_CVAL:COL_CVAL + CONV_CH]
        cgate = proj_ref[pl.ds(r0, seg), COL_CGATE:COL_CGATE + CONV_CH]

        def load_history():
            ubuf_ref[pl.ds(0, 8), :] = jnp.zeros((8, CONV_CH), F32)
            ubuf_ref[pl.ds(2, HIST), :] = c0_ref[sq]

        if carry:
            pl.when(j == 0)(load_history)
        else:
            load_history()
        ubuf_ref[pl.ds(CONV_TILE, seg), :] = cval * _sigmoid(cgate)
        if seg == tile:
            conv_tile(0, r0)
        else:
            def tile_body(t, cc):
                t0 = pl.multiple_of(t * tile, tile)
                conv_tile(t0, r0 + t0)
                return cc
            lax.fori_loop(0, seg // tile, tile_body, 0)
        conv_norm(r0)
        hist = ubuf_ref[pl.ds(seg + 2, HIST), :]
        if carry:
            @pl.when(j == nj - 1)
            def _():
                cfin_ref[sq] = hist
            ubuf_ref[pl.ds(2, HIST), :] = hist
        else:
            cfin_ref[sq] = hist
        return c

    if spb > 1:
        lax.fori_loop(0, spb, conv_seg, 0)
    else:
        conv_seg(0, 0)

    h_ref[...] = _dot(mix_ref[...], wout_ref[...]) + x


def _even_mixer(x2d, hbuf, weights, s0, c0, *, n_rows_total, block0, n_seq, blocks_per_seq, chunk, cps, spb):
    carry = spb == 1
    aliased = hbuf is not None
    const = lambda s, j: (0, 0)
    seq_blk = (lambda s, j: (s, 0, 0, 0)) if carry else (lambda s, j: (0, 0, 0, 0))
    seq_blk3 = (lambda s, j: (s, 0, 0)) if carry else (lambda s, j: (0, 0, 0))
    n_state = 1 if carry else spb
    wspec = lambda shape: pl.BlockSpec(shape, const, pipeline_mode=pl.Buffered(1))
    in_specs = [
        pl.BlockSpec((ROW_BLOCK, D_MODEL), lambda s, j: (s * blocks_per_seq + j, 0)),
        wspec((1, D_MODEL)),
        wspec((D_MODEL, EVEN_COLS)),
        wspec((LANES, QK_W)),
        wspec((1, QK_W)),
        wspec((1, V_W)),
        wspec((CONV_WIDTH, CONV_CH)),
        wspec((1, CONV_CH)),
        wspec((1, CONV_CH)),
        wspec((1, CONV_CH)),
        wspec((V_W + CONV_CH, D_MODEL)),
        pl.BlockSpec((n_state, GLA_HEADS, GLA_DK, GLA_DV), seq_blk),
        pl.BlockSpec((n_state, HIST, CONV_CH), seq_blk3),
    ]
    args = [x2d, *weights, s0, c0]
    aliases = {}
    if aliased:
        in_specs.append(pl.BlockSpec(memory_space=pl.ANY))
        args.append(hbuf)
        aliases = {len(args) - 1: 0}
    n_all = s0.shape[0]
    out_shape = (
        jax.ShapeDtypeStruct((n_rows_total, D_MODEL), F32),
        jax.ShapeDtypeStruct((n_all, GLA_HEADS, GLA_DK, GLA_DV), F32),
        jax.ShapeDtypeStruct((n_all, HIST, CONV_CH), F32),
    )
    out_specs = (
        pl.BlockSpec((ROW_BLOCK, D_MODEL), lambda s, j: (block0 + s * blocks_per_seq + j, 0)),
        pl.BlockSpec((n_state, GLA_HEADS, GLA_DK, GLA_DV), seq_blk),
        pl.BlockSpec((n_state, HIST, CONV_CH), seq_blk3),
    )
    seg = cps * chunk
    scratch = [
        pltpu.VMEM((ROW_BLOCK, EVEN_COLS), F32),
        pltpu.VMEM((ROW_BLOCK, QK_W), F32),
        pltpu.VMEM((chunk, QK_W), F32),
        pltpu.VMEM((chunk, V_W), F32),
        pltpu.VMEM((ROW_BLOCK, V_W + CONV_CH), BF16),
        pltpu.VMEM((GLA_HEADS, GLA_DK, GLA_DV), F32),
        pltpu.VMEM((CONV_TILE + seg, CONV_CH), F32),
    ]
    kern = functools.partial(_even_mixer_kernel, chunk=chunk, cps=cps, spb=spb, carry=carry, aliased=aliased)
    return pl.pallas_call(
        kern, out_shape=out_shape, grid=(n_seq, blocks_per_seq), in_specs=in_specs, out_specs=out_specs,
        scratch_shapes=scratch, input_output_aliases=aliases,
        compiler_params=pltpu.CompilerParams(dimension_semantics=("arbitrary", "arbitrary"), vmem_limit_bytes=VMEM_LIMIT),
        name="even_mixer_carry" if carry else "even_mixer_step",
    )(*args)


def _to_row_tiles(ref, x):
    rows = x.shape[0]
    for c in range(x.shape[1] // LANES):
        ref[pl.ds(c, rows, stride=SUBLANES), :] = x[:, c * LANES:(c + 1) * LANES]


def _from_row_tiles(ref, rows, width=D_MODEL):
    return jnp.concatenate([ref[pl.ds(c, rows, stride=SUBLANES), :] for c in range(width // LANES)], axis=1)


def _route_kernel(h_ref, g_ref, wr_hi_ref, wr_lo_ref, br_ref, xn_ref, mi_ref, mw_ref, cnt_ref, base_ref):
    i = pl.program_id(0)

    @pl.when(i == 0)
    def _():
        base_ref[...] = jnp.zeros_like(base_ref)

    xn = _rmsnorm(h_ref[...], g_ref[...])
    _to_row_tiles(xn_ref, xn)
    x_hi = xn.astype(BF16)
    x_lo = (xn - x_hi.astype(F32)).astype(BF16)
    logits = _dot(x_hi, wr_hi_ref[...]) + (_dot(x_lo, wr_hi_ref[...]) + _dot(x_hi, wr_lo_ref[...])) + br_ref[...]
    lane = lax.broadcasted_iota(jnp.int32, logits.shape, 1).astype(F32)
    far = float(1 << 20)
    gl = jnp.where(lane < N_GROUPS, logits, -jnp.inf)
    gmax = jnp.max(gl, axis=-1, keepdims=True)
    gidx = jnp.min(jnp.where(gl == gmax, lane, far), axis=-1, keepdims=True)
    gw = 1.0 / jnp.sum(jnp.exp(gl - gmax), axis=-1, keepdims=True)
    lo = N_GROUPS + gidx * EXPERTS_PER_GROUP
    el = jnp.where(jnp.logical_and(lane >= lo, lane < lo + EXPERTS_PER_GROUP), logits, -jnp.inf)
    m1 = jnp.max(el, axis=-1, keepdims=True)
    i1 = jnp.min(jnp.where(el == m1, lane, far), axis=-1, keepdims=True)
    el2 = jnp.where(lane == i1, -jnp.inf, el)
    m2 = jnp.max(el2, axis=-1, keepdims=True)
    i2 = jnp.min(jnp.where(el2 == m2, lane, far), axis=-1, keepdims=True)
    e2 = jnp.exp(m2 - m1)
    den = 1.0 + e2
    w0 = (1.0 / den) * gw
    w1 = (e2 / den) * gw
    id0 = i1 - N_GROUPS
    id1 = i2 - N_GROUPS
    oh0 = jnp.where(lane == id0, 1.0, 0.0)
    oh1 = jnp.where(lane == id1, 1.0, 0.0)
    rr = lax.broadcasted_iota(jnp.int32, (ROW_BLOCK, ROW_BLOCK), 0)
    cc = lax.broadcasted_iota(jnp.int32, (ROW_BLOCK, ROW_BLOCK), 1)
    below = jnp.where(rr > cc, 1.0, 0.0).astype(BF16)
    p0 = _dot(below, oh0.astype(BF16))
    p1 = _dot(below, oh1.astype(BF16))
    cnt0 = jnp.sum(oh0, axis=0, keepdims=True)
    cnt1 = jnp.sum(oh1, axis=0, keepdims=True)
    base = base_ref[...]
    rank0 = jnp.sum(oh0 * (p0 + base), axis=-1, keepdims=True)
    rank1 = jnp.sum(oh1 * (p1 + base + cnt0), axis=-1, keepdims=True)
    new_base = base + cnt0 + cnt1
    base_ref[...] = new_base
    cnt_ref[...] = new_base.astype(jnp.int32)
    meta = jnp.where(lane == 0, id0, jnp.where(lane == 1, id1, jnp.where(lane == 2, rank0, jnp.where(lane == 3, rank1, 0.0))))
    mi_ref[0] = meta.T[0:8, :].astype(jnp.int32)
    mw_ref[...] = jnp.where(lane == 0, w0, jnp.where(lane == 1, w1, 0.0))


def _route(h, g, wr, br, n_blocks):
    n_rows = n_blocks * ROW_BLOCK
    wr_hi = wr.astype(BF16)
    wr_hi_residual = (wr - wr_hi.astype(F32)).astype(BF16)
    const = lambda i: (0, 0)
    return pl.pallas_call(
        _route_kernel,
        out_shape=(
            jax.ShapeDtypeStruct((n_rows * SUBLANES, LANES), F32),
            jax.ShapeDtypeStruct((n_blocks, 8, ROW_BLOCK), jnp.int32),
            jax.ShapeDtypeStruct((n_rows, LANES), F32),
            jax.ShapeDtypeStruct((1, LANES), jnp.int32),
        ),
        grid=(n_blocks,),
        in_specs=[
            pl.BlockSpec((ROW_BLOCK, D_MODEL), lambda i: (i, 0)),
            pl.BlockSpec((1, D_MODEL), const),
            pl.BlockSpec((D_MODEL, LANES), const),
            pl.BlockSpec((D_MODEL, LANES), const),
            pl.BlockSpec((1, LANES), const),
        ],
        out_specs=(
            pl.BlockSpec((ROW_BLOCK * SUBLANES, LANES), lambda i: (i, 0)),
            pl.BlockSpec((1, 8, ROW_BLOCK), lambda i: (i, 0, 0)),
            pl.BlockSpec((ROW_BLOCK, LANES), lambda i: (i, 0)),
            pl.BlockSpec((1, LANES), const),
        ),
        scratch_shapes=[pltpu.VMEM((1, LANES), F32)],
        compiler_params=pltpu.CompilerParams(dimension_semantics=("arbitrary",), vmem_limit_bytes=VMEM_LIMIT),
        name="moe_route",
    )(h, g, wr_hi, wr_hi_residual, br)


def _slot_index_kernel(dest_ref, dst0_ref, dst_ref, sem, *, n_blocks, n_rows):
    init = pltpu.make_async_copy(dst0_ref, dst_ref, sem)
    init.start()
    init.wait()

    def block(b, c):
        for k in range(2):
            def body(j, cc):
                d = dest_ref[b * (2 * ROW_BLOCK) + k * ROW_BLOCK + j]
                dst_ref[d] = (k * n_rows + b * ROW_BLOCK + j) * SUBLANES
                return cc

            lax.fori_loop(0, ROW_BLOCK, body, 0, unroll=8)
        return c

    lax.fori_loop(0, n_blocks, block, 0)


def _slot_index(dest_flat, dst0, n_blocks, n_rows):
    return pl.pallas_call(
        functools.partial(_slot_index_kernel, n_blocks=n_blocks, n_rows=n_rows),
        out_shape=jax.ShapeDtypeStruct(dst0.shape, jnp.int32),
        in_specs=[pl.BlockSpec(memory_space=pltpu.SMEM), pl.BlockSpec(memory_space=pl.ANY)],
        out_specs=pl.BlockSpec(memory_space=pltpu.SMEM),
        scratch_shapes=[pltpu.SemaphoreType.DMA(())],
        name="moe_slot_index",
    )(dest_flat, dst0)


GMM_ISSUE_GROUPS = 8
GMM_BUFFERS = 3
GATHER_PRIORITY = 0
SCATTER_PRIORITY = 1


def _gmm_kernel(te_ref, nv_ref, dst_ref, xn_ref, wg_ref, wu_ref, wd_ref, yp_ref,
                xbuf, ybuf, wgb, wub, wdb, gsem, ssem):
    i = pl.program_id(0)
    nv = nv_ref[0]
    n_tiles = pl.num_programs(0) - 1
    slot = i % GMM_BUFFERS
    nxt = (i + 1) % GMM_BUFFERS
    prv = (i + 2) % GMM_BUFFERS

    in_rows = xn_ref.shape[0]

    def gather_copy(tile, r, buf):
        d = dst_ref[tile * GMM_TILE + r]
        d = jnp.where(d >= in_rows, d - in_rows, d)
        src = pl.multiple_of(jnp.where(d >= in_rows, d - in_rows, d), SUBLANES)
        return pltpu.make_async_copy(xn_ref.at[pl.ds(src, SUBLANES), :],
                                     xbuf.at[buf, pl.ds(r * SUBLANES, SUBLANES), :], gsem.at[buf])

    def scatter_copy(tile, r, buf):
        d = pl.multiple_of(dst_ref[tile * GMM_TILE + r], SUBLANES)
        return pltpu.make_async_copy(ybuf.at[buf, pl.ds(r * SUBLANES, SUBLANES), :],
                                     yp_ref.at[pl.ds(d, SUBLANES), :], ssem.at[buf])

    def wait_tile(copy_of_row, buf):
        def body(r, c):
            copy_of_row(0, 0, buf).wait()
            return c

        lax.fori_loop(0, GMM_TILE, body, 0, unroll=16)

    @pl.when(i == 0)
    def _():
        ybuf[...] = jnp.zeros_like(ybuf)

        def body(r, c):
            gather_copy(0, r, 0).start(priority=GATHER_PRIORITY)
            gather_copy(1, r, 1).start(priority=GATHER_PRIORITY)
            return c

        lax.fori_loop(0, GMM_TILE, body, 0, unroll=8)

    @pl.when(jnp.logical_and(i >= 2, i <= nv))
    def _():
        wait_tile(scatter_copy, slot)

    @pl.when(i < nv)
    def _():
        wait_tile(gather_copy, slot)
        prev = te_ref[jnp.maximum(i - 1, 0)]

        @pl.when(jnp.logical_or(i == 0, te_ref[i] != prev))
        def _():
            wgb[...] = wg_ref[0, 0].astype(BF16)
            wub[...] = wu_ref[0, 0].astype(BF16)
            wdb[...] = wd_ref[0, 0].astype(BF16)

        prev_tile = jnp.where(i >= 1, i - 1, n_tiles)
        rows_per_group = GMM_TILE // GMM_ISSUE_GROUPS

        def issue(group):
            for r in range(group * rows_per_group, (group + 1) * rows_per_group):
                gather_copy(i + 2, r, prv).start(priority=GATHER_PRIORITY)
                scatter_copy(prev_tile, r, prv).start(priority=SCATTER_PRIORITY)

        x = _from_row_tiles(xbuf.at[slot], GMM_TILE).astype(BF16)
        half = D_EXPERT // 2
        quarter = D_MODEL // 4
        a0 = _dot(x, wgb[:, 0:half])
        issue(0)
        a1 = _dot(x, wgb[:, half:D_EXPERT])
        issue(1)
        b0 = _dot(x, wub[:, 0:half])
        issue(2)
        b1 = _dot(x, wub[:, half:D_EXPERT])
        issue(3)
        he = jnp.concatenate([(a0 * _sigmoid(a0)) * b0, (a1 * _sigmoid(a1)) * b1], axis=1).astype(BF16)
        for n in range(4):
            yn = _dot(he, wdb[:, n * quarter:(n + 1) * quarter])
            for c in range(quarter // LANES):
                ybuf[slot, pl.ds(n * (quarter // LANES) + c, GMM_TILE, stride=SUBLANES), :] = yn[:, c * LANES:(c + 1) * LANES]
            issue(4 + n)

    @pl.when(i == nv)
    def _():
        wait_tile(gather_copy, slot)
        wait_tile(gather_copy, nxt)
        wait_tile(scatter_copy, nxt)

        def body(r, c):
            scatter_copy(nv - 1, r, prv).start(priority=SCATTER_PRIORITY)
            return c

        lax.fori_loop(0, GMM_TILE, body, 0, unroll=8)
        wait_tile(scatter_copy, prv)


def _gmm(tile_expert, n_valid, dst, xn_tiles, w_gate, w_up, w_down, layer):
    n_rows = xn_tiles.shape[0] // SUBLANES
    n_tiles = tile_expert.shape[0] - 1

    def w_idx(i, te, nv, dst):
        return (layer, te[jnp.maximum(jnp.minimum(i, nv[0] - 1), 0)], 0, 0)

    return pl.pallas_call(
        _gmm_kernel,
        out_shape=jax.ShapeDtypeStruct(((2 * n_rows + GMM_BUFFERS * GMM_TILE) * SUBLANES, LANES), F32),
        grid_spec=pltpu.PrefetchScalarGridSpec(
            num_scalar_prefetch=3,
            grid=(n_tiles + 1,),
            in_specs=[
                pl.BlockSpec(memory_space=pl.ANY),
                pl.BlockSpec((1, 1, D_MODEL, D_EXPERT), w_idx),
                pl.BlockSpec((1, 1, D_MODEL, D_EXPERT), w_idx),
                pl.BlockSpec((1, 1, D_EXPERT, D_MODEL), w_idx),
            ],
            out_specs=pl.BlockSpec(memory_space=pl.ANY),
            scratch_shapes=[
                pltpu.VMEM((GMM_BUFFERS, GMM_TILE * SUBLANES, LANES), F32),
                pltpu.VMEM((GMM_BUFFERS, GMM_TILE * SUBLANES, LANES), F32),
                pltpu.VMEM((D_MODEL, D_EXPERT), BF16),
                pltpu.VMEM((D_MODEL, D_EXPERT), BF16),
                pltpu.VMEM((D_EXPERT, D_MODEL), BF16),
                pltpu.SemaphoreType.DMA((GMM_BUFFERS,)),
                pltpu.SemaphoreType.DMA((GMM_BUFFERS,)),
            ],
        ),
        compiler_params=pltpu.CompilerParams(dimension_semantics=("arbitrary",), vmem_limit_bytes=VMEM_LIMIT),
        name="moe_gmm",
    )(tile_expert, n_valid, dst, xn_tiles, w_gate, w_up, w_down)


def _combine_kernel(*refs, final):
    if final:
        h_ref, mw_ref, y0_ref, y1_ref, gfin_ref, out_ref = refs
    else:
        h_ref, mw_ref, y0_ref, y1_ref, out_ref = refs
    mw = mw_ref[...]
    y0 = _from_row_tiles(y0_ref, ROW_BLOCK)
    y1 = _from_row_tiles(y1_ref, ROW_BLOCK)
    hn = h_ref[...] + (mw[:, 0:1] * y0 + mw[:, 1:2] * y1)
    out_ref[...] = _rmsnorm(hn, gfin_ref[...]) if final else hn


def _combine(h, mw, yp, gfin, *, block0, n_blocks):
    final = gfin is not None
    total_blocks = h.shape[0] // ROW_BLOCK
    in_specs = [
        pl.BlockSpec((ROW_BLOCK, D_MODEL), lambda i: (block0 + i, 0)),
        pl.BlockSpec((ROW_BLOCK, LANES), lambda i: (block0 + i, 0)),
        pl.BlockSpec((ROW_BLOCK * SUBLANES, LANES), lambda i: (block0 + i, 0)),
        pl.BlockSpec((ROW_BLOCK * SUBLANES, LANES), lambda i: (total_blocks + block0 + i, 0)),
    ]
    args = [h, mw, yp, yp]
    if final:
        in_specs.append(pl.BlockSpec((1, D_MODEL), lambda i: (0, 0)))
        args.append(gfin)
    return pl.pallas_call(
        functools.partial(_combine_kernel, final=final),
        out_shape=jax.ShapeDtypeStruct((n_blocks * ROW_BLOCK, D_MODEL), F32),
        grid=(n_blocks,),
        in_specs=in_specs,
        out_specs=pl.BlockSpec((ROW_BLOCK, D_MODEL), lambda i: (i, 0)),
        compiler_params=pltpu.CompilerParams(dimension_semantics=("arbitrary",), vmem_limit_bytes=VMEM_LIMIT),
        name="moe_combine_final" if final else "moe_combine",
    )(*args)


def _moe_tiles(n_rows):
    return (2 * n_rows + N_EXPERTS * (GMM_TILE - 1) + GMM_TILE - 1) // GMM_TILE


def _hier_moe(h, g, w_rg, b_rg, w_re, b_re, w_gate, w_up, w_down, layer, gfin, splits):
    n_rows = h.shape[0]
    n_blocks = n_rows // ROW_BLOCK
    n_re = N_GROUPS * EXPERTS_PER_GROUP
    wr = jnp.zeros((D_MODEL, LANES), F32)
    wr = wr.at[:, :N_GROUPS].set(w_rg).at[:, N_GROUPS:N_GROUPS + n_re].set(w_re.reshape(D_MODEL, n_re))
    br = jnp.zeros((1, LANES), F32)
    br = br.at[0, :N_GROUPS].set(b_rg).at[0, N_GROUPS:N_GROUPS + n_re].set(b_re.reshape(n_re))
    xn_tiles, meta_i, meta_w, counts = _route(h, g.reshape(1, D_MODEL), wr, br, n_blocks)

    n_tiles = _moe_tiles(n_rows)
    cnt = counts[0, :N_EXPERTS]
    padded = ((cnt + GMM_TILE - 1) // GMM_TILE) * GMM_TILE
    ends = jnp.cumsum(padded)
    offs = ends - padded
    tile_start = jnp.arange(n_tiles + 1, dtype=jnp.int32) * GMM_TILE
    tile_expert = jnp.minimum(jnp.sum((tile_start[:, None] >= ends[None, :]).astype(jnp.int32), axis=1), N_EXPERTS - 1)
    n_valid = (ends[-1:] // GMM_TILE).astype(jnp.int32)
    eid = meta_i[:, 0:2, :]
    seg_start = jnp.sum(jnp.where(eid[..., None] == jnp.arange(N_EXPERTS, dtype=jnp.int32), offs, 0), axis=-1)
    dest = (seg_start + meta_i[:, 2:4, :]).astype(jnp.int32).reshape(-1)
    slots = jnp.arange((n_tiles + GMM_BUFFERS - 1) * GMM_TILE, dtype=jnp.int32)
    scratch_rows = 2 * n_rows + slots % (GMM_BUFFERS * GMM_TILE)
    dst = _slot_index(dest, scratch_rows * SUBLANES, n_blocks, n_rows)

    yp = _gmm(tile_expert.astype(jnp.int32), n_valid, dst, xn_tiles, w_gate, w_up, w_down, layer)
    gf = None if gfin is None else gfin.reshape(1, D_MODEL)
    return [_combine(h, meta_w, yp, gf, block0=b0, n_blocks=nb) for (b0, nb) in splits]


def _qkv_kernel(h_ref, g_ref, w_ref, q_ref, k_ref, v_ref):
    xn = _rmsnorm(h_ref[...], g_ref[...]).astype(BF16)
    width = ATT_HEADS * ATT_DH
    q_ref[...] = (_dot(xn, w_ref[:, 0:width]) * (ATT_DH ** -0.5)).astype(BF16)
    k_ref[...] = _dot(xn, w_ref[:, width:2 * width])
    v_ref[...] = _dot(xn, w_ref[:, 2 * width:3 * width])


def _qkv(h, g, w_bf16, n_blocks):
    n_rows = n_blocks * ROW_BLOCK
    width = ATT_HEADS * ATT_DH
    row_blk = lambda i: (i, 0)
    return pl.pallas_call(
        _qkv_kernel,
        out_shape=(
            jax.ShapeDtypeStruct((n_rows, width), BF16),
            jax.ShapeDtypeStruct((n_rows, width), F32),
            jax.ShapeDtypeStruct((n_rows, width), F32),
        ),
        grid=(n_blocks,),
        in_specs=[
            pl.BlockSpec((ROW_BLOCK, D_MODEL), row_blk),
            pl.BlockSpec((1, D_MODEL), lambda i: (0, 0)),
            pl.BlockSpec((D_MODEL, 3 * width), lambda i: (0, 0), pipeline_mode=pl.Buffered(1)),
        ],
        out_specs=(
            pl.BlockSpec((ROW_BLOCK, width), row_blk),
            pl.BlockSpec((ROW_BLOCK, width), row_blk),
            pl.BlockSpec((ROW_BLOCK, width), row_blk),
        ),
        compiler_params=pltpu.CompilerParams(dimension_semantics=("arbitrary",), vmem_limit_bytes=VMEM_LIMIT),
        name="qkv_proj",
    )(h, g, w_bf16)


def _softmax_parts(parts):
    m = functools.reduce(jnp.maximum, [jnp.max(s, axis=-1, keepdims=True) for s in parts])
    ps = [jnp.exp(s - m) for s in parts]
    inv = 1.0 / functools.reduce(jnp.add, [jnp.sum(p, axis=-1, keepdims=True) for p in ps])
    return ps, inv


def _band_prompt_kernel(h_ref, q_ref, ka_ref, kb_ref, kc_ref, va_ref, vb_ref, vc_ref, bm_ref, wout_ref,
                        out_ref, kw_ref, vw_ref, o_ref):
    i = pl.program_id(1)
    for b, (k_ref, v_ref) in enumerate(((ka_ref, va_ref), (kb_ref, vb_ref), (kc_ref, vc_ref))):
        kw_ref[pl.ds(b * ATT_QB, ATT_QB), :] = k_ref[...].astype(BF16)
        vw_ref[pl.ds(b * ATT_QB, ATT_QB), :] = v_ref[...].astype(BF16)
    col = lax.broadcasted_iota(jnp.int32, (1, ATT_WIN), 1)
    first_valid = (2 - jnp.minimum(i, 2)) * ATT_QB
    col_mask = jnp.where(col >= first_valid, 0.0, NEG_BIG)
    for h in range(ATT_HEADS):
        hs = slice(h * ATT_DH, (h + 1) * ATT_DH)
        s = _dot_nt(q_ref[:, hs], kw_ref[:, hs]) + bm_ref[h] + col_mask
        (p,), inv = _softmax_parts([s])
        o_ref[:, hs] = (_dot(p.astype(BF16), vw_ref[:, hs]) * inv).astype(BF16)
    out_ref[...] = _dot(o_ref[...], wout_ref[...]) + h_ref[...]


def _band_prompt(h, q, k, v, bm, wout, *, n_rows_total, n_seq, blocks_per_seq):
    width = ATT_HEADS * ATT_DH
    blk = lambda s, i: (s * blocks_per_seq + i, 0)
    back = lambda n: (lambda s, i: (s * blocks_per_seq + jnp.maximum(i - n, 0), 0))
    kv_spec = lambda n: pl.BlockSpec((ATT_QB, width), back(n))
    return pl.pallas_call(
        _band_prompt_kernel,
        out_shape=jax.ShapeDtypeStruct((n_rows_total, D_MODEL), F32),
        grid=(n_seq, blocks_per_seq),
        in_specs=[
            pl.BlockSpec((ATT_QB, D_MODEL), blk),
            pl.BlockSpec((ATT_QB, width), blk),
            kv_spec(2), kv_spec(1), kv_spec(0),
            kv_spec(2), kv_spec(1), kv_spec(0),
            pl.BlockSpec((ATT_HEADS, ATT_QB, ATT_WIN), lambda s, i: (0, 0, 0), pipeline_mode=pl.Buffered(1)),
            pl.BlockSpec((width, D_MODEL), lambda s, i: (0, 0), pipeline_mode=pl.Buffered(1)),
        ],
        out_specs=pl.BlockSpec((ATT_QB, D_MODEL), blk),
        scratch_shapes=[
            pltpu.VMEM((ATT_WIN, width), BF16),
            pltpu.VMEM((ATT_WIN, width), BF16),
            pltpu.VMEM((ATT_QB, width), BF16),
        ],
        compiler_params=pltpu.CompilerParams(dimension_semantics=("arbitrary", "arbitrary"), vmem_limit_bytes=VMEM_LIMIT),
        name="band_attn_prompt",
    )(h, q, k, k, k, v, v, v, bm, wout)


def _band_sample_kernel(h_ref, q_ref, kn_ref, vn_ref, ck_ref, cv_ref, b1_ref, b2_ref, wout_ref, hbuf_ref,
                        out_ref, o_ref):
    del hbuf_ref
    for h in range(ATT_HEADS):
        hs = slice(h * ATT_DH, (h + 1) * ATT_DH)
        qh = q_ref[:, hs]
        s1 = _dot(qh, ck_ref[0, h].astype(BF16)) + b1_ref[h]
        s2 = _dot_nt(qh, kn_ref[:, hs].astype(BF16)) + b2_ref[h]
        (p1, p2), inv = _softmax_parts([s1, s2])
        o = _dot_nt(p1.astype(BF16), cv_ref[0, h].astype(BF16)) + _dot(p2.astype(BF16), vn_ref[:, hs].astype(BF16))
        o_ref[:, hs] = (o * inv).astype(BF16)
    out_ref[...] = _dot(o_ref[...], wout_ref[...]) + h_ref[...]


def _band_sample(h, q, k, v, cache_k, cache_v, b1, b2, wout, hbuf, *, row0, n_seq, seq_len):
    width = ATT_HEADS * ATT_DH
    past = cache_k.shape[3]
    blk0 = row0 // seq_len
    blk = lambda s: (blk0 + s, 0)
    const3 = lambda s: (0, 0, 0)
    return pl.pallas_call(
        _band_sample_kernel,
        out_shape=jax.ShapeDtypeStruct(hbuf.shape, F32),
        grid=(n_seq,),
        in_specs=[
            pl.BlockSpec((seq_len, D_MODEL), blk),
            pl.BlockSpec((seq_len, width), blk),
            pl.BlockSpec((seq_len, width), blk),
            pl.BlockSpec((seq_len, width), blk),
            pl.BlockSpec((1, ATT_HEADS, ATT_DH, past), lambda s: (s, 0, 0, 0)),
            pl.BlockSpec((1, ATT_HEADS, ATT_DH, past), lambda s: (s, 0, 0, 0)),
            pl.BlockSpec((ATT_HEADS, seq_len, past), const3),
            pl.BlockSpec((ATT_HEADS, seq_len, seq_len), const3),
            pl.BlockSpec((width, D_MODEL), lambda s: (0, 0)),
            pl.BlockSpec(memory_space=pl.ANY),
        ],
        out_specs=pl.BlockSpec((seq_len, D_MODEL), blk),
        scratch_shapes=[pltpu.VMEM((seq_len, width), BF16)],
        input_output_aliases={9: 0},
        compiler_params=pltpu.CompilerParams(dimension_semantics=("arbitrary",), vmem_limit_bytes=VMEM_LIMIT),
        name="band_attn_sample",
    )(h, q, k, v, cache_k, cache_v, b1, b2, wout, hbuf)


def _band_bias_table(rb):
    n_top = BAND_PAST - REL_CLIP + 1
    n_var = REL_CLIP + CHUNK - 2
    n_bot = ATT_WIN - n_top - n_var
    n = ATT_WIN + ATT_QB
    heads = rb.shape[0]
    top = jnp.broadcast_to(rb[:, n_var + 1:n_var + 2], (heads, n_top))
    by_d = jnp.concatenate([top, rb[:, 1:n_var + 1][:, ::-1], jnp.broadcast_to(rb[:, 0:1], (heads, n_bot)),
                            jnp.broadcast_to(rb[:, n_var + 1:n_var + 2], (heads, ATT_QB))], axis=1)
    shifted = jnp.tile(by_d, (1, ATT_QB))[:, :ATT_QB * (n - 1)].reshape(heads, ATT_QB, n - 1)
    return shifted[:, :, :ATT_WIN]


def kernel(x_prompt, x_sample, state_gla, state_conv, cache_band_k, cache_band_v, norm_mix_g, norm_ffn_g, norm_final_g, w_in_even, w_alpha_up, b_alpha, gla_norm_g, conv_w, conv_b, conv_ln_g, conv_ln_b, w_out_even, w_qkv_odd, rel_bias, w_out_odd, w_router_grp, b_router_grp, w_router_exp, b_router_exp, w_exp_gate, w_exp_up, w_exp_down):
    batch, seq, _ = x_prompt.shape
    dec_batch, dec_seq, _ = x_sample.shape
    n_prompt = batch * seq
    n_sample = dec_batch * dec_seq
    assert seq % ROW_BLOCK == 0 and n_sample == ROW_BLOCK and seq % ATT_QB == 0
    n_rows = n_prompt + n_sample
    prompt_blocks = n_prompt // ROW_BLOCK
    n_blocks = n_rows // ROW_BLOCK
    width = ATT_HEADS * ATT_DH
    past = cache_band_k.shape[2]

    w_in = w_in_even[0]
    c0, c1, c2, c3, c4, c5 = (QK_W, 2 * QK_W, 2 * QK_W + V_W, 2 * QK_W + 2 * V_W,
                              2 * QK_W + 2 * V_W + GLA_LOWRANK, 2 * QK_W + 2 * V_W + GLA_LOWRANK + CONV_CH)
    w_in_p = jnp.concatenate(
        [w_in[:, :c3], w_in[:, c4:], w_in[:, c3:c4], jnp.zeros((D_MODEL, LANES - GLA_LOWRANK), F32)], axis=1).astype(BF16)
    wau_p = jnp.concatenate([w_alpha_up[0], jnp.zeros((LANES - GLA_LOWRANK, QK_W), F32)], axis=0).astype(BF16)
    even_w = (norm_mix_g[0].reshape(1, D_MODEL), w_in_p, wau_p, b_alpha[0].reshape(1, QK_W),
              gla_norm_g[0].reshape(1, V_W), conv_w[0], conv_b[0].reshape(1, CONV_CH),
              conv_ln_g[0].reshape(1, CONV_CH), conv_ln_b[0].reshape(1, CONV_CH), w_out_even[0].astype(BF16))
    h1, gla_p, conv_p = _even_mixer(
        x_prompt.reshape(n_prompt, D_MODEL), None, even_w,
        jnp.zeros((batch, GLA_HEADS, GLA_DK, GLA_DV), F32), jnp.zeros((batch, HIST, CONV_CH), F32),
        n_rows_total=n_rows, block0=0, n_seq=batch, blocks_per_seq=seq // ROW_BLOCK,
        chunk=CHUNK, cps=ROW_BLOCK // CHUNK, spb=1)
    h1, gla_s, conv_s = _even_mixer(
        x_sample.reshape(n_sample, D_MODEL), h1, even_w, state_gla[0], state_conv[0],
        n_rows_total=n_rows, block0=prompt_blocks, n_seq=1, blocks_per_seq=1,
        chunk=dec_seq, cps=1, spb=dec_batch)

    def moe(h, layer, gfin, splits):
        return _hier_moe(h, norm_ffn_g[layer], w_router_grp[layer], b_router_grp[layer], w_router_exp[layer],
                         b_router_exp[layer], w_exp_gate, w_exp_up, w_exp_down, layer, gfin, splits)

    (h2,) = moe(h1, 0, None, [(0, n_blocks)])

    q, k, v = _qkv(h2, norm_mix_g[1].reshape(1, D_MODEL), w_qkv_odd[0].astype(BF16), n_blocks)
    toe = _band_bias_table(rel_bias[0])
    r = jnp.arange(ATT_QB)[:, None]
    jb = jnp.arange(ATT_WIN)[None, :] - CHUNK * (r // CHUNK)
    in_band = jnp.logical_and(jb >= 0, jb < BAND_PAST + CHUNK)
    bm = jnp.where(in_band[None], toe, NEG_BIG)
    wout_odd = w_out_odd[0].astype(BF16)
    h3 = _band_prompt(h2, q, k, v, bm, wout_odd, n_rows_total=n_rows, n_seq=batch, blocks_per_seq=seq // ATT_QB)
    assert past == BAND_PAST and dec_seq <= CHUNK
    b1 = toe[:, :dec_seq, :past]
    b2 = toe[:, :dec_seq, past:past + dec_seq]
    h3 = _band_sample(h2, q, k, v, jnp.transpose(cache_band_k[0], (0, 2, 3, 1)),
                      jnp.transpose(cache_band_v[0], (0, 2, 3, 1)), b1, b2, wout_odd, h3,
                      row0=n_prompt, n_seq=dec_batch, seq_len=dec_seq)

    y_prompt, y_sample = moe(h3, 1, norm_final_g, [(0, prompt_blocks), (prompt_blocks, n_blocks - prompt_blocks)])

    rows = min(BAND_PAST, seq)
    tail = lambda a: jnp.stack([a[b * seq + seq - rows:(b + 1) * seq] for b in range(batch)]).reshape(
        batch, rows, ATT_HEADS, ATT_DH)
    kp, vp = tail(k), tail(v)
    ks = k[n_prompt:].reshape(dec_batch, dec_seq, ATT_HEADS, ATT_DH)
    vs = v[n_prompt:].reshape(dec_batch, dec_seq, ATT_HEADS, ATT_DH)
    return (y_prompt.reshape(batch, seq, D_MODEL), y_sample.reshape(dec_batch, dec_seq, D_MODEL),
            gla_p[None], gla_s[None], conv_p[None], conv_s[None], kp[None], vp[None], ks[None], vs[None])
```

```python
import functools

import jax
import jax.numpy as jnp
from jax import lax
from jax.experimental import pallas as pl
from jax.experimental.pallas import tpu as pltpu

F32 = jnp.float32
BF16 = jnp.bfloat16
HIGHEST = lax.Precision.HIGHEST

D_MODEL = 1024
CHUNK = 64
EPS = 1e-6
GLA_HEADS = 4
GLA_DK = 64
GLA_DV = 128
GLA_LOWRANK = 16
GLA_TAU = 16.0
GLA_SEPARABLE_MAX_EXPONENT = 40.0
CONV_CH = 512
CONV_WIDTH = 31
HIST = CONV_WIDTH - 1
ATT_HEADS = 16
ATT_DH = 64
BAND_CHUNKS_PAST = 8
BAND_PAST = BAND_CHUNKS_PAST * CHUNK
REL_CLIP = 256
N_GROUPS = 4
EXPERTS_PER_GROUP = 8
N_EXPERTS = N_GROUPS * EXPERTS_PER_GROUP
D_EXPERT = 512

LANES = 128
SUBLANES = 8
ROW_BLOCK = 512
CONV_TILE = 32
GMM_TILE = 256
ATT_QB = 256
ATT_WIN = ATT_QB + BAND_PAST
NEG_BIG = -1e30
VMEM_LIMIT = 56 * 1024 * 1024

QK_W = GLA_HEADS * GLA_DK
V_W = GLA_HEADS * GLA_DV
COL_Q = 0
COL_K = COL_Q + QK_W
COL_V = COL_K + QK_W
COL_GATE = COL_V + V_W
COL_CVAL = COL_GATE + V_W
COL_CGATE = COL_CVAL + CONV_CH
COL_ALR = COL_CGATE + CONV_CH
EVEN_COLS = COL_ALR + LANES


def _rmsnorm(x, g):
    return x * lax.rsqrt(jnp.mean(x * x, axis=-1, keepdims=True) + EPS) * g


def _sigmoid(x):
    return 1.0 / (1.0 + jnp.exp(-x))


def _dot(a, b):
    return jnp.dot(a, b, preferred_element_type=F32)


def _dot_nt(a, b):
    return lax.dot_general(a, b, (((1,), (1,)), ((), ())), preferred_element_type=F32)


def _dot_tn(a, b, precision=None):
    return lax.dot_general(a, b, (((0,), (0,)), ((), ())), preferred_element_type=F32, precision=precision)


def _even_mixer_kernel(*refs, chunk, cps, spb, carry, aliased):
    (x_ref, g_ref, win_ref, wau_ref, bal_ref, gng_ref, cw_ref, cb_ref, lng_ref, lnb_ref, wout_ref,
     s0_ref, c0_ref) = refs[:13]
    refs = refs[13 + (1 if aliased else 0):]
    h_ref, sfin_ref, cfin_ref, proj_ref, lg_ref, cum_ref, intra_ref, mix_ref, s_ref, ubuf_ref = refs
    j = pl.program_id(1)
    nj = pl.num_programs(1)
    seg = cps * chunk

    x = x_ref[...]
    xn = _rmsnorm(x, g_ref[...])
    proj_ref[...] = _dot(xn.astype(BF16), win_ref[...])
    alr = proj_ref[:, COL_ALR:COL_ALR + LANES]
    xa = _dot(alr.astype(BF16), wau_ref[...]) + bal_ref[...]
    lg_ref[...] = (jnp.minimum(xa, 0.0) - jnp.log1p(jnp.exp(-jnp.abs(xa)))) * (1.0 / GLA_TAU)

    row_i = lax.broadcasted_iota(jnp.int32, (chunk, QK_W), 0)
    tri_r = lax.broadcasted_iota(jnp.int32, (chunk, chunk), 0)
    tri_c = lax.broadcasted_iota(jnp.int32, (chunk, chunk), 1)
    causal = tri_r >= tri_c
    ones_cols = jnp.ones((chunk, LANES), F32)

    def intra_exact(r0):
        intra_ref[...] = jnp.zeros_like(intra_ref)
        k_all = proj_ref[pl.ds(r0, chunk), COL_K:COL_K + QK_W]
        sub_q = lax.broadcasted_iota(jnp.int32, (SUBLANES, QK_W), 0)
        sub_o = lax.broadcasted_iota(jnp.int32, (SUBLANES, V_W), 0)

        def row_body(i, c):
            grp = pl.multiple_of((i // SUBLANES) * SUBLANES, SUBLANES)
            pick = sub_q == (i % SUBLANES)
            cum_i = jnp.sum(jnp.where(pick, cum_ref[pl.ds(grp, SUBLANES), :], 0.0), axis=0, keepdims=True)
            q_rows = proj_ref[pl.ds(pl.multiple_of(r0 + grp, SUBLANES), SUBLANES), COL_Q:COL_Q + QK_W]
            q_i = jnp.sum(jnp.where(pick, q_rows, 0.0), axis=0, keepdims=True) * (GLA_DK ** -0.5)
            seen = row_i <= i
            p = jnp.where(seen, jnp.exp(jnp.where(seen, cum_i - cum_ref[...], 0.0)), 0.0) * k_all * q_i
            outs = []
            for h in range(GLA_HEADS):
                s_h = jnp.sum(p[:, h * GLA_DK:(h + 1) * GLA_DK], axis=-1, keepdims=True)
                v_h = proj_ref[pl.ds(r0, chunk), COL_V + h * GLA_DV:COL_V + (h + 1) * GLA_DV]
                outs.append(jnp.sum(s_h * v_h, axis=0, keepdims=True))
            o_row = jnp.concatenate(outs, axis=1)
            old = intra_ref[pl.ds(grp, SUBLANES), :]
            intra_ref[pl.ds(grp, SUBLANES), :] = jnp.where(sub_o == (i % SUBLANES), o_row, old)
            return c

        lax.fori_loop(0, chunk, row_body, 0)

    def chunk_body(ci, c, *, separable):
        r0 = pl.multiple_of(ci * chunk, chunk)
        sq = ci // cps if spb > 1 else 0
        first = (ci % cps) == 0
        last = (ci % cps) == (cps - 1)
        if carry:
            first = jnp.logical_and(first, j == 0)
            last = jnp.logical_and(last, j == nj - 1)

        @pl.when(first)
        def _():
            s_ref[...] = s0_ref[sq]

        lg = lg_ref[pl.ds(r0, chunk), :]
        cum = lg
        shift = 1
        while shift < chunk:
            cum = cum + jnp.where(row_i >= shift, pltpu.roll(cum, shift, 0), 0.0)
            shift *= 2
        cum_ref[...] = cum
        tot = cum_ref[pl.ds(chunk - 1, 1), :]
        tot_col = jnp.exp(_dot_tn(lg, ones_cols, precision=HIGHEST))
        e_in = jnp.exp(cum)
        e_s = jnp.exp(tot - cum)
        if separable:
            mid = cum_ref[pl.ds(chunk // 2 - 1, 1), :]
            e_q = jnp.exp(cum - mid)
            e_k = jnp.exp(mid - cum)
        else:
            intra_exact(r0)
        for h in range(GLA_HEADS):
            ks = slice(h * GLA_DK, (h + 1) * GLA_DK)
            vs = slice(h * GLA_DV, (h + 1) * GLA_DV)
            q = proj_ref[pl.ds(r0, chunk), COL_Q + h * GLA_DK:COL_Q + (h + 1) * GLA_DK] * (GLA_DK ** -0.5)
            k = proj_ref[pl.ds(r0, chunk), COL_K + h * GLA_DK:COL_K + (h + 1) * GLA_DK]
            v = proj_ref[pl.ds(r0, chunk), COL_V + h * GLA_DV:COL_V + (h + 1) * GLA_DV].astype(BF16)
            gate = proj_ref[pl.ds(r0, chunk), COL_GATE + h * GLA_DV:COL_GATE + (h + 1) * GLA_DV]
            s_old = s_ref[h]
            if separable:
                scores = _dot_nt((q * e_q[:, ks]).astype(BF16), (k * e_k[:, ks]).astype(BF16))
                intra = _dot(jnp.where(causal, scores, 0.0).astype(BF16), v)
            else:
                intra = intra_ref[:, vs]
            o = _dot((q * e_in[:, ks]).astype(BF16), s_old.astype(BF16)) + intra
            s_ref[h] = tot_col[h * GLA_DK:(h + 1) * GLA_DK, :] * s_old + _dot_tn((k * e_s[:, ks]).astype(BF16), v)
            o = o * lax.rsqrt(jnp.mean(o * o, axis=-1, keepdims=True) + EPS)
            o = o * gng_ref[:, vs] * (gate * _sigmoid(gate))
            mix_ref[pl.ds(r0, chunk), vs] = o.astype(BF16)

        @pl.when(last)
        def _():
            sfin_ref[sq] = s_ref[...]

        return c

    in_range = jnp.max(-lg_ref[...]) * (chunk // 2) < GLA_SEPARABLE_MAX_EXPONENT

    @pl.when(in_range)
    def _():
        lax.fori_loop(0, spb * cps, functools.partial(chunk_body, separable=True), 0, unroll=2)

    @pl.when(jnp.logical_not(in_range))
    def _():
        lax.fori_loop(0, spb * cps, functools.partial(chunk_body, separable=False), 0)

    tile = CONV_TILE

    def conv_tile(t0, out_r0):
        wv = ubuf_ref[pl.ds(t0, 2 * tile), :]
        acc = jnp.zeros((tile, CONV_CH), F32)
        for b in range(SUBLANES):
            sb = wv if b == 0 else pltpu.roll(wv, 2 * tile - b, 0)
            for a in range(tile // SUBLANES + 1):
                off = SUBLANES * a + b
                if 2 <= off <= HIST + 2:
                    acc = acc + sb[SUBLANES * a:SUBLANES * a + tile, :] * cw_ref[pl.ds(off - 2, 1), :]
        proj_ref[pl.ds(out_r0, tile), COL_CVAL:COL_CVAL + CONV_CH] = acc + cb_ref[...]

    def conv_norm(r0):
        cv = proj_ref[pl.ds(r0, seg), COL_CVAL:COL_CVAL + CONV_CH]
        mu = jnp.mean(cv, axis=-1, keepdims=True)
        var = jnp.mean(jnp.square(cv - mu), axis=-1, keepdims=True)
        y = (cv - mu) * lax.rsqrt(var + EPS) * lng_ref[...] + lnb_ref[...]
        mix_ref[pl.ds(r0, seg), V_W:V_W + CONV_CH] = (y * _sigmoid(y)).astype(BF16)

    def conv_seg(sq, c):
        r0 = pl.multiple_of(sq * seg, seg) if spb > 1 else 0
        cval = proj_ref[pl.ds(r0, seg), COL_CVAL:COL_CVAL + CONV_CH]
        cgate = proj_ref[pl.ds(r0, seg), COL_CGATE:COL_CGATE + CONV_CH]

        def load_history():
            ubuf_ref[pl.ds(0, 8), :] = jnp.zeros((8, CONV_CH), F32)
            ubuf_ref[pl.ds(2, HIST), :] = c0_ref[sq]

        if carry:
            pl.when(j == 0)(load_history)
        else:
            load_history()
        ubuf_ref[pl.ds(CONV_TILE, seg), :] = cval * _sigmoid(cgate)
        if seg == tile:
            conv_tile(0, r0)
        else:
            def tile_body(t, cc):
                t0 = pl.multiple_of(t * tile, tile)
                conv_tile(t0, r0 + t0)
                return cc
            lax.fori_loop(0, seg // tile, tile_body, 0)
        conv_norm(r0)
        hist = ubuf_ref[pl.ds(seg + 2, HIST), :]
        if carry:
            @pl.when(j == nj - 1)
            def _():
                cfin_ref[sq] = hist
            ubuf_ref[pl.ds(2, HIST), :] = hist
        else:
            cfin_ref[sq] = hist
        return c

    if spb > 1:
        lax.fori_loop(0, spb, conv_seg, 0)
    else:
        conv_seg(0, 0)

    h_ref[...] = _dot(mix_ref[...], wout_ref[...]) + x


def _even_mixer(x2d, hbuf, weights, s0, c0, *, n_rows_total, block0, n_seq, blocks_per_seq, chunk, cps, spb):
    carry = spb == 1
    aliased = hbuf is not None
    const = lambda s, j: (0, 0)
    seq_blk = (lambda s, j: (s, 0, 0, 0)) if carry else (lambda s, j: (0, 0, 0, 0))
    seq_blk3 = (lambda s, j: (s, 0, 0)) if carry else (lambda s, j: (0, 0, 0))
    n_state = 1 if carry else spb
    wspec = lambda shape: pl.BlockSpec(shape, const, pipeline_mode=pl.Buffered(1))
    in_specs = [
        pl.BlockSpec((ROW_BLOCK, D_MODEL), lambda s, j: (s * blocks_per_seq + j, 0)),
        wspec((1, D_MODEL)),
        wspec((D_MODEL, EVEN_COLS)),
        wspec((LANES, QK_W)),
        wspec((1, QK_W)),
        wspec((1, V_W)),
        wspec((CONV_WIDTH, CONV_CH)),
        wspec((1, CONV_CH)),
        wspec((1, CONV_CH)),
        wspec((1, CONV_CH)),
        wspec((V_W + CONV_CH, D_MODEL)),
        pl.BlockSpec((n_state, GLA_HEADS, GLA_DK, GLA_DV), seq_blk),
        pl.BlockSpec((n_state, HIST, CONV_CH), seq_blk3),
    ]
    args = [x2d, *weights, s0, c0]
    aliases = {}
    if aliased:
        in_specs.append(pl.BlockSpec(memory_space=pl.ANY))
        args.append(hbuf)
        aliases = {len(args) - 1: 0}
    n_all = s0.shape[0]
    out_shape = (
        jax.ShapeDtypeStruct((n_rows_total, D_MODEL), F32),
        jax.ShapeDtypeStruct((n_all, GLA_HEADS, GLA_DK, GLA_DV), F32),
        jax.ShapeDtypeStruct((n_all, HIST, CONV_CH), F32),
    )
    out_specs = (
        pl.BlockSpec((ROW_BLOCK, D_MODEL), lambda s, j: (block0 + s * blocks_per_seq + j, 0)),
        pl.BlockSpec((n_state, GLA_HEADS, GLA_DK, GLA_DV), seq_blk),
        pl.BlockSpec((n_state, HIST, CONV_CH), seq_blk3),
    )
    seg = cps * chunk
    scratch = [
        pltpu.VMEM((ROW_BLOCK, EVEN_COLS), F32),
        pltpu.VMEM((ROW_BLOCK, QK_W), F32),
        pltpu.VMEM((chunk, QK_W), F32),
        pltpu.VMEM((chunk, V_W), F32),
        pltpu.VMEM((ROW_BLOCK, V_W + CONV_CH), BF16),
        pltpu.VMEM((GLA_HEADS, GLA_DK, GLA_DV), F32),
        pltpu.VMEM((CONV_TILE + seg, CONV_CH), F32),
    ]
    kern = functools.partial(_even_mixer_kernel, chunk=chunk, cps=cps, spb=spb, carry=carry, aliased=aliased)
    return pl.pallas_call(
        kern, out_shape=out_shape, grid=(n_seq, blocks_per_seq), in_specs=in_specs, out_specs=out_specs,
        scratch_shapes=scratch, input_output_aliases=aliases,
        compiler_params=pltpu.CompilerParams(dimension_semantics=("arbitrary", "arbitrary"), vmem_limit_bytes=VMEM_LIMIT),
        name="even_mixer_carry" if carry else "even_mixer_step",
    )(*args)


def _to_row_tiles(ref, x):
    rows = x.shape[0]
    for c in range(x.shape[1] // LANES):
        ref[pl.ds(c, rows, stride=SUBLANES), :] = x[:, c * LANES:(c + 1) * LANES]


def _from_row_tiles(ref, rows, width=D_MODEL):
    return jnp.concatenate([ref[pl.ds(c, rows, stride=SUBLANES), :] for c in range(width // LANES)], axis=1)


def _route_kernel(h_ref, g_ref, wr_hi_ref, wr_lo_ref, br_ref, xn_ref, mi_ref, mw_ref, cnt_ref, base_ref):
    i = pl.program_id(0)

    @pl.when(i == 0)
    def _():
        base_ref[...] = jnp.zeros_like(base_ref)

    xn = _rmsnorm(h_ref[...], g_ref[...])
    _to_row_tiles(xn_ref, xn)
    x_hi = xn.astype(BF16)
    x_lo = (xn - x_hi.astype(F32)).astype(BF16)
    logits = _dot(x_hi, wr_hi_ref[...]) + (_dot(x_lo, wr_hi_ref[...]) + _dot(x_hi, wr_lo_ref[...])) + br_ref[...]
    lane = lax.broadcasted_iota(jnp.int32, logits.shape, 1).astype(F32)
    far = float(1 << 20)
    gl = jnp.where(lane < N_GROUPS, logits, -jnp.inf)
    gmax = jnp.max(gl, axis=-1, keepdims=True)
    gidx = jnp.min(jnp.where(gl == gmax, lane, far), axis=-1, keepdims=True)
    gw = 1.0 / jnp.sum(jnp.exp(gl - gmax), axis=-1, keepdims=True)
    lo = N_GROUPS + gidx * EXPERTS_PER_GROUP
    el = jnp.where(jnp.logical_and(lane >= lo, lane < lo + EXPERTS_PER_GROUP), logits, -jnp.inf)
    m1 = jnp.max(el, axis=-1, keepdims=True)
    i1 = jnp.min(jnp.where(el == m1, lane, far), axis=-1, keepdims=True)
    el2 = jnp.where(lane == i1, -jnp.inf, el)
    m2 = jnp.max(el2, axis=-1, keepdims=True)
    i2 = jnp.min(jnp.where(el2 == m2, lane, far), axis=-1, keepdims=True)
    e2 = jnp.exp(m2 - m1)
    den = 1.0 + e2
    w0 = (1.0 / den) * gw
    w1 = (e2 / den) * gw
    id0 = i1 - N_GROUPS
    id1 = i2 - N_GROUPS
    oh0 = jnp.where(lane == id0, 1.0, 0.0)
    oh1 = jnp.where(lane == id1, 1.0, 0.0)
    rr = lax.broadcasted_iota(jnp.int32, (ROW_BLOCK, ROW_BLOCK), 0)
    cc = lax.broadcasted_iota(jnp.int32, (ROW_BLOCK, ROW_BLOCK), 1)
    below = jnp.where(rr > cc, 1.0, 0.0).astype(BF16)
    p0 = _dot(below, oh0.astype(BF16))
    p1 = _dot(below, oh1.astype(BF16))
    cnt0 = jnp.sum(oh0, axis=0, keepdims=True)
    cnt1 = jnp.sum(oh1, axis=0, keepdims=True)
    base = base_ref[...]
    rank0 = jnp.sum(oh0 * (p0 + base), axis=-1, keepdims=True)
    rank1 = jnp.sum(oh1 * (p1 + base + cnt0), axis=-1, keepdims=True)
    new_base = base + cnt0 + cnt1
    base_ref[...] = new_base
    cnt_ref[...] = new_base.astype(jnp.int32)
    meta = jnp.where(lane == 0, id0, jnp.where(lane == 1, id1, jnp.where(lane == 2, rank0, jnp.where(lane == 3, rank1, 0.0))))
    mi_ref[0] = meta.T[0:8, :].astype(jnp.int32)
    mw_ref[...] = jnp.where(lane == 0, w0, jnp.where(lane == 1, w1, 0.0))


def _route(h, g, wr, br, n_blocks):
    n_rows = n_blocks * ROW_BLOCK
    wr_hi = wr.astype(BF16)
    wr_hi_residual = (wr - wr_hi.astype(F32)).astype(BF16)
    const = lambda i: (0, 0)
    return pl.pallas_call(
        _route_kernel,
        out_shape=(
            jax.ShapeDtypeStruct((n_rows * SUBLANES, LANES), F32),
            jax.ShapeDtypeStruct((n_blocks, 8, ROW_BLOCK), jnp.int32),
            jax.ShapeDtypeStruct((n_rows, LANES), F32),
            jax.ShapeDtypeStruct((1, LANES), jnp.int32),
        ),
        grid=(n_blocks,),
        in_specs=[
            pl.BlockSpec((ROW_BLOCK, D_MODEL), lambda i: (i, 0)),
            pl.BlockSpec((1, D_MODEL), const),
            pl.BlockSpec((D_MODEL, LANES), const),
            pl.BlockSpec((D_MODEL, LANES), const),
            pl.BlockSpec((1, LANES), const),
        ],
        out_specs=(
            pl.BlockSpec((ROW_BLOCK * SUBLANES, LANES), lambda i: (i, 0)),
            pl.BlockSpec((1, 8, ROW_BLOCK), lambda i: (i, 0, 0)),
            pl.BlockSpec((ROW_BLOCK, LANES), lambda i: (i, 0)),
            pl.BlockSpec((1, LANES), const),
        ),
        scratch_shapes=[pltpu.VMEM((1, LANES), F32)],
        compiler_params=pltpu.CompilerParams(dimension_semantics=("arbitrary",), vmem_limit_bytes=VMEM_LIMIT),
        name="moe_route",
    )(h, g, wr_hi, wr_hi_residual, br)


def _slot_index_kernel(dest_ref, dst0_ref, dst_ref, sem, *, n_blocks, n_rows):
    init = pltpu.make_async_copy(dst0_ref, dst_ref, sem)
    init.start()
    init.wait()

    def block(b, c):
        for k in range(2):
            def body(j, cc):
                d = dest_ref[b * (2 * ROW_BLOCK) + k * ROW_BLOCK + j]
                dst_ref[d] = (k * n_rows + b * ROW_BLOCK + j) * SUBLANES
                return cc

            lax.fori_loop(0, ROW_BLOCK, body, 0, unroll=8)
        return c

    lax.fori_loop(0, n_blocks, block, 0)


def _slot_index(dest_flat, dst0, n_blocks, n_rows):
    return pl.pallas_call(
        functools.partial(_slot_index_kernel, n_blocks=n_blocks, n_rows=n_rows),
        out_shape=jax.ShapeDtypeStruct(dst0.shape, jnp.int32),
        in_specs=[pl.BlockSpec(memory_space=pltpu.SMEM), pl.BlockSpec(memory_space=pl.ANY)],
        out_specs=pl.BlockSpec(memory_space=pltpu.SMEM),
        scratch_shapes=[pltpu.SemaphoreType.DMA(())],
        name="moe_slot_index",
    )(dest_flat, dst0)


GMM_ISSUE_GROUPS = 8
GMM_BUFFERS = 3
GATHER_PRIORITY = 0
SCATTER_PRIORITY = 1


def _gmm_kernel(te_ref, nv_ref, dst_ref, xn_ref, wg_ref, wu_ref, wd_ref, yp_ref,
                xbuf, ybuf, wgb, wub, wdb, gsem, ssem):
    i = pl.program_id(0)
    nv = nv_ref[0]
    n_tiles = pl.num_programs(0) - 1
    slot = i % GMM_BUFFERS
    nxt = (i + 1) % GMM_BUFFERS
    prv = (i + 2) % GMM_BUFFERS

    in_rows = xn_ref.shape[0]

    def gather_tile(src, r, buf):
        return pltpu.make_async_copy(xn_ref.at[pl.ds(src, SUBLANES), :],
                                     xbuf.at[buf, pl.ds(r * SUBLANES, SUBLANES), :], gsem.at[buf])

    def scatter_tile(d, r, buf):
        return pltpu.make_async_copy(ybuf.at[buf, pl.ds(r * SUBLANES, SUBLANES), :],
                                     yp_ref.at[pl.ds(d, SUBLANES), :], ssem.at[buf])

    def gather_copy(tile, r, buf):
        d = dst_ref[tile * GMM_TILE + r]
        d = jnp.where(d >= in_rows, d - in_rows, d)
        src = pl.multiple_of(jnp.where(d >= in_rows, d - in_rows, d), SUBLANES)
        return gather_tile(src, r, buf)

    def scatter_copy(tile, r, buf):
        return scatter_tile(pl.multiple_of(dst_ref[tile * GMM_TILE + r], SUBLANES), r, buf)

    def wait_tile(row_tile_copy, buf):
        def body(r, c):
            row_tile_copy(0, 0, buf).wait()
            return c

        lax.fori_loop(0, GMM_TILE, body, 0, unroll=16)

    @pl.when(i == 0)
    def _():
        ybuf[...] = jnp.zeros_like(ybuf)

        def body(r, c):
            gather_copy(0, r, 0).start(priority=GATHER_PRIORITY)
            gather_copy(1, r, 1).start(priority=GATHER_PRIORITY)
            return c

        lax.fori_loop(0, GMM_TILE, body, 0, unroll=8)

    @pl.when(jnp.logical_and(i >= 2, i <= nv))
    def _():
        wait_tile(scatter_tile, slot)

    @pl.when(i < nv)
    def _():
        wait_tile(gather_tile, slot)
        prev = te_ref[jnp.maximum(i - 1, 0)]

        @pl.when(jnp.logical_or(i == 0, te_ref[i] != prev))
        def _():
            wgb[...] = wg_ref[0, 0].astype(BF16)
            wub[...] = wu_ref[0, 0].astype(BF16)
            wdb[...] = wd_ref[0, 0].astype(BF16)

        prev_tile = jnp.where(i >= 1, i - 1, n_tiles)
        rows_per_group = GMM_TILE // GMM_ISSUE_GROUPS

        def issue(group):
            for r in range(group * rows_per_group, (group + 1) * rows_per_group):
                gather_copy(i + 2, r, prv).start(priority=GATHER_PRIORITY)
                scatter_copy(prev_tile, r, prv).start(priority=SCATTER_PRIORITY)

        x = _from_row_tiles(xbuf.at[slot], GMM_TILE).astype(BF16)
        half = D_EXPERT // 2
        quarter = D_MODEL // 4
        a0 = _dot(x, wgb[:, 0:half])
        issue(0)
        a1 = _dot(x, wgb[:, half:D_EXPERT])
        issue(1)
        b0 = _dot(x, wub[:, 0:half])
        issue(2)
        b1 = _dot(x, wub[:, half:D_EXPERT])
        issue(3)
        he = jnp.concatenate([(a0 * _sigmoid(a0)) * b0, (a1 * _sigmoid(a1)) * b1], axis=1).astype(BF16)
        for n in range(4):
            yn = _dot(he, wdb[:, n * quarter:(n + 1) * quarter])
            for c in range(quarter // LANES):
                ybuf[slot, pl.ds(n * (quarter // LANES) + c, GMM_TILE, stride=SUBLANES), :] = yn[:, c * LANES:(c + 1) * LANES]
            issue(4 + n)

    @pl.when(i == nv)
    def _():
        wait_tile(gather_tile, slot)
        wait_tile(gather_tile, nxt)
        wait_tile(scatter_tile, nxt)

        def body(r, c):
            scatter_copy(nv - 1, r, prv).start(priority=SCATTER_PRIORITY)
            return c

        lax.fori_loop(0, GMM_TILE, body, 0, unroll=8)
        wait_tile(scatter_tile, prv)


def _gmm(tile_expert, n_valid, dst, xn_tiles, w_gate, w_up, w_down, layer):
    n_rows = xn_tiles.shape[0] // SUBLANES
    n_tiles = tile_expert.shape[0] - 1

    def w_idx(i, te, nv, dst):
        return (layer, te[jnp.maximum(jnp.minimum(i, nv[0] - 1), 0)], 0, 0)

    return pl.pallas_call(
        _gmm_kernel,
        out_shape=jax.ShapeDtypeStruct(((2 * n_rows + GMM_BUFFERS * GMM_TILE) * SUBLANES, LANES), F32),
        grid_spec=pltpu.PrefetchScalarGridSpec(
            num_scalar_prefetch=3,
            grid=(n_tiles + 1,),
            in_specs=[
                pl.BlockSpec(memory_space=pl.ANY),
                pl.BlockSpec((1, 1, D_MODEL, D_EXPERT), w_idx),
                pl.BlockSpec((1, 1, D_MODEL, D_EXPERT), w_idx),
                pl.BlockSpec((1, 1, D_EXPERT, D_MODEL), w_idx),
            ],
            out_specs=pl.BlockSpec(memory_space=pl.ANY),
            scratch_shapes=[
                pltpu.VMEM((GMM_BUFFERS, GMM_TILE * SUBLANES, LANES), F32),
                pltpu.VMEM((GMM_BUFFERS, GMM_TILE * SUBLANES, LANES), F32),
                pltpu.VMEM((D_MODEL, D_EXPERT), BF16),
                pltpu.VMEM((D_MODEL, D_EXPERT), BF16),
                pltpu.VMEM((D_EXPERT, D_MODEL), BF16),
                pltpu.SemaphoreType.DMA((GMM_BUFFERS,)),
                pltpu.SemaphoreType.DMA((GMM_BUFFERS,)),
            ],
        ),
        compiler_params=pltpu.CompilerParams(dimension_semantics=("arbitrary",), vmem_limit_bytes=VMEM_LIMIT),
        name="moe_gmm",
    )(tile_expert, n_valid, dst, xn_tiles, w_gate, w_up, w_down)


def _combine_kernel(*refs, final):
    if final:
        h_ref, mw_ref, y0_ref, y1_ref, gfin_ref, out_ref = refs
    else:
        h_ref, mw_ref, y0_ref, y1_ref, out_ref = refs
    mw = mw_ref[...]
    y0 = _from_row_tiles(y0_ref, ROW_BLOCK)
    y1 = _from_row_tiles(y1_ref, ROW_BLOCK)
    hn = h_ref[...] + (mw[:, 0:1] * y0 + mw[:, 1:2] * y1)
    out_ref[...] = _rmsnorm(hn, gfin_ref[...]) if final else hn


def _combine(h, mw, yp, gfin, *, block0, n_blocks):
    final = gfin is not None
    total_blocks = h.shape[0] // ROW_BLOCK
    in_specs = [
        pl.BlockSpec((ROW_BLOCK, D_MODEL), lambda i: (block0 + i, 0)),
        pl.BlockSpec((ROW_BLOCK, LANES), lambda i: (block0 + i, 0)),
        pl.BlockSpec((ROW_BLOCK * SUBLANES, LANES), lambda i: (block0 + i, 0)),
        pl.BlockSpec((ROW_BLOCK * SUBLANES, LANES), lambda i: (total_blocks + block0 + i, 0)),
    ]
    args = [h, mw, yp, yp]
    if final:
        in_specs.append(pl.BlockSpec((1, D_MODEL), lambda i: (0, 0)))
        args.append(gfin)
    return pl.pallas_call(
        functools.partial(_combine_kernel, final=final),
        out_shape=jax.ShapeDtypeStruct((n_blocks * ROW_BLOCK, D_MODEL), F32),
        grid=(n_blocks,),
        in_specs=in_specs,
        out_specs=pl.BlockSpec((ROW_BLOCK, D_MODEL), lambda i: (i, 0)),
        compiler_params=pltpu.CompilerParams(dimension_semantics=("arbitrary",), vmem_limit_bytes=VMEM_LIMIT),
        name="moe_combine_final" if final else "moe_combine",
    )(*args)


def _moe_tiles(n_rows):
    return (2 * n_rows + N_EXPERTS * (GMM_TILE - 1) + GMM_TILE - 1) // GMM_TILE


def _hier_moe(h, g, w_rg, b_rg, w_re, b_re, w_gate, w_up, w_down, layer, gfin, splits):
    n_rows = h.shape[0]
    n_blocks = n_rows // ROW_BLOCK
    n_re = N_GROUPS * EXPERTS_PER_GROUP
    wr = jnp.zeros((D_MODEL, LANES), F32)
    wr = wr.at[:, :N_GROUPS].set(w_rg).at[:, N_GROUPS:N_GROUPS + n_re].set(w_re.reshape(D_MODEL, n_re))
    br = jnp.zeros((1, LANES), F32)
    br = br.at[0, :N_GROUPS].set(b_rg).at[0, N_GROUPS:N_GROUPS + n_re].set(b_re.reshape(n_re))
    xn_tiles, meta_i, meta_w, counts = _route(h, g.reshape(1, D_MODEL), wr, br, n_blocks)

    n_tiles = _moe_tiles(n_rows)
    cnt = counts[0, :N_EXPERTS]
    padded = ((cnt + GMM_TILE - 1) // GMM_TILE) * GMM_TILE
    ends = jnp.cumsum(padded)
    offs = ends - padded
    tile_start = jnp.arange(n_tiles + 1, dtype=jnp.int32) * GMM_TILE
    tile_expert = jnp.minimum(jnp.sum((tile_start[:, None] >= ends[None, :]).astype(jnp.int32), axis=1), N_EXPERTS - 1)
    n_valid = (ends[-1:] // GMM_TILE).astype(jnp.int32)
    eid = meta_i[:, 0:2, :]
    seg_start = jnp.sum(jnp.where(eid[..., None] == jnp.arange(N_EXPERTS, dtype=jnp.int32), offs, 0), axis=-1)
    dest = (seg_start + meta_i[:, 2:4, :]).astype(jnp.int32).reshape(-1)
    slots = jnp.arange((n_tiles + GMM_BUFFERS - 1) * GMM_TILE, dtype=jnp.int32)
    scratch_rows = 2 * n_rows + slots % (GMM_BUFFERS * GMM_TILE)
    dst = _slot_index(dest, scratch_rows * SUBLANES, n_blocks, n_rows)

    yp = _gmm(tile_expert.astype(jnp.int32), n_valid, dst, xn_tiles, w_gate, w_up, w_down, layer)
    gf = None if gfin is None else gfin.reshape(1, D_MODEL)
    return [_combine(h, meta_w, yp, gf, block0=b0, n_blocks=nb) for (b0, nb) in splits]


def _qkv_kernel(h_ref, g_ref, w_ref, q_ref, k_ref, v_ref):
    xn = _rmsnorm(h_ref[...], g_ref[...]).astype(BF16)
    width = ATT_HEADS * ATT_DH
    q_ref[...] = (_dot(xn, w_ref[:, 0:width]) * (ATT_DH ** -0.5)).astype(BF16)
    k_ref[...] = _dot(xn, w_ref[:, width:2 * width])
    v_ref[...] = _dot(xn, w_ref[:, 2 * width:3 * width])


def _qkv(h, g, w_bf16, n_blocks):
    n_rows = n_blocks * ROW_BLOCK
    width = ATT_HEADS * ATT_DH
    row_blk = lambda i: (i, 0)
    return pl.pallas_call(
        _qkv_kernel,
        out_shape=(
            jax.ShapeDtypeStruct((n_rows, width), BF16),
            jax.ShapeDtypeStruct((n_rows, width), F32),
            jax.ShapeDtypeStruct((n_rows, width), F32),
        ),
        grid=(n_blocks,),
        in_specs=[
            pl.BlockSpec((ROW_BLOCK, D_MODEL), row_blk),
            pl.BlockSpec((1, D_MODEL), lambda i: (0, 0)),
            pl.BlockSpec((D_MODEL, 3 * width), lambda i: (0, 0), pipeline_mode=pl.Buffered(1)),
        ],
        out_specs=(
            pl.BlockSpec((ROW_BLOCK, width), row_blk),
            pl.BlockSpec((ROW_BLOCK, width), row_blk),
            pl.BlockSpec((ROW_BLOCK, width), row_blk),
        ),
        compiler_params=pltpu.CompilerParams(dimension_semantics=("arbitrary",), vmem_limit_bytes=VMEM_LIMIT),
        name="qkv_proj",
    )(h, g, w_bf16)


def _softmax_parts(parts):
    m = functools.reduce(jnp.maximum, [jnp.max(s, axis=-1, keepdims=True) for s in parts])
    ps = [jnp.exp(s - m) for s in parts]
    inv = 1.0 / functools.reduce(jnp.add, [jnp.sum(p, axis=-1, keepdims=True) for p in ps])
    return ps, inv


def _band_prompt_kernel(h_ref, q_ref, ka_ref, kb_ref, kc_ref, va_ref, vb_ref, vc_ref, bm_ref, wout_ref,
                        out_ref, kw_ref, vw_ref, o_ref):
    i = pl.program_id(1)
    for b, (k_ref, v_ref) in enumerate(((ka_ref, va_ref), (kb_ref, vb_ref), (kc_ref, vc_ref))):
        kw_ref[pl.ds(b * ATT_QB, ATT_QB), :] = k_ref[...].astype(BF16)
        vw_ref[pl.ds(b * ATT_QB, ATT_QB), :] = v_ref[...].astype(BF16)
    col = lax.broadcasted_iota(jnp.int32, (1, ATT_WIN), 1)
    first_valid = (2 - jnp.minimum(i, 2)) * ATT_QB
    col_mask = jnp.where(col >= first_valid, 0.0, NEG_BIG)
    for h in range(ATT_HEADS):
        hs = slice(h * ATT_DH, (h + 1) * ATT_DH)
        s = _dot_nt(q_ref[:, hs], kw_ref[:, hs]) + bm_ref[h] + col_mask
        (p,), inv = _softmax_parts([s])
        o_ref[:, hs] = (_dot(p.astype(BF16), vw_ref[:, hs]) * inv).astype(BF16)
    out_ref[...] = _dot(o_ref[...], wout_ref[...]) + h_ref[...]


def _band_prompt(h, q, k, v, bm, wout, *, n_rows_total, n_seq, blocks_per_seq):
    width = ATT_HEADS * ATT_DH
    blk = lambda s, i: (s * blocks_per_seq + i, 0)
    back = lambda n: (lambda s, i: (s * blocks_per_seq + jnp.maximum(i - n, 0), 0))
    kv_spec = lambda n: pl.BlockSpec((ATT_QB, width), back(n))
    return pl.pallas_call(
        _band_prompt_kernel,
        out_shape=jax.ShapeDtypeStruct((n_rows_total, D_MODEL), F32),
        grid=(n_seq, blocks_per_seq),
        in_specs=[
            pl.BlockSpec((ATT_QB, D_MODEL), blk),
            pl.BlockSpec((ATT_QB, width), blk),
            kv_spec(2), kv_spec(1), kv_spec(0),
            kv_spec(2), kv_spec(1), kv_spec(0),
            pl.BlockSpec((ATT_HEADS, ATT_QB, ATT_WIN), lambda s, i: (0, 0, 0), pipeline_mode=pl.Buffered(1)),
            pl.BlockSpec((width, D_MODEL), lambda s, i: (0, 0), pipeline_mode=pl.Buffered(1)),
        ],
        out_specs=pl.BlockSpec((ATT_QB, D_MODEL), blk),
        scratch_shapes=[
            pltpu.VMEM((ATT_WIN, width), BF16),
            pltpu.VMEM((ATT_WIN, width), BF16),
            pltpu.VMEM((ATT_QB, width), BF16),
        ],
        compiler_params=pltpu.CompilerParams(dimension_semantics=("arbitrary", "arbitrary"), vmem_limit_bytes=VMEM_LIMIT),
        name="band_attn_prompt",
    )(h, q, k, k, k, v, v, v, bm, wout)


def _band_sample_kernel(h_ref, q_ref, kn_ref, vn_ref, ck_ref, cv_ref, b1_ref, b2_ref, wout_ref, hbuf_ref,
                        out_ref, o_ref):
    del hbuf_ref
    for h in range(ATT_HEADS):
        hs = slice(h * ATT_DH, (h + 1) * ATT_DH)
        qh = q_ref[:, hs]
        s1 = _dot(qh, ck_ref[0, h].astype(BF16)) + b1_ref[h]
        s2 = _dot_nt(qh, kn_ref[:, hs].astype(BF16)) + b2_ref[h]
        (p1, p2), inv = _softmax_parts([s1, s2])
        o = _dot_nt(p1.astype(BF16), cv_ref[0, h].astype(BF16)) + _dot(p2.astype(BF16), vn_ref[:, hs].astype(BF16))
        o_ref[:, hs] = (o * inv).astype(BF16)
    out_ref[...] = _dot(o_ref[...], wout_ref[...]) + h_ref[...]


def _band_sample(h, q, k, v, cache_k, cache_v, b1, b2, wout, hbuf, *, row0, n_seq, seq_len):
    width = ATT_HEADS * ATT_DH
    past = cache_k.shape[3]
    blk0 = row0 // seq_len
    blk = lambda s: (blk0 + s, 0)
    const3 = lambda s: (0, 0, 0)
    return pl.pallas_call(
        _band_sample_kernel,
        out_shape=jax.ShapeDtypeStruct(hbuf.shape, F32),
        grid=(n_seq,),
        in_specs=[
            pl.BlockSpec((seq_len, D_MODEL), blk),
            pl.BlockSpec((seq_len, width), blk),
            pl.BlockSpec((seq_len, width), blk),
            pl.BlockSpec((seq_len, width), blk),
            pl.BlockSpec((1, ATT_HEADS, ATT_DH, past), lambda s: (s, 0, 0, 0)),
            pl.BlockSpec((1, ATT_HEADS, ATT_DH, past), lambda s: (s, 0, 0, 0)),
            pl.BlockSpec((ATT_HEADS, seq_len, past), const3),
            pl.BlockSpec((ATT_HEADS, seq_len, seq_len), const3),
            pl.BlockSpec((width, D_MODEL), lambda s: (0, 0)),
            pl.BlockSpec(memory_space=pl.ANY),
        ],
        out_specs=pl.BlockSpec((seq_len, D_MODEL), blk),
        scratch_shapes=[pltpu.VMEM((seq_len, width), BF16)],
        input_output_aliases={9: 0},
        compiler_params=pltpu.CompilerParams(dimension_semantics=("arbitrary",), vmem_limit_bytes=VMEM_LIMIT),
        name="band_attn_sample",
    )(h, q, k, v, cache_k, cache_v, b1, b2, wout, hbuf)


def _band_bias_table(rb):
    n_top = BAND_PAST - REL_CLIP + 1
    n_var = REL_CLIP + CHUNK - 2
    n_bot = ATT_WIN - n_top - n_var
    n = ATT_WIN + ATT_QB
    heads = rb.shape[0]
    top = jnp.broadcast_to(rb[:, n_var + 1:n_var + 2], (heads, n_top))
    by_d = jnp.concatenate([top, rb[:, 1:n_var + 1][:, ::-1], jnp.broadcast_to(rb[:, 0:1], (heads, n_bot)),
                            jnp.broadcast_to(rb[:, n_var + 1:n_var + 2], (heads, ATT_QB))], axis=1)
    shifted = jnp.tile(by_d, (1, ATT_QB))[:, :ATT_QB * (n - 1)].reshape(heads, ATT_QB, n - 1)
    return shifted[:, :, :ATT_WIN]


def kernel(x_prompt, x_sample, state_gla, state_conv, cache_band_k, cache_band_v, norm_mix_g, norm_ffn_g, norm_final_g, w_in_even, w_alpha_up, b_alpha, gla_norm_g, conv_w, conv_b, conv_ln_g, conv_ln_b, w_out_even, w_qkv_odd, rel_bias, w_out_odd, w_router_grp, b_router_grp, w_router_exp, b_router_exp, w_exp_gate, w_exp_up, w_exp_down):
    batch, seq, _ = x_prompt.shape
    dec_batch, dec_seq, _ = x_sample.shape
    n_prompt = batch * seq
    n_sample = dec_batch * dec_seq
    assert seq % ROW_BLOCK == 0 and n_sample == ROW_BLOCK and seq % ATT_QB == 0
    n_rows = n_prompt + n_sample
    prompt_blocks = n_prompt // ROW_BLOCK
    n_blocks = n_rows // ROW_BLOCK
    width = ATT_HEADS * ATT_DH
    past = cache_band_k.shape[2]

    w_in = w_in_even[0]
    c0, c1, c2, c3, c4, c5 = (QK_W, 2 * QK_W, 2 * QK_W + V_W, 2 * QK_W + 2 * V_W,
                              2 * QK_W + 2 * V_W + GLA_LOWRANK, 2 * QK_W + 2 * V_W + GLA_LOWRANK + CONV_CH)
    w_in_p = jnp.concatenate(
        [w_in[:, :c3], w_in[:, c4:], w_in[:, c3:c4], jnp.zeros((D_MODEL, LANES - GLA_LOWRANK), F32)], axis=1).astype(BF16)
    wau_p = jnp.concatenate([w_alpha_up[0], jnp.zeros((LANES - GLA_LOWRANK, QK_W), F32)], axis=0).astype(BF16)
    even_w = (norm_mix_g[0].reshape(1, D_MODEL), w_in_p, wau_p, b_alpha[0].reshape(1, QK_W),
              gla_norm_g[0].reshape(1, V_W), conv_w[0], conv_b[0].reshape(1, CONV_CH),
              conv_ln_g[0].reshape(1, CONV_CH), conv_ln_b[0].reshape(1, CONV_CH), w_out_even[0].astype(BF16))
    h1, gla_p, conv_p = _even_mixer(
        x_prompt.reshape(n_prompt, D_MODEL), None, even_w,
        jnp.zeros((batch, GLA_HEADS, GLA_DK, GLA_DV), F32), jnp.zeros((batch, HIST, CONV_CH), F32),
        n_rows_total=n_rows, block0=0, n_seq=batch, blocks_per_seq=seq // ROW_BLOCK,
        chunk=CHUNK, cps=ROW_BLOCK // CHUNK, spb=1)
    h1, gla_s, conv_s = _even_mixer(
        x_sample.reshape(n_sample, D_MODEL), h1, even_w, state_gla[0], state_conv[0],
        n_rows_total=n_rows, block0=prompt_blocks, n_seq=1, blocks_per_seq=1,
        chunk=dec_seq, cps=1, spb=dec_batch)

    def moe(h, layer, gfin, splits):
        return _hier_moe(h, norm_ffn_g[layer], w_router_grp[layer], b_router_grp[layer], w_router_exp[layer],
                         b_router_exp[layer], w_exp_gate, w_exp_up, w_exp_down, layer, gfin, splits)

    (h2,) = moe(h1, 0, None, [(0, n_blocks)])

    q, k, v = _qkv(h2, norm_mix_g[1].reshape(1, D_MODEL), w_qkv_odd[0].astype(BF16), n_blocks)
    toe = _band_bias_table(rel_bias[0])
    r = jnp.arange(ATT_QB)[:, None]
    jb = jnp.arange(ATT_WIN)[None, :] - CHUNK * (r // CHUNK)
    in_band = jnp.logical_and(jb >= 0, jb < BAND_PAST + CHUNK)
    bm = jnp.where(in_band[None], toe, NEG_BIG)
    wout_odd = w_out_odd[0].astype(BF16)
    h3 = _band_prompt(h2, q, k, v, bm, wout_odd, n_rows_total=n_rows, n_seq=batch, blocks_per_seq=seq // ATT_QB)
    assert past == BAND_PAST and dec_seq <= CHUNK
    b1 = toe[:, :dec_seq, :past]
    b2 = toe[:, :dec_seq, past:past + dec_seq]
    h3 = _band_sample(h2, q, k, v, jnp.transpose(cache_band_k[0], (0, 2, 3, 1)),
                      jnp.transpose(cache_band_v[0], (0, 2, 3, 1)), b1, b2, wout_odd, h3,
                      row0=n_prompt, n_seq=dec_batch, seq_len=dec_seq)

    y_prompt, y_sample = moe(h3, 1, norm_final_g, [(0, prompt_blocks), (prompt_blocks, n_blocks - prompt_blocks)])

    rows = min(BAND_PAST, seq)
    tail = lambda a: jnp.stack([a[b * seq + seq - rows:(b + 1) * seq] for b in range(batch)]).reshape(
        batch, rows, ATT_HEADS, ATT_DH)
    kp, vp = tail(k), tail(v)
    ks = k[n_prompt:].reshape(dec_batch, dec_seq, ATT_HEADS, ATT_DH)
    vs = v[n_prompt:].reshape(dec_batch, dec_seq, ATT_HEADS, ATT_DH)
    return (y_prompt.reshape(batch, seq, D_MODEL), y_sample.reshape(dec_batch, dec_seq, D_MODEL),
            gla_p[None], gla_s[None], conv_p[None], conv_s[None], kp[None], vp[None], ks[None], vs[None])
```

```python
import functools

import jax
import jax.numpy as jnp
from jax import lax
from jax.experimental import pallas as pl
from jax.experimental.pallas import tpu as pltpu

F32 = jnp.float32
BF16 = jnp.bfloat16
HIGHEST = lax.Precision.HIGHEST

D_MODEL = 1024
CHUNK = 64
EPS = 1e-6
GLA_HEADS = 4
GLA_DK = 64
GLA_DV = 128
GLA_LOWRANK = 16
GLA_TAU = 16.0
GLA_SEPARABLE_MAX_EXPONENT = 40.0
CONV_CH = 512
CONV_WIDTH = 31
HIST = CONV_WIDTH - 1
ATT_HEADS = 16
ATT_DH = 64
BAND_CHUNKS_PAST = 8
BAND_PAST = BAND_CHUNKS_PAST * CHUNK
REL_CLIP = 256
N_GROUPS = 4
EXPERTS_PER_GROUP = 8
N_EXPERTS = N_GROUPS * EXPERTS_PER_GROUP
D_EXPERT = 512

LANES = 128
SUBLANES = 8
ROW_BLOCK = 512
CONV_TILE = 32
GMM_TILE = 256
ATT_QB = 256
ATT_WIN = ATT_QB + BAND_PAST
NEG_BIG = -1e30
VMEM_LIMIT = 56 * 1024 * 1024

QK_W = GLA_HEADS * GLA_DK
V_W = GLA_HEADS * GLA_DV
COL_Q = 0
COL_K = COL_Q + QK_W
COL_V = COL_K + QK_W
COL_GATE = COL_V + V_W
COL_CVAL = COL_GATE + V_W
COL_CGATE = COL_CVAL + CONV_CH
COL_ALR = COL_CGATE + CONV_CH
EVEN_COLS = COL_ALR + LANES


def _rmsnorm(x, g):
    return x * lax.rsqrt(jnp.mean(x * x, axis=-1, keepdims=True) + EPS) * g


def _sigmoid(x):
    return 1.0 / (1.0 + jnp.exp(-x))


def _dot(a, b):
    return jnp.dot(a, b, preferred_element_type=F32)


def _dot_nt(a, b):
    return lax.dot_general(a, b, (((1,), (1,)), ((), ())), preferred_element_type=F32)


def _dot_tn(a, b, precision=None):
    return lax.dot_general(a, b, (((0,), (0,)), ((), ())), preferred_element_type=F32, precision=precision)


def _even_mixer_kernel(*refs, chunk, cps, spb, carry, aliased):
    (x_ref, g_ref, win_ref, wau_ref, bal_ref, gng_ref, cw_ref, cb_ref, lng_ref, lnb_ref, wout_ref,
     s0_ref, c0_ref) = refs[:13]
    refs = refs[13 + (1 if aliased else 0):]
    h_ref, sfin_ref, cfin_ref, proj_ref, lg_ref, cum_ref, intra_ref, mix_ref, s_ref, ubuf_ref = refs
    j = pl.program_id(1)
    nj = pl.num_programs(1)
    seg = cps * chunk

    x = x_ref[...]
    xn = _rmsnorm(x, g_ref[...])
    proj_ref[...] = _dot(xn.astype(BF16), win_ref[...])
    alr = proj_ref[:, COL_ALR:COL_ALR + LANES]
    xa = _dot(alr.astype(BF16), wau_ref[...]) + bal_ref[...]
    lg_ref[...] = (jnp.minimum(xa, 0.0) - jnp.log1p(jnp.exp(-jnp.abs(xa)))) * (1.0 / GLA_TAU)

    row_i = lax.broadcasted_iota(jnp.int32, (chunk, QK_W), 0)
    tri_r = lax.broadcasted_iota(jnp.int32, (chunk, chunk), 0)
    tri_c = lax.broadcasted_iota(jnp.int32, (chunk, chunk), 1)
    causal = tri_r >= tri_c
    ones_cols = jnp.ones((chunk, LANES), F32)

    def intra_exact(r0):
        intra_ref[...] = jnp.zeros_like(intra_ref)
        k_all = proj_ref[pl.ds(r0, chunk), COL_K:COL_K + QK_W]
        sub_q = lax.broadcasted_iota(jnp.int32, (SUBLANES, QK_W), 0)
        sub_o = lax.broadcasted_iota(jnp.int32, (SUBLANES, V_W), 0)

        def row_body(i, c):
            grp = pl.multiple_of((i // SUBLANES) * SUBLANES, SUBLANES)
            pick = sub_q == (i % SUBLANES)
            cum_i = jnp.sum(jnp.where(pick, cum_ref[pl.ds(grp, SUBLANES), :], 0.0), axis=0, keepdims=True)
            q_rows = proj_ref[pl.ds(pl.multiple_of(r0 + grp, SUBLANES), SUBLANES), COL_Q:COL_Q + QK_W]
            q_i = jnp.sum(jnp.where(pick, q_rows, 0.0), axis=0, keepdims=True) * (GLA_DK ** -0.5)
            seen = row_i <= i
            p = jnp.where(seen, jnp.exp(jnp.where(seen, cum_i - cum_ref[...], 0.0)), 0.0) * k_all * q_i
            outs = []
            for h in range(GLA_HEADS):
                s_h = jnp.sum(p[:, h * GLA_DK:(h + 1) * GLA_DK], axis=-1, keepdims=True)
                v_h = proj_ref[pl.ds(r0, chunk), COL_V + h * GLA_DV:COL_V + (h + 1) * GLA_DV]
                outs.append(jnp.sum(s_h * v_h, axis=0, keepdims=True))
            o_row = jnp.concatenate(outs, axis=1)
            old = intra_ref[pl.ds(grp, SUBLANES), :]
            intra_ref[pl.ds(grp, SUBLANES), :] = jnp.where(sub_o == (i % SUBLANES), o_row, old)
            return c

        lax.fori_loop(0, chunk, row_body, 0)

    def chunk_body(ci, c, *, separable):
        r0 = pl.multiple_of(ci * chunk, chunk)
        sq = ci // cps if spb > 1 else 0
        first = (ci % cps) == 0
        last = (ci % cps) == (cps - 1)
        if carry:
            first = jnp.logical_and(first, j == 0)
            last = jnp.logical_and(last, j == nj - 1)

        @pl.when(first)
        def _():
            s_ref[...] = s0_ref[sq]

        lg = lg_ref[pl.ds(r0, chunk), :]
        cum = lg
        shift = 1
        while shift < chunk:
            cum = cum + jnp.where(row_i >= shift, pltpu.roll(cum, shift, 0), 0.0)
            shift *= 2
        cum_ref[...] = cum
        tot = cum_ref[pl.ds(chunk - 1, 1), :]
        tot_col = jnp.exp(_dot_tn(lg, ones_cols, precision=HIGHEST))
        e_in = jnp.exp(cum)
        e_s = jnp.exp(tot - cum)
        if separable:
            mid = cum_ref[pl.ds(chunk // 2 - 1, 1), :]
            e_q = jnp.exp(cum - mid)
            e_k = jnp.exp(mid - cum)
        else:
            intra_exact(r0)
        for h in range(GLA_HEADS):
            ks = slice(h * GLA_DK, (h + 1) * GLA_DK)
            vs = slice(h * GLA_DV, (h + 1) * GLA_DV)
            q = proj_ref[pl.ds(r0, chunk), COL_Q + h * GLA_DK:COL_Q + (h + 1) * GLA_DK] * (GLA_DK ** -0.5)
            k = proj_ref[pl.ds(r0, chunk), COL_K + h * GLA_DK:COL_K + (h + 1) * GLA_DK]
            v = proj_ref[pl.ds(r0, chunk), COL_V + h * GLA_DV:COL_V + (h + 1) * GLA_DV].astype(BF16)
            gate = proj_ref[pl.ds(r0, chunk), COL_GATE + h * GLA_DV:COL_GATE + (h + 1) * GLA_DV]
            s_old = s_ref[h]
            if separable:
                scores = _dot_nt((q * e_q[:, ks]).astype(BF16), (k * e_k[:, ks]).astype(BF16))
                intra = _dot(jnp.where(causal, scores, 0.0).astype(BF16), v)
            else:
                intra = intra_ref[:, vs]
            o = _dot((q * e_in[:, ks]).astype(BF16), s_old.astype(BF16)) + intra
            s_ref[h] = tot_col[h * GLA_DK:(h + 1) * GLA_DK, :] * s_old + _dot_tn((k * e_s[:, ks]).astype(BF16), v)
            o = o * lax.rsqrt(jnp.mean(o * o, axis=-1, keepdims=True) + EPS)
            o = o * gng_ref[:, vs] * (gate * _sigmoid(gate))
            mix_ref[pl.ds(r0, chunk), vs] = o.astype(BF16)

        @pl.when(last)
        def _():
            sfin_ref[sq] = s_ref[...]

        return c

    in_range = jnp.max(-lg_ref[...]) * (chunk // 2) < GLA_SEPARABLE_MAX_EXPONENT

    @pl.when(in_range)
    def _():
        lax.fori_loop(0, spb * cps, functools.partial(chunk_body, separable=True), 0, unroll=2)

    @pl.when(jnp.logical_not(in_range))
    def _():
        lax.fori_loop(0, spb * cps, functools.partial(chunk_body, separable=False), 0)

    tile = CONV_TILE

    def conv_tile(t0, out_r0):
        wv = ubuf_ref[pl.ds(t0, 2 * tile), :]
        acc = jnp.zeros((tile, CONV_CH), F32)
        for b in range(SUBLANES):
            sb = wv if b == 0 else pltpu.roll(wv, 2 * tile - b, 0)
            for a in range(tile // SUBLANES + 1):
                off = SUBLANES * a + b
                if 2 <= off <= HIST + 2:
                    acc = acc + sb[SUBLANES * a:SUBLANES * a + tile, :] * cw_ref[pl.ds(off - 2, 1), :]
        proj_ref[pl.ds(out_r0, tile), COL_CVAL:COL_CVAL + CONV_CH] = acc + cb_ref[...]

    def conv_norm(r0):
        cv = proj_ref[pl.ds(r0, seg), COL_CVAL:COL_CVAL + CONV_CH]
        mu = jnp.mean(cv, axis=-1, keepdims=True)
        var = jnp.mean(jnp.square(cv - mu), axis=-1, keepdims=True)
        y = (cv - mu) * lax.rsqrt(var + EPS) * lng_ref[...] + lnb_ref[...]
        mix_ref[pl.ds(r0, seg), V_W:V_W + CONV_CH] = (y * _sigmoid(y)).astype(BF16)

    def conv_seg(sq, c):
        r0 = pl.multiple_of(sq * seg, seg) if spb > 1 else 0
        cval = proj_ref[pl.ds(r0, seg), COL_CVAL:COL_CVAL + CONV_CH]
        cgate = proj_ref[pl.ds(r0, seg), COL_CGATE:COL_CGATE + CONV_CH]

        def load_history():
            ubuf_ref[pl.ds(0, 8), :] = jnp.zeros((8, CONV_CH), F32)
            ubuf_ref[pl.ds(2, HIST), :] = c0_ref[sq]

        if carry:
            pl.when(j == 0)(load_history)
        else:
            load_history()
        ubuf_ref[pl.ds(CONV_TILE, seg), :] = cval * _sigmoid(cgate)
        if seg == tile:
            conv_tile(0, r0)
        else:
            def tile_body(t, cc):
                t0 = pl.multiple_of(t * tile, tile)
                conv_tile(t0, r0 + t0)
                return cc
            lax.fori_loop(0, seg // tile, tile_body, 0)
        conv_norm(r0)
        hist = ubuf_ref[pl.ds(seg + 2, HIST), :]
        if carry:
            @pl.when(j == nj - 1)
            def _():
                cfin_ref[sq] = hist
            ubuf_ref[pl.ds(2, HIST), :] = hist
        else:
            cfin_ref[sq] = hist
        return c

    if spb > 1:
        lax.fori_loop(0, spb, conv_seg, 0)
    else:
        conv_seg(0, 0)

    h_ref[...] = _dot(mix_ref[...], wout_ref[...]) + x


def _even_mixer(x2d, hbuf, weights, s0, c0, *, n_rows_total, block0, n_seq, blocks_per_seq, chunk, cps, spb):
    carry = spb == 1
    aliased = hbuf is not None
    const = lambda s, j: (0, 0)
    seq_blk = (lambda s, j: (s, 0, 0, 0)) if carry else (lambda s, j: (0, 0, 0, 0))
    seq_blk3 = (lambda s, j: (s, 0, 0)) if carry else (lambda s, j: (0, 0, 0))
    n_state = 1 if carry else spb
    wspec = lambda shape: pl.BlockSpec(shape, const, pipeline_mode=pl.Buffered(1))
    in_specs = [
        pl.BlockSpec((ROW_BLOCK, D_MODEL), lambda s, j: (s * blocks_per_seq + j, 0)),
        wspec((1, D_MODEL)),
        wspec((D_MODEL, EVEN_COLS)),
        wspec((LANES, QK_W)),
        wspec((1, QK_W)),
        wspec((1, V_W)),
        wspec((CONV_WIDTH, CONV_CH)),
        wspec((1, CONV_CH)),
        wspec((1, CONV_CH)),
        wspec((1, CONV_CH)),
        wspec((V_W + CONV_CH, D_MODEL)),
        pl.BlockSpec((n_state, GLA_HEADS, GLA_DK, GLA_DV), seq_blk),
        pl.BlockSpec((n_state, HIST, CONV_CH), seq_blk3),
    ]
    args = [x2d, *weights, s0, c0]
    aliases = {}
    if aliased:
        in_specs.append(pl.BlockSpec(memory_space=pl.ANY))
        args.append(hbuf)
        aliases = {len(args) - 1: 0}
    n_all = s0.shape[0]
    out_shape = (
        jax.ShapeDtypeStruct((n_rows_total, D_MODEL), F32),
        jax.ShapeDtypeStruct((n_all, GLA_HEADS, GLA_DK, GLA_DV), F32),
        jax.ShapeDtypeStruct((n_all, HIST, CONV_CH), F32),
    )
    out_specs = (
        pl.BlockSpec((ROW_BLOCK, D_MODEL), lambda s, j: (block0 + s * blocks_per_seq + j, 0)),
        pl.BlockSpec((n_state, GLA_HEADS, GLA_DK, GLA_DV), seq_blk),
        pl.BlockSpec((n_state, HIST, CONV_CH), seq_blk3),
    )
    seg = cps * chunk
    scratch = [
        pltpu.VMEM((ROW_BLOCK, EVEN_COLS), F32),
        pltpu.VMEM((ROW_BLOCK, QK_W), F32),
        pltpu.VMEM((chunk, QK_W), F32),
        pltpu.VMEM((chunk, V_W), F32),
        pltpu.VMEM((ROW_BLOCK, V_W + CONV_CH), BF16),
        pltpu.VMEM((GLA_HEADS, GLA_DK, GLA_DV), F32),
        pltpu.VMEM((CONV_TILE + seg, CONV_CH), F32),
    ]
    kern = functools.partial(_even_mixer_kernel, chunk=chunk, cps=cps, spb=spb, carry=carry, aliased=aliased)
    return pl.pallas_call(
        kern, out_shape=out_shape, grid=(n_seq, blocks_per_seq), in_specs=in_specs, out_specs=out_specs,
        scratch_shapes=scratch, input_output_aliases=aliases,
        compiler_params=pltpu.CompilerParams(dimension_semantics=("arbitrary", "arbitrary"), vmem_limit_bytes=VMEM_LIMIT),
        name="even_mixer_carry" if carry else "even_mixer_step",
    )(*args)


def _to_row_tiles(ref, x):
    rows = x.shape[0]
    for c in range(x.shape[1] // LANES):
        ref[pl.ds(c, rows, stride=SUBLANES), :] = x[:, c * LANES:(c + 1) * LANES]


def _from_row_tiles(ref, rows, width=D_MODEL):
    return jnp.concatenate([ref[pl.ds(c, rows, stride=SUBLANES), :] for c in range(width // LANES)], axis=1)


def _route_kernel(h_ref, g_ref, wr_hi_ref, wr_lo_ref, br_ref, xn_ref, mi_ref, mw_ref, cnt_ref, base_ref):
    i = pl.program_id(0)

    @pl.when(i == 0)
    def _():
        base_ref[...] = jnp.zeros_like(base_ref)

    xn = _rmsnorm(h_ref[...], g_ref[...])
    _to_row_tiles(xn_ref, xn)
    x_hi = xn.astype(BF16)
    x_lo = (xn - x_hi.astype(F32)).astype(BF16)
    logits = _dot(x_hi, wr_hi_ref[...]) + (_dot(x_lo, wr_hi_ref[...]) + _dot(x_hi, wr_lo_ref[...])) + br_ref[...]
    lane = lax.broadcasted_iota(jnp.int32, logits.shape, 1).astype(F32)
    far = float(1 << 20)
    gl = jnp.where(lane < N_GROUPS, logits, -jnp.inf)
    gmax = jnp.max(gl, axis=-1, keepdims=True)
    gidx = jnp.min(jnp.where(gl == gmax, lane, far), axis=-1, keepdims=True)
    gw = 1.0 / jnp.sum(jnp.exp(gl - gmax), axis=-1, keepdims=True)
    lo = N_GROUPS + gidx * EXPERTS_PER_GROUP
    el = jnp.where(jnp.logical_and(lane >= lo, lane < lo + EXPERTS_PER_GROUP), logits, -jnp.inf)
    m1 = jnp.max(el, axis=-1, keepdims=True)
    i1 = jnp.min(jnp.where(el == m1, lane, far), axis=-1, keepdims=True)
    el2 = jnp.where(lane == i1, -jnp.inf, el)
    m2 = jnp.max(el2, axis=-1, keepdims=True)
    i2 = jnp.min(jnp.where(el2 == m2, lane, far), axis=-1, keepdims=True)
    e2 = jnp.exp(m2 - m1)
    den = 1.0 + e2
    w0 = (1.0 / den) * gw
    w1 = (e2 / den) * gw
    id0 = i1 - N_GROUPS
    id1 = i2 - N_GROUPS
    oh0 = jnp.where(lane == id0, 1.0, 0.0)
    oh1 = jnp.where(lane == id1, 1.0, 0.0)
    rr = lax.broadcasted_iota(jnp.int32, (ROW_BLOCK, ROW_BLOCK), 0)
    cc = lax.broadcasted_iota(jnp.int32, (ROW_BLOCK, ROW_BLOCK), 1)
    below = jnp.where(rr > cc, 1.0, 0.0).astype(BF16)
    p0 = _dot(below, oh0.astype(BF16))
    p1 = _dot(below, oh1.astype(BF16))
    cnt0 = jnp.sum(oh0, axis=0, keepdims=True)
    cnt1 = jnp.sum(oh1, axis=0, keepdims=True)
    base = base_ref[...]
    rank0 = jnp.sum(oh0 * (p0 + base), axis=-1, keepdims=True)
    rank1 = jnp.sum(oh1 * (p1 + base + cnt0), axis=-1, keepdims=True)
    new_base = base + cnt0 + cnt1
    base_ref[...] = new_base
    cnt_ref[...] = new_base.astype(jnp.int32)
    meta = jnp.where(lane == 0, id0, jnp.where(lane == 1, id1, jnp.where(lane == 2, rank0, jnp.where(lane == 3, rank1, 0.0))))
    mi_ref[0] = meta.T[0:8, :].astype(jnp.int32)
    mw_ref[...] = jnp.where(lane == 0, w0, jnp.where(lane == 1, w1, 0.0))


def _route(h, g, wr, br, n_blocks):
    n_rows = n_blocks * ROW_BLOCK
    wr_hi = wr.astype(BF16)
    wr_hi_residual = (wr - wr_hi.astype(F32)).astype(BF16)
    const = lambda i: (0, 0)
    return pl.pallas_call(
        _route_kernel,
        out_shape=(
            jax.ShapeDtypeStruct((n_rows * SUBLANES, LANES), F32),
            jax.ShapeDtypeStruct((n_blocks, 8, ROW_BLOCK), jnp.int32),
            jax.ShapeDtypeStruct((n_rows, LANES), F32),
            jax.ShapeDtypeStruct((1, LANES), jnp.int32),
        ),
        grid=(n_blocks,),
        in_specs=[
            pl.BlockSpec((ROW_BLOCK, D_MODEL), lambda i: (i, 0)),
            pl.BlockSpec((1, D_MODEL), const),
            pl.BlockSpec((D_MODEL, LANES), const),
            pl.BlockSpec((D_MODEL, LANES), const),
            pl.BlockSpec((1, LANES), const),
        ],
        out_specs=(
            pl.BlockSpec((ROW_BLOCK * SUBLANES, LANES), lambda i: (i, 0)),
            pl.BlockSpec((1, 8, ROW_BLOCK), lambda i: (i, 0, 0)),
            pl.BlockSpec((ROW_BLOCK, LANES), lambda i: (i, 0)),
            pl.BlockSpec((1, LANES), const),
        ),
        scratch_shapes=[pltpu.VMEM((1, LANES), F32)],
        compiler_params=pltpu.CompilerParams(dimension_semantics=("arbitrary",), vmem_limit_bytes=VMEM_LIMIT),
        name="moe_route",
    )(h, g, wr_hi, wr_hi_residual, br)


def _slot_index_kernel(dest_ref, dst0_ref, dst_ref, sem, *, n_blocks, n_rows):
    init = pltpu.make_async_copy(dst0_ref, dst_ref, sem)
    init.start()
    init.wait()

    def block(b, c):
        for k in range(2):
            def body(j, cc):
                d = dest_ref[b * (2 * ROW_BLOCK) + k * ROW_BLOCK + j]
                dst_ref[d] = (k * n_rows + b * ROW_BLOCK + j) * SUBLANES
                return cc

            lax.fori_loop(0, ROW_BLOCK, body, 0, unroll=8)
        return c

    lax.fori_loop(0, n_blocks, block, 0)


def _slot_index(dest_flat, dst0, n_blocks, n_rows):
    return pl.pallas_call(
        functools.partial(_slot_index_kernel, n_blocks=n_blocks, n_rows=n_rows),
        out_shape=jax.ShapeDtypeStruct(dst0.shape, jnp.int32),
        in_specs=[pl.BlockSpec(memory_space=pltpu.SMEM), pl.BlockSpec(memory_space=pl.ANY)],
        out_specs=pl.BlockSpec(memory_space=pltpu.SMEM),
        scratch_shapes=[pltpu.SemaphoreType.DMA(())],
        name="moe_slot_index",
    )(dest_flat, dst0)


GMM_ISSUE_GROUPS = 8
GMM_BUFFERS = 3
GATHER_PRIORITY = 0
SCATTER_PRIORITY = 1
GATHER_SPILL_PERIOD = 4


def _gmm_kernel(te_ref, nv_ref, dst_ref, xn_ref, wg_ref, wu_ref, wd_ref, yp_ref,
                xbuf, ybuf, wgb, wub, wdb, gsem, ssem):
    i = pl.program_id(0)
    nv = nv_ref[0]
    n_tiles = pl.num_programs(0) - 1
    slot = i % GMM_BUFFERS
    nxt = (i + 1) % GMM_BUFFERS
    prv = (i + 2) % GMM_BUFFERS

    in_rows = xn_ref.shape[0]

    def gather_copy(tile, r, buf):
        d = dst_ref[tile * GMM_TILE + r]
        d = jnp.where(d >= in_rows, d - in_rows, d)
        src = pl.multiple_of(jnp.where(d >= in_rows, d - in_rows, d), SUBLANES)
        return pltpu.make_async_copy(xn_ref.at[pl.ds(src, SUBLANES), :],
                                     xbuf.at[buf, pl.ds(r * SUBLANES, SUBLANES), :], gsem.at[buf])

    def scatter_copy(tile, r, buf):
        d = pl.multiple_of(dst_ref[tile * GMM_TILE + r], SUBLANES)
        return pltpu.make_async_copy(ybuf.at[buf, pl.ds(r * SUBLANES, SUBLANES), :],
                                     yp_ref.at[pl.ds(d, SUBLANES), :], ssem.at[buf])

    def wait_tile(copy_of_row, buf):
        def body(r, c):
            copy_of_row(0, 0, buf).wait()
            return c

        lax.fori_loop(0, GMM_TILE, body, 0, unroll=16)

    @pl.when(i == 0)
    def _():
        ybuf[...] = jnp.zeros_like(ybuf)

        def body(r, c):
            gather_copy(0, r, 0).start(priority=GATHER_PRIORITY)
            gather_copy(1, r, 1).start(priority=GATHER_PRIORITY)
            return c

        lax.fori_loop(0, GMM_TILE, body, 0, unroll=8)

    @pl.when(jnp.logical_and(i >= 2, i <= nv))
    def _():
        wait_tile(scatter_copy, slot)

    @pl.when(i < nv)
    def _():
        wait_tile(gather_copy, slot)
        prev = te_ref[jnp.maximum(i - 1, 0)]

        @pl.when(jnp.logical_or(i == 0, te_ref[i] != prev))
        def _():
            wgb[...] = wg_ref[0, 0].astype(BF16)
            wub[...] = wu_ref[0, 0].astype(BF16)
            wdb[...] = wd_ref[0, 0].astype(BF16)

        prev_tile = jnp.where(i >= 1, i - 1, n_tiles)
        rows_per_group = GMM_TILE // GMM_ISSUE_GROUPS

        def issue(group):
            for r in range(group * rows_per_group, (group + 1) * rows_per_group):
                on_write_queue = r % GATHER_SPILL_PERIOD == GATHER_SPILL_PERIOD - 1
                gather_copy(i + 2, r, prv).start(priority=SCATTER_PRIORITY if on_write_queue else GATHER_PRIORITY)
                scatter_copy(prev_tile, r, prv).start(priority=SCATTER_PRIORITY)

        x = _from_row_tiles(xbuf.at[slot], GMM_TILE).astype(BF16)
        half = D_EXPERT // 2
        quarter = D_MODEL // 4
        a0 = _dot(x, wgb[:, 0:half])
        issue(0)
        a1 = _dot(x, wgb[:, half:D_EXPERT])
        issue(1)
        b0 = _dot(x, wub[:, 0:half])
        issue(2)
        b1 = _dot(x, wub[:, half:D_EXPERT])
        issue(3)
        he = jnp.concatenate([(a0 * _sigmoid(a0)) * b0, (a1 * _sigmoid(a1)) * b1], axis=1).astype(BF16)
        for n in range(4):
            yn = _dot(he, wdb[:, n * quarter:(n + 1) * quarter])
            for c in range(quarter // LANES):
                ybuf[slot, pl.ds(n * (quarter // LANES) + c, GMM_TILE, stride=SUBLANES), :] = yn[:, c * LANES:(c + 1) * LANES]
            issue(4 + n)

    @pl.when(i == nv)
    def _():
        wait_tile(gather_copy, slot)
        wait_tile(gather_copy, nxt)
        wait_tile(scatter_copy, nxt)

        def body(r, c):
            scatter_copy(nv - 1, r, prv).start(priority=SCATTER_PRIORITY)
            return c

        lax.fori_loop(0, GMM_TILE, body, 0, unroll=8)
        wait_tile(scatter_copy, prv)


def _gmm(tile_expert, n_valid, dst, xn_tiles, w_gate, w_up, w_down, layer):
    n_rows = xn_tiles.shape[0] // SUBLANES
    n_tiles = tile_expert.shape[0] - 1

    def w_idx(i, te, nv, dst):
        return (layer, te[jnp.maximum(jnp.minimum(i, nv[0] - 1), 0)], 0, 0)

    return pl.pallas_call(
        _gmm_kernel,
        out_shape=jax.ShapeDtypeStruct(((2 * n_rows + GMM_BUFFERS * GMM_TILE) * SUBLANES, LANES), F32),
        grid_spec=pltpu.PrefetchScalarGridSpec(
            num_scalar_prefetch=3,
            grid=(n_tiles + 1,),
            in_specs=[
                pl.BlockSpec(memory_space=pl.ANY),
                pl.BlockSpec((1, 1, D_MODEL, D_EXPERT), w_idx),
                pl.BlockSpec((1, 1, D_MODEL, D_EXPERT), w_idx),
                pl.BlockSpec((1, 1, D_EXPERT, D_MODEL), w_idx),
            ],
            out_specs=pl.BlockSpec(memory_space=pl.ANY),
            scratch_shapes=[
                pltpu.VMEM((GMM_BUFFERS, GMM_TILE * SUBLANES, LANES), F32),
                pltpu.VMEM((GMM_BUFFERS, GMM_TILE * SUBLANES, LANES), F32),
                pltpu.VMEM((D_MODEL, D_EXPERT), BF16),
                pltpu.VMEM((D_MODEL, D_EXPERT), BF16),
                pltpu.VMEM((D_EXPERT, D_MODEL), BF16),
                pltpu.SemaphoreType.DMA((GMM_BUFFERS,)),
                pltpu.SemaphoreType.DMA((GMM_BUFFERS,)),
            ],
        ),
        compiler_params=pltpu.CompilerParams(dimension_semantics=("arbitrary",), vmem_limit_bytes=VMEM_LIMIT),
        name="moe_gmm",
    )(tile_expert, n_valid, dst, xn_tiles, w_gate, w_up, w_down)


def _combine_kernel(*refs, final):
    if final:
        h_ref, mw_ref, y0_ref, y1_ref, gfin_ref, out_ref = refs
    else:
        h_ref, mw_ref, y0_ref, y1_ref, out_ref = refs
    mw = mw_ref[...]
    y0 = _from_row_tiles(y0_ref, ROW_BLOCK)
    y1 = _from_row_tiles(y1_ref, ROW_BLOCK)
    hn = h_ref[...] + (mw[:, 0:1] * y0 + mw[:, 1:2] * y1)
    out_ref[...] = _rmsnorm(hn, gfin_ref[...]) if final else hn


def _combine(h, mw, yp, gfin, *, block0, n_blocks):
    final = gfin is not None
    total_blocks = h.shape[0] // ROW_BLOCK
    in_specs = [
        pl.BlockSpec((ROW_BLOCK, D_MODEL), lambda i: (block0 + i, 0)),
        pl.BlockSpec((ROW_BLOCK, LANES), lambda i: (block0 + i, 0)),
        pl.BlockSpec((ROW_BLOCK * SUBLANES, LANES), lambda i: (block0 + i, 0)),
        pl.BlockSpec((ROW_BLOCK * SUBLANES, LANES), lambda i: (total_blocks + block0 + i, 0)),
    ]
    args = [h, mw, yp, yp]
    if final:
        in_specs.append(pl.BlockSpec((1, D_MODEL), lambda i: (0, 0)))
        args.append(gfin)
    return pl.pallas_call(
        functools.partial(_combine_kernel, final=final),
        out_shape=jax.ShapeDtypeStruct((n_blocks * ROW_BLOCK, D_MODEL), F32),
        grid=(n_blocks,),
        in_specs=in_specs,
        out_specs=pl.BlockSpec((ROW_BLOCK, D_MODEL), lambda i: (i, 0)),
        compiler_params=pltpu.CompilerParams(dimension_semantics=("arbitrary",), vmem_limit_bytes=VMEM_LIMIT),
        name="moe_combine_final" if final else "moe_combine",
    )(*args)


def _moe_tiles(n_rows):
    return (2 * n_rows + N_EXPERTS * (GMM_TILE - 1) + GMM_TILE - 1) // GMM_TILE


def _hier_moe(h, g, w_rg, b_rg, w_re, b_re, w_gate, w_up, w_down, layer, gfin, splits):
    n_rows = h.shape[0]
    n_blocks = n_rows // ROW_BLOCK
    n_re = N_GROUPS * EXPERTS_PER_GROUP
    wr = jnp.zeros((D_MODEL, LANES), F32)
    wr = wr.at[:, :N_GROUPS].set(w_rg).at[:, N_GROUPS:N_GROUPS + n_re].set(w_re.reshape(D_MODEL, n_re))
    br = jnp.zeros((1, LANES), F32)
    br = br.at[0, :N_GROUPS].set(b_rg).at[0, N_GROUPS:N_GROUPS + n_re].set(b_re.reshape(n_re))
    xn_tiles, meta_i, meta_w, counts = _route(h, g.reshape(1, D_MODEL), wr, br, n_blocks)

    n_tiles = _moe_tiles(n_rows)
    cnt = counts[0, :N_EXPERTS]
    padded = ((cnt + GMM_TILE - 1) // GMM_TILE) * GMM_TILE
    ends = jnp.cumsum(padded)
    offs = ends - padded
    tile_start = jnp.arange(n_tiles + 1, dtype=jnp.int32) * GMM_TILE
    tile_expert = jnp.minimum(jnp.sum((tile_start[:, None] >= ends[None, :]).astype(jnp.int32), axis=1), N_EXPERTS - 1)
    n_valid = (ends[-1:] // GMM_TILE).astype(jnp.int32)
    eid = meta_i[:, 0:2, :]
    seg_start = jnp.sum(jnp.where(eid[..., None] == jnp.arange(N_EXPERTS, dtype=jnp.int32), offs, 0), axis=-1)
    dest = (seg_start + meta_i[:, 2:4, :]).astype(jnp.int32).reshape(-1)
    slots = jnp.arange((n_tiles + GMM_BUFFERS - 1) * GMM_TILE, dtype=jnp.int32)
    scratch_rows = 2 * n_rows + slots % (GMM_BUFFERS * GMM_TILE)
    dst = _slot_index(dest, scratch_rows * SUBLANES, n_blocks, n_rows)

    yp = _gmm(tile_expert.astype(jnp.int32), n_valid, dst, xn_tiles, w_gate, w_up, w_down, layer)
    gf = None if gfin is None else gfin.reshape(1, D_MODEL)
    return [_combine(h, meta_w, yp, gf, block0=b0, n_blocks=nb) for (b0, nb) in splits]


def _qkv_kernel(h_ref, g_ref, w_ref, q_ref, k_ref, v_ref):
    xn = _rmsnorm(h_ref[...], g_ref[...]).astype(BF16)
    width = ATT_HEADS * ATT_DH
    q_ref[...] = (_dot(xn, w_ref[:, 0:width]) * (ATT_DH ** -0.5)).astype(BF16)
    k_ref[...] = _dot(xn, w_ref[:, width:2 * width])
    v_ref[...] = _dot(xn, w_ref[:, 2 * width:3 * width])


def _qkv(h, g, w_bf16, n_blocks):
    n_rows = n_blocks * ROW_BLOCK
    width = ATT_HEADS * ATT_DH
    row_blk = lambda i: (i, 0)
    return pl.pallas_call(
        _qkv_kernel,
        out_shape=(
            jax.ShapeDtypeStruct((n_rows, width), BF16),
            jax.ShapeDtypeStruct((n_rows, width), F32),
            jax.ShapeDtypeStruct((n_rows, width), F32),
        ),
        grid=(n_blocks,),
        in_specs=[
            pl.BlockSpec((ROW_BLOCK, D_MODEL), row_blk),
            pl.BlockSpec((1, D_MODEL), lambda i: (0, 0)),
            pl.BlockSpec((D_MODEL, 3 * width), lambda i: (0, 0), pipeline_mode=pl.Buffered(1)),
        ],
        out_specs=(
            pl.BlockSpec((ROW_BLOCK, width), row_blk),
            pl.BlockSpec((ROW_BLOCK, width), row_blk),
            pl.BlockSpec((ROW_BLOCK, width), row_blk),
        ),
        compiler_params=pltpu.CompilerParams(dimension_semantics=("arbitrary",), vmem_limit_bytes=VMEM_LIMIT),
        name="qkv_proj",
    )(h, g, w_bf16)


def _softmax_parts(parts):
    m = functools.reduce(jnp.maximum, [jnp.max(s, axis=-1, keepdims=True) for s in parts])
    ps = [jnp.exp(s - m) for s in parts]
    inv = 1.0 / functools.reduce(jnp.add, [jnp.sum(p, axis=-1, keepdims=True) for p in ps])
    return ps, inv


def _band_prompt_kernel(h_ref, q_ref, ka_ref, kb_ref, kc_ref, va_ref, vb_ref, vc_ref, bm_ref, wout_ref,
                        out_ref, kw_ref, vw_ref, o_ref):
    i = pl.program_id(1)
    for b, (k_ref, v_ref) in enumerate(((ka_ref, va_ref), (kb_ref, vb_ref), (kc_ref, vc_ref))):
        kw_ref[pl.ds(b * ATT_QB, ATT_QB), :] = k_ref[...].astype(BF16)
        vw_ref[pl.ds(b * ATT_QB, ATT_QB), :] = v_ref[...].astype(BF16)
    col = lax.broadcasted_iota(jnp.int32, (1, ATT_WIN), 1)
    first_valid = (2 - jnp.minimum(i, 2)) * ATT_QB
    col_mask = jnp.where(col >= first_valid, 0.0, NEG_BIG)
    for h in range(ATT_HEADS):
        hs = slice(h * ATT_DH, (h + 1) * ATT_DH)
        s = _dot_nt(q_ref[:, hs], kw_ref[:, hs]) + bm_ref[h] + col_mask
        (p,), inv = _softmax_parts([s])
        o_ref[:, hs] = (_dot(p.astype(BF16), vw_ref[:, hs]) * inv).astype(BF16)
    out_ref[...] = _dot(o_ref[...], wout_ref[...]) + h_ref[...]


def _band_prompt(h, q, k, v, bm, wout, *, n_rows_total, n_seq, blocks_per_seq):
    width = ATT_HEADS * ATT_DH
    blk = lambda s, i: (s * blocks_per_seq + i, 0)
    back = lambda n: (lambda s, i: (s * blocks_per_seq + jnp.maximum(i - n, 0), 0))
    kv_spec = lambda n: pl.BlockSpec((ATT_QB, width), back(n))
    return pl.pallas_call(
        _band_prompt_kernel,
        out_shape=jax.ShapeDtypeStruct((n_rows_total, D_MODEL), F32),
        grid=(n_seq, blocks_per_seq),
        in_specs=[
            pl.BlockSpec((ATT_QB, D_MODEL), blk),
            pl.BlockSpec((ATT_QB, width), blk),
            kv_spec(2), kv_spec(1), kv_spec(0),
            kv_spec(2), kv_spec(1), kv_spec(0),
            pl.BlockSpec((ATT_HEADS, ATT_QB, ATT_WIN), lambda s, i: (0, 0, 0), pipeline_mode=pl.Buffered(1)),
            pl.BlockSpec((width, D_MODEL), lambda s, i: (0, 0), pipeline_mode=pl.Buffered(1)),
        ],
        out_specs=pl.BlockSpec((ATT_QB, D_MODEL), blk),
        scratch_shapes=[
            pltpu.VMEM((ATT_WIN, width), BF16),
            pltpu.VMEM((ATT_WIN, width), BF16),
            pltpu.VMEM((ATT_QB, width), BF16),
        ],
        compiler_params=pltpu.CompilerParams(dimension_semantics=("arbitrary", "arbitrary"), vmem_limit_bytes=VMEM_LIMIT),
        name="band_attn_prompt",
    )(h, q, k, k, k, v, v, v, bm, wout)


def _band_sample_kernel(h_ref, q_ref, kn_ref, vn_ref, ck_ref, cv_ref, b1_ref, b2_ref, wout_ref, hbuf_ref,
                        out_ref, o_ref):
    del hbuf_ref
    for h in range(ATT_HEADS):
        hs = slice(h * ATT_DH, (h + 1) * ATT_DH)
        qh = q_ref[:, hs]
        s1 = _dot(qh, ck_ref[0, h].astype(BF16)) + b1_ref[h]
        s2 = _dot_nt(qh, kn_ref[:, hs].astype(BF16)) + b2_ref[h]
        (p1, p2), inv = _softmax_parts([s1, s2])
        o = _dot_nt(p1.astype(BF16), cv_ref[0, h].astype(BF16)) + _dot(p2.astype(BF16), vn_ref[:, hs].astype(BF16))
        o_ref[:, hs] = (o * inv).astype(BF16)
    out_ref[...] = _dot(o_ref[...], wout_ref[...]) + h_ref[...]


def _band_sample(h, q, k, v, cache_k, cache_v, b1, b2, wout, hbuf, *, row0, n_seq, seq_len):
    width = ATT_HEADS * ATT_DH
    past = cache_k.shape[3]
    blk0 = row0 // seq_len
    blk = lambda s: (blk0 + s, 0)
    const3 = lambda s: (0, 0, 0)
    return pl.pallas_call(
        _band_sample_kernel,
        out_shape=jax.ShapeDtypeStruct(hbuf.shape, F32),
        grid=(n_seq,),
        in_specs=[
            pl.BlockSpec((seq_len, D_MODEL), blk),
            pl.BlockSpec((seq_len, width), blk),
            pl.BlockSpec((seq_len, width), blk),
            pl.BlockSpec((seq_len, width), blk),
            pl.BlockSpec((1, ATT_HEADS, ATT_DH, past), lambda s: (s, 0, 0, 0)),
            pl.BlockSpec((1, ATT_HEADS, ATT_DH, past), lambda s: (s, 0, 0, 0)),
            pl.BlockSpec((ATT_HEADS, seq_len, past), const3),
            pl.BlockSpec((ATT_HEADS, seq_len, seq_len), const3),
            pl.BlockSpec((width, D_MODEL), lambda s: (0, 0)),
            pl.BlockSpec(memory_space=pl.ANY),
        ],
        out_specs=pl.BlockSpec((seq_len, D_MODEL), blk),
        scratch_shapes=[pltpu.VMEM((seq_len, width), BF16)],
        input_output_aliases={9: 0},
        compiler_params=pltpu.CompilerParams(dimension_semantics=("arbitrary",), vmem_limit_bytes=VMEM_LIMIT),
        name="band_attn_sample",
    )(h, q, k, v, cache_k, cache_v, b1, b2, wout, hbuf)


def _band_bias_table(rb):
    n_top = BAND_PAST - REL_CLIP + 1
    n_var = REL_CLIP + CHUNK - 2
    n_bot = ATT_WIN - n_top - n_var
    n = ATT_WIN + ATT_QB
    heads = rb.shape[0]
    top = jnp.broadcast_to(rb[:, n_var + 1:n_var + 2], (heads, n_top))
    by_d = jnp.concatenate([top, rb[:, 1:n_var + 1][:, ::-1], jnp.broadcast_to(rb[:, 0:1], (heads, n_bot)),
                            jnp.broadcast_to(rb[:, n_var + 1:n_var + 2], (heads, ATT_QB))], axis=1)
    shifted = jnp.tile(by_d, (1, ATT_QB))[:, :ATT_QB * (n - 1)].reshape(heads, ATT_QB, n - 1)
    return shifted[:, :, :ATT_WIN]


def kernel(x_prompt, x_sample, state_gla, state_conv, cache_band_k, cache_band_v, norm_mix_g, norm_ffn_g, norm_final_g, w_in_even, w_alpha_up, b_alpha, gla_norm_g, conv_w, conv_b, conv_ln_g, conv_ln_b, w_out_even, w_qkv_odd, rel_bias, w_out_odd, w_router_grp, b_router_grp, w_router_exp, b_router_exp, w_exp_gate, w_exp_up, w_exp_down):
    batch, seq, _ = x_prompt.shape
    dec_batch, dec_seq, _ = x_sample.shape
    n_prompt = batch * seq
    n_sample = dec_batch * dec_seq
    assert seq % ROW_BLOCK == 0 and n_sample == ROW_BLOCK and seq % ATT_QB == 0
    n_rows = n_prompt + n_sample
    prompt_blocks = n_prompt // ROW_BLOCK
    n_blocks = n_rows // ROW_BLOCK
    width = ATT_HEADS * ATT_DH
    past = cache_band_k.shape[2]

    w_in = w_in_even[0]
    c0, c1, c2, c3, c4, c5 = (QK_W, 2 * QK_W, 2 * QK_W + V_W, 2 * QK_W + 2 * V_W,
                              2 * QK_W + 2 * V_W + GLA_LOWRANK, 2 * QK_W + 2 * V_W + GLA_LOWRANK + CONV_CH)
    w_in_p = jnp.concatenate(
        [w_in[:, :c3], w_in[:, c4:], w_in[:, c3:c4], jnp.zeros((D_MODEL, LANES - GLA_LOWRANK), F32)], axis=1).astype(BF16)
    wau_p = jnp.concatenate([w_alpha_up[0], jnp.zeros((LANES - GLA_LOWRANK, QK_W), F32)], axis=0).astype(BF16)
    even_w = (norm_mix_g[0].reshape(1, D_MODEL), w_in_p, wau_p, b_alpha[0].reshape(1, QK_W),
              gla_norm_g[0].reshape(1, V_W), conv_w[0], conv_b[0].reshape(1, CONV_CH),
              conv_ln_g[0].reshape(1, CONV_CH), conv_ln_b[0].reshape(1, CONV_CH), w_out_even[0].astype(BF16))
    h1, gla_p, conv_p = _even_mixer(
        x_prompt.reshape(n_prompt, D_MODEL), None, even_w,
        jnp.zeros((batch, GLA_HEADS, GLA_DK, GLA_DV), F32), jnp.zeros((batch, HIST, CONV_CH), F32),
        n_rows_total=n_rows, block0=0, n_seq=batch, blocks_per_seq=seq // ROW_BLOCK,
        chunk=CHUNK, cps=ROW_BLOCK // CHUNK, spb=1)
    h1, gla_s, conv_s = _even_mixer(
        x_sample.reshape(n_sample, D_MODEL), h1, even_w, state_gla[0], state_conv[0],
        n_rows_total=n_rows, block0=prompt_blocks, n_seq=1, blocks_per_seq=1,
        chunk=dec_seq, cps=1, spb=dec_batch)

    def moe(h, layer, gfin, splits):
        return _hier_moe(h, norm_ffn_g[layer], w_router_grp[layer], b_router_grp[layer], w_router_exp[layer],
                         b_router_exp[layer], w_exp_gate, w_exp_up, w_exp_down, layer, gfin, splits)

    (h2,) = moe(h1, 0, None, [(0, n_blocks)])

    q, k, v = _qkv(h2, norm_mix_g[1].reshape(1, D_MODEL), w_qkv_odd[0].astype(BF16), n_blocks)
    toe = _band_bias_table(rel_bias[0])
    r = jnp.arange(ATT_QB)[:, None]
    jb = jnp.arange(ATT_WIN)[None, :] - CHUNK * (r // CHUNK)
    in_band = jnp.logical_and(jb >= 0, jb < BAND_PAST + CHUNK)
    bm = jnp.where(in_band[None], toe, NEG_BIG)
    wout_odd = w_out_odd[0].astype(BF16)
    h3 = _band_prompt(h2, q, k, v, bm, wout_odd, n_rows_total=n_rows, n_seq=batch, blocks_per_seq=seq // ATT_QB)
    assert past == BAND_PAST and dec_seq <= CHUNK
    b1 = toe[:, :dec_seq, :past]
    b2 = toe[:, :dec_seq, past:past + dec_seq]
    h3 = _band_sample(h2, q, k, v, jnp.transpose(cache_band_k[0], (0, 2, 3, 1)),
                      jnp.transpose(cache_band_v[0], (0, 2, 3, 1)), b1, b2, wout_odd, h3,
                      row0=n_prompt, n_seq=dec_batch, seq_len=dec_seq)

    y_prompt, y_sample = moe(h3, 1, norm_final_g, [(0, prompt_blocks), (prompt_blocks, n_blocks - prompt_blocks)])

    rows = min(BAND_PAST, seq)
    tail = lambda a: jnp.stack([a[b * seq + seq - rows:(b + 1) * seq] for b in range(batch)]).reshape(
        batch, rows, ATT_HEADS, ATT_DH)
    kp, vp = tail(k), tail(v)
    ks = k[n_prompt:].reshape(dec_batch, dec_seq, ATT_HEADS, ATT_DH)
    vs = v[n_prompt:].reshape(dec_batch, dec_seq, ATT_HEADS, ATT_DH)
    return (y_prompt.reshape(batch, seq, D_MODEL), y_sample.reshape(dec_batch, dec_seq, D_MODEL),
            gla_p[None], gla_s[None], conv_p[None], conv_s[None], kp[None], vp[None], ks[None], vs[None])
```

```python
import functools

import jax
import jax.numpy as jnp
from jax import lax
from jax.experimental import pallas as pl
from jax.experimental.pallas import tpu as pltpu

F32 = jnp.float32
BF16 = jnp.bfloat16
HIGHEST = lax.Precision.HIGHEST

D_MODEL = 1024
CHUNK = 64
EPS = 1e-6
GLA_HEADS = 4
GLA_DK = 64
GLA_DV = 128
GLA_LOWRANK = 16
GLA_TAU = 16.0
GLA_SEPARABLE_MAX_EXPONENT = 40.0
CONV_CH = 512
CONV_WIDTH = 31
HIST = CONV_WIDTH - 1
ATT_HEADS = 16
ATT_DH = 64
BAND_CHUNKS_PAST = 8
BAND_PAST = BAND_CHUNKS_PAST * CHUNK
REL_CLIP = 256
N_GROUPS = 4
EXPERTS_PER_GROUP = 8
N_EXPERTS = N_GROUPS * EXPERTS_PER_GROUP
D_EXPERT = 512

LANES = 128
SUBLANES = 8
ROW_BLOCK = 512
CONV_TILE = 32
GMM_TILE = 256
ATT_QB = 256
ATT_WIN = ATT_QB + BAND_PAST
NEG_BIG = -1e30
VMEM_LIMIT = 56 * 1024 * 1024

QK_W = GLA_HEADS * GLA_DK
V_W = GLA_HEADS * GLA_DV
COL_Q = 0
COL_K = COL_Q + QK_W
COL_V = COL_K + QK_W
COL_GATE = COL_V + V_W
COL_CVAL = COL_GATE + V_W
COL_CGATE = COL_CVAL + CONV_CH
COL_ALR = COL_CGATE + CONV_CH
EVEN_COLS = COL_ALR + LANES


def _rmsnorm(x, g):
    return x * lax.rsqrt(jnp.mean(x * x, axis=-1, keepdims=True) + EPS) * g


def _sigmoid(x):
    return 1.0 / (1.0 + jnp.exp(-x))


def _dot(a, b):
    return jnp.dot(a, b, preferred_element_type=F32)


def _dot_nt(a, b):
    return lax.dot_general(a, b, (((1,), (1,)), ((), ())), preferred_element_type=F32)


def _dot_tn(a, b, precision=None):
    return lax.dot_general(a, b, (((0,), (0,)), ((), ())), preferred_element_type=F32, precision=precision)


def _even_mixer_kernel(*refs, chunk, cps, spb, carry, aliased):
    (x_ref, g_ref, win_ref, wau_ref, bal_ref, gng_ref, cw_ref, cb_ref, lng_ref, lnb_ref, wout_ref,
     s0_ref, c0_ref) = refs[:13]
    refs = refs[13 + (1 if aliased else 0):]
    (h_ref, sfin_ref, cfin_ref, proj_ref, lg_ref, cum_ref, intra_ref, free_ref, inc_ref, tcol_ref, qin_ref, mix_ref,
     s_ref, ubuf_ref) = refs
    j = pl.program_id(1)
    nj = pl.num_programs(1)
    seg = cps * chunk

    x = x_ref[...]
    xn = _rmsnorm(x, g_ref[...])
    proj_ref[...] = _dot(xn.astype(BF16), win_ref[...])
    alr = proj_ref[:, COL_ALR:COL_ALR + LANES]
    xa = _dot(alr.astype(BF16), wau_ref[...]) + bal_ref[...]
    lg_ref[...] = (jnp.minimum(xa, 0.0) - jnp.log1p(jnp.exp(-jnp.abs(xa)))) * (1.0 / GLA_TAU)

    row_i = lax.broadcasted_iota(jnp.int32, (chunk, QK_W), 0)
    tri_r = lax.broadcasted_iota(jnp.int32, (chunk, chunk), 0)
    tri_c = lax.broadcasted_iota(jnp.int32, (chunk, chunk), 1)
    causal = tri_r >= tri_c
    ones_cols = jnp.ones((chunk, LANES), F32)

    def intra_exact(r0):
        intra_ref[...] = jnp.zeros_like(intra_ref)
        k_all = proj_ref[pl.ds(r0, chunk), COL_K:COL_K + QK_W]
        sub_q = lax.broadcasted_iota(jnp.int32, (SUBLANES, QK_W), 0)
        sub_o = lax.broadcasted_iota(jnp.int32, (SUBLANES, V_W), 0)

        def row_body(i, c):
            grp = pl.multiple_of((i // SUBLANES) * SUBLANES, SUBLANES)
            pick = sub_q == (i % SUBLANES)
            cum_i = jnp.sum(jnp.where(pick, cum_ref[pl.ds(grp, SUBLANES), :], 0.0), axis=0, keepdims=True)
            q_rows = proj_ref[pl.ds(pl.multiple_of(r0 + grp, SUBLANES), SUBLANES), COL_Q:COL_Q + QK_W]
            q_i = jnp.sum(jnp.where(pick, q_rows, 0.0), axis=0, keepdims=True) * (GLA_DK ** -0.5)
            seen = row_i <= i
            p = jnp.where(seen, jnp.exp(jnp.where(seen, cum_i - cum_ref[...], 0.0)), 0.0) * k_all * q_i
            outs = []
            for h in range(GLA_HEADS):
                s_h = jnp.sum(p[:, h * GLA_DK:(h + 1) * GLA_DK], axis=-1, keepdims=True)
                v_h = proj_ref[pl.ds(r0, chunk), COL_V + h * GLA_DV:COL_V + (h + 1) * GLA_DV]
                outs.append(jnp.sum(s_h * v_h, axis=0, keepdims=True))
            o_row = jnp.concatenate(outs, axis=1)
            old = intra_ref[pl.ds(grp, SUBLANES), :]
            intra_ref[pl.ds(grp, SUBLANES), :] = jnp.where(sub_o == (i % SUBLANES), o_row, old)
            return c

        lax.fori_loop(0, chunk, row_body, 0)

    def seq_flags(ci):
        sq = ci // cps if spb > 1 else 0
        first = (ci % cps) == 0
        last = (ci % cps) == (cps - 1)
        if carry:
            first = jnp.logical_and(first, j == 0)
            last = jnp.logical_and(last, j == nj - 1)
        return sq, first, last

    def head_slices(h):
        return slice(h * GLA_DK, (h + 1) * GLA_DK), slice(h * GLA_DV, (h + 1) * GLA_DV)

    def load_head(r0, h):
        q = proj_ref[pl.ds(r0, chunk), COL_Q + h * GLA_DK:COL_Q + (h + 1) * GLA_DK] * (GLA_DK ** -0.5)
        k = proj_ref[pl.ds(r0, chunk), COL_K + h * GLA_DK:COL_K + (h + 1) * GLA_DK]
        v = proj_ref[pl.ds(r0, chunk), COL_V + h * GLA_DV:COL_V + (h + 1) * GLA_DV].astype(BF16)
        return q, k, v

    def finish_head(r0, h, o):
        _, vs = head_slices(h)
        gate = proj_ref[pl.ds(r0, chunk), COL_GATE + h * GLA_DV:COL_GATE + (h + 1) * GLA_DV]
        o = o * lax.rsqrt(jnp.mean(o * o, axis=-1, keepdims=True) + EPS)
        o = o * gng_ref[:, vs] * (gate * _sigmoid(gate))
        mix_ref[pl.ds(r0, chunk), vs] = o.astype(BF16)

    def prefix_sum(lg):
        cum = lg
        shift = 1
        while shift < chunk:
            cum = cum + jnp.where(row_i >= shift, pltpu.roll(cum, shift, 0), 0.0)
            shift *= 2
        return cum

    def chunk_exact(ci, c):
        r0 = pl.multiple_of(ci * chunk, chunk)
        sq, first, last = seq_flags(ci)

        @pl.when(first)
        def _():
            s_ref[...] = s0_ref[sq]

        lg = lg_ref[pl.ds(r0, chunk), :]
        cum = prefix_sum(lg)
        cum_ref[...] = cum
        tot = cum_ref[pl.ds(chunk - 1, 1), :]
        tot_col = jnp.exp(_dot_tn(lg, ones_cols, precision=HIGHEST))
        e_in = jnp.exp(cum)
        e_s = jnp.exp(tot - cum)
        intra_exact(r0)
        for h in range(GLA_HEADS):
            ks, vs = head_slices(h)
            q, k, v = load_head(r0, h)
            s_old = s_ref[h]
            o = _dot((q * e_in[:, ks]).astype(BF16), s_old.astype(BF16)) + intra_ref[:, vs]
            s_ref[h] = tot_col[h * GLA_DK:(h + 1) * GLA_DK, :] * s_old + _dot_tn((k * e_s[:, ks]).astype(BF16), v)
            finish_head(r0, h, o)

        @pl.when(last)
        def _():
            sfin_ref[sq] = s_ref[...]

        return c

    def chunk_free_terms(ci, c):
        r0 = pl.multiple_of(ci * chunk, chunk)
        lg = lg_ref[pl.ds(r0, chunk), :]
        cum = prefix_sum(lg)
        tot = jnp.sum(lg, axis=0, keepdims=True)
        mid = jnp.sum(lg[:chunk // 2], axis=0, keepdims=True)
        tcol_ref[ci] = jnp.exp(_dot_tn(lg, ones_cols, precision=HIGHEST))
        e_in = jnp.exp(cum)
        e_q = jnp.exp(cum - mid)
        e_k = jnp.exp(mid - cum)
        e_s = jnp.exp(tot - cum)
        for h in range(GLA_HEADS):
            ks, vs = head_slices(h)
            q, k, v = load_head(r0, h)
            scores = _dot_nt((q * e_q[:, ks]).astype(BF16), (k * e_k[:, ks]).astype(BF16))
            free_ref[pl.ds(r0, chunk), vs] = _dot(jnp.where(causal, scores, 0.0).astype(BF16), v)
            inc_ref[ci, h] = _dot_tn((k * e_s[:, ks]).astype(BF16), v)
            qin_ref[pl.ds(r0, chunk), ks] = (q * e_in[:, ks]).astype(BF16)
        return c

    def chunk_recurrence(ci, state):
        r0 = pl.multiple_of(ci * chunk, chunk)
        sq, first, last = seq_flags(ci)
        new_state = []
        for h in range(GLA_HEADS):
            ks, vs = head_slices(h)
            s_old = state[h] if carry else jnp.where(first, s0_ref[sq, h], state[h])
            o = _dot(qin_ref[pl.ds(r0, chunk), ks], s_old.astype(BF16)) + free_ref[pl.ds(r0, chunk), vs]
            new_state.append(tcol_ref[ci, h * GLA_DK:(h + 1) * GLA_DK, :] * s_old + inc_ref[ci, h])
            finish_head(r0, h, o)

        @pl.when(last)
        def _():
            for h in range(GLA_HEADS):
                sfin_ref[sq, h] = new_state[h]

        return tuple(new_state)

    in_range = jnp.max(-lg_ref[...]) * (chunk // 2) < GLA_SEPARABLE_MAX_EXPONENT

    @pl.when(in_range)
    def _():
        lax.fori_loop(0, spb * cps, chunk_free_terms, 0, unroll=4)
        if carry:
            @pl.when(j == 0)
            def _():
                s_ref[...] = s0_ref[0]
            state = tuple(s_ref[h] for h in range(GLA_HEADS))
        else:
            state = tuple(jnp.zeros((GLA_DK, GLA_DV), F32) for _ in range(GLA_HEADS))
        state = lax.fori_loop(0, spb * cps, chunk_recurrence, state, unroll=4)
        if carry:
            for h in range(GLA_HEADS):
                s_ref[h] = state[h]

    @pl.when(jnp.logical_not(in_range))
    def _():
        lax.fori_loop(0, spb * cps, chunk_exact, 0)

    tile = CONV_TILE

    def conv_tile(t0, out_r0):
        wv = ubuf_ref[pl.ds(t0, 2 * tile), :]
        acc = jnp.zeros((tile, CONV_CH), F32)
        for b in range(SUBLANES):
            sb = wv if b == 0 else pltpu.roll(wv, 2 * tile - b, 0)
            for a in range(tile // SUBLANES + 1):
                off = SUBLANES * a + b
                if 2 <= off <= HIST + 2:
                    acc = acc + sb[SUBLANES * a:SUBLANES * a + tile, :] * cw_ref[pl.ds(off - 2, 1), :]
        proj_ref[pl.ds(out_r0, tile), COL_CVAL:COL_CVAL + CONV_CH] = acc + cb_ref[...]

    def conv_norm(r0):
        cv = proj_ref[pl.ds(r0, seg), COL_CVAL:COL_CVAL + CONV_CH]
        mu = jnp.mean(cv, axis=-1, keepdims=True)
        var = jnp.mean(jnp.square(cv - mu), axis=-1, keepdims=True)
        y = (cv - mu) * lax.rsqrt(var + EPS) * lng_ref[...] + lnb_ref[...]
        mix_ref[pl.ds(r0, seg), V_W:V_W + CONV_CH] = (y * _sigmoid(y)).astype(BF16)

    def conv_seg(sq, c):
        r0 = pl.multiple_of(sq * seg, seg) if spb > 1 else 0
        cval = proj_ref[pl.ds(r0, seg), COL_CVAL:COL_CVAL + CONV_CH]
        cgate = proj_ref[pl.ds(r0, seg), COL_CGATE:COL_CGATE + CONV_CH]

        def load_history():
            ubuf_ref[pl.ds(0, 8), :] = jnp.zeros((8, CONV_CH), F32)
            ubuf_ref[pl.ds(2, HIST), :] = c0_ref[sq]

        if carry:
            pl.when(j == 0)(load_history)
        else:
            load_history()
        ubuf_ref[pl.ds(CONV_TILE, seg), :] = cval * _sigmoid(cgate)
        if seg == tile:
            conv_tile(0, r0)
        else:
            def tile_body(t, cc):
                t0 = pl.multiple_of(t * tile, tile)
                conv_tile(t0, r0 + t0)
                return cc
            lax.fori_loop(0, seg // tile, tile_body, 0)
        conv_norm(r0)
        hist = ubuf_ref[pl.ds(seg + 2, HIST), :]
        if carry:
            @pl.when(j == nj - 1)
            def _():
                cfin_ref[sq] = hist
            ubuf_ref[pl.ds(2, HIST), :] = hist
        else:
            cfin_ref[sq] = hist
        return c

    if spb > 1:
        lax.fori_loop(0, spb, conv_seg, 0)
    else:
        conv_seg(0, 0)

    h_ref[...] = _dot(mix_ref[...], wout_ref[...]) + x


def _even_mixer(x2d, hbuf, weights, s0, c0, *, n_rows_total, block0, n_seq, blocks_per_seq, chunk, cps, spb):
    carry = spb == 1
    aliased = hbuf is not None
    const = lambda s, j: (0, 0)
    seq_blk = (lambda s, j: (s, 0, 0, 0)) if carry else (lambda s, j: (0, 0, 0, 0))
    seq_blk3 = (lambda s, j: (s, 0, 0)) if carry else (lambda s, j: (0, 0, 0))
    n_state = 1 if carry else spb
    wspec = lambda shape: pl.BlockSpec(shape, const, pipeline_mode=pl.Buffered(1))
    in_specs = [
        pl.BlockSpec((ROW_BLOCK, D_MODEL), lambda s, j: (s * blocks_per_seq + j, 0)),
        wspec((1, D_MODEL)),
        wspec((D_MODEL, EVEN_COLS)),
        wspec((LANES, QK_W)),
        wspec((1, QK_W)),
        wspec((1, V_W)),
        wspec((CONV_WIDTH, CONV_CH)),
        wspec((1, CONV_CH)),
        wspec((1, CONV_CH)),
        wspec((1, CONV_CH)),
        wspec((V_W + CONV_CH, D_MODEL)),
        pl.BlockSpec((n_state, GLA_HEADS, GLA_DK, GLA_DV), seq_blk),
        pl.BlockSpec((n_state, HIST, CONV_CH), seq_blk3),
    ]
    args = [x2d, *weights, s0, c0]
    aliases = {}
    if aliased:
        in_specs.append(pl.BlockSpec(memory_space=pl.ANY))
        args.append(hbuf)
        aliases = {len(args) - 1: 0}
    n_all = s0.shape[0]
    out_shape = (
        jax.ShapeDtypeStruct((n_rows_total, D_MODEL), F32),
        jax.ShapeDtypeStruct((n_all, GLA_HEADS, GLA_DK, GLA_DV), F32),
        jax.ShapeDtypeStruct((n_all, HIST, CONV_CH), F32),
    )
    out_specs = (
        pl.BlockSpec((ROW_BLOCK, D_MODEL), lambda s, j: (block0 + s * blocks_per_seq + j, 0)),
        pl.BlockSpec((n_state, GLA_HEADS, GLA_DK, GLA_DV), seq_blk),
        pl.BlockSpec((n_state, HIST, CONV_CH), seq_blk3),
    )
    seg = cps * chunk
    scratch = [
        pltpu.VMEM((ROW_BLOCK, EVEN_COLS), F32),
        pltpu.VMEM((ROW_BLOCK, QK_W), F32),
        pltpu.VMEM((chunk, QK_W), F32),
        pltpu.VMEM((chunk, V_W), F32),
        pltpu.VMEM((ROW_BLOCK, V_W), F32),
        pltpu.VMEM((spb * cps, GLA_HEADS, GLA_DK, GLA_DV), F32),
        pltpu.VMEM((spb * cps, QK_W, LANES), F32),
        pltpu.VMEM((ROW_BLOCK, QK_W), BF16),
        pltpu.VMEM((ROW_BLOCK, V_W + CONV_CH), BF16),
        pltpu.VMEM((GLA_HEADS, GLA_DK, GLA_DV), F32),
        pltpu.VMEM((CONV_TILE + seg, CONV_CH), F32),
    ]
    kern = functools.partial(_even_mixer_kernel, chunk=chunk, cps=cps, spb=spb, carry=carry, aliased=aliased)
    return pl.pallas_call(
        kern, out_shape=out_shape, grid=(n_seq, blocks_per_seq), in_specs=in_specs, out_specs=out_specs,
        scratch_shapes=scratch, input_output_aliases=aliases,
        compiler_params=pltpu.CompilerParams(dimension_semantics=("arbitrary", "arbitrary"), vmem_limit_bytes=VMEM_LIMIT),
        name="even_mixer_carry" if carry else "even_mixer_step",
    )(*args)


def _to_row_tiles(ref, x):
    rows = x.shape[0]
    for c in range(x.shape[1] // LANES):
        ref[pl.ds(c, rows, stride=SUBLANES), :] = x[:, c * LANES:(c + 1) * LANES]


def _from_row_tiles(ref, rows, width=D_MODEL):
    return jnp.concatenate([ref[pl.ds(c, rows, stride=SUBLANES), :] for c in range(width // LANES)], axis=1)


def _route_kernel(h_ref, g_ref, wr_hi_ref, wr_lo_ref, br_ref, xn_ref, mi_ref, mw_ref, cnt_ref, base_ref):
    i = pl.program_id(0)

    @pl.when(i == 0)
    def _():
        base_ref[...] = jnp.zeros_like(base_ref)

    xn = _rmsnorm(h_ref[...], g_ref[...])
    _to_row_tiles(xn_ref, xn)
    x_hi = xn.astype(BF16)
    x_lo = (xn - x_hi.astype(F32)).astype(BF16)
    logits = _dot(x_hi, wr_hi_ref[...]) + (_dot(x_lo, wr_hi_ref[...]) + _dot(x_hi, wr_lo_ref[...])) + br_ref[...]
    lane = lax.broadcasted_iota(jnp.int32, logits.shape, 1).astype(F32)
    far = float(1 << 20)
    gl = jnp.where(lane < N_GROUPS, logits, -jnp.inf)
    gmax = jnp.max(gl, axis=-1, keepdims=True)
    gidx = jnp.min(jnp.where(gl == gmax, lane, far), axis=-1, keepdims=True)
    gw = 1.0 / jnp.sum(jnp.exp(gl - gmax), axis=-1, keepdims=True)
    lo = N_GROUPS + gidx * EXPERTS_PER_GROUP
    el = jnp.where(jnp.logical_and(lane >= lo, lane < lo + EXPERTS_PER_GROUP), logits, -jnp.inf)
    m1 = jnp.max(el, axis=-1, keepdims=True)
    i1 = jnp.min(jnp.where(el == m1, lane, far), axis=-1, keepdims=True)
    el2 = jnp.where(lane == i1, -jnp.inf, el)
    m2 = jnp.max(el2, axis=-1, keepdims=True)
    i2 = jnp.min(jnp.where(el2 == m2, lane, far), axis=-1, keepdims=True)
    e2 = jnp.exp(m2 - m1)
    den = 1.0 + e2
    w0 = (1.0 / den) * gw
    w1 = (e2 / den) * gw
    id0 = i1 - N_GROUPS
    id1 = i2 - N_GROUPS
    oh0 = jnp.where(lane == id0, 1.0, 0.0)
    oh1 = jnp.where(lane == id1, 1.0, 0.0)
    rr = lax.broadcasted_iota(jnp.int32, (ROW_BLOCK, ROW_BLOCK), 0)
    cc = lax.broadcasted_iota(jnp.int32, (ROW_BLOCK, ROW_BLOCK), 1)
    below = jnp.where(rr > cc, 1.0, 0.0).astype(BF16)
    p0 = _dot(below, oh0.astype(BF16))
    p1 = _dot(below, oh1.astype(BF16))
    cnt0 = jnp.sum(oh0, axis=0, keepdims=True)
    cnt1 = jnp.sum(oh1, axis=0, keepdims=True)
    base = base_ref[...]
    rank0 = jnp.sum(oh0 * (p0 + base), axis=-1, keepdims=True)
    rank1 = jnp.sum(oh1 * (p1 + base + cnt0), axis=-1, keepdims=True)
    new_base = base + cnt0 + cnt1
    base_ref[...] = new_base
    cnt_ref[...] = new_base.astype(jnp.int32)
    meta = jnp.where(lane == 0, id0, jnp.where(lane == 1, id1, jnp.where(lane == 2, rank0, jnp.where(lane == 3, rank1, 0.0))))
    mi_ref[0] = meta.T[0:8, :].astype(jnp.int32)
    mw_ref[...] = jnp.where(lane == 0, w0, jnp.where(lane == 1, w1, 0.0))


def _route(h, g, wr, br, n_blocks):
    n_rows = n_blocks * ROW_BLOCK
    wr_hi = wr.astype(BF16)
    wr_hi_residual = (wr - wr_hi.astype(F32)).astype(BF16)
    const = lambda i: (0, 0)
    return pl.pallas_call(
        _route_kernel,
        out_shape=(
            jax.ShapeDtypeStruct((n_rows * SUBLANES, LANES), F32),
            jax.ShapeDtypeStruct((n_blocks, 8, ROW_BLOCK), jnp.int32),
            jax.ShapeDtypeStruct((n_rows, LANES), F32),
            jax.ShapeDtypeStruct((1, LANES), jnp.int32),
        ),
        grid=(n_blocks,),
        in_specs=[
            pl.BlockSpec((ROW_BLOCK, D_MODEL), lambda i: (i, 0)),
            pl.BlockSpec((1, D_MODEL), const),
            pl.BlockSpec((D_MODEL, LANES), const),
            pl.BlockSpec((D_MODEL, LANES), const),
            pl.BlockSpec((1, LANES), const),
        ],
        out_specs=(
            pl.BlockSpec((ROW_BLOCK * SUBLANES, LANES), lambda i: (i, 0)),
            pl.BlockSpec((1, 8, ROW_BLOCK), lambda i: (i, 0, 0)),
            pl.BlockSpec((ROW_BLOCK, LANES), lambda i: (i, 0)),
            pl.BlockSpec((1, LANES), const),
        ),
        scratch_shapes=[pltpu.VMEM((1, LANES), F32)],
        compiler_params=pltpu.CompilerParams(dimension_semantics=("arbitrary",), vmem_limit_bytes=VMEM_LIMIT),
        name="moe_route",
    )(h, g, wr_hi, wr_hi_residual, br)


def _slot_index_kernel(dest_ref, dst0_ref, dst_ref, sem, *, n_blocks, n_rows):
    init = pltpu.make_async_copy(dst0_ref, dst_ref, sem)
    init.start()
    init.wait()

    def block(b, c):
        for k in range(2):
            def body(j, cc):
                d = dest_ref[b * (2 * ROW_BLOCK) + k * ROW_BLOCK + j]
                dst_ref[d] = (k * n_rows + b * ROW_BLOCK + j) * SUBLANES
                return cc

            lax.fori_loop(0, ROW_BLOCK, body, 0, unroll=8)
        return c

    lax.fori_loop(0, n_blocks, block, 0)


def _slot_index(dest_flat, dst0, n_blocks, n_rows):
    return pl.pallas_call(
        functools.partial(_slot_index_kernel, n_blocks=n_blocks, n_rows=n_rows),
        out_shape=jax.ShapeDtypeStruct(dst0.shape, jnp.int32),
        in_specs=[pl.BlockSpec(memory_space=pltpu.SMEM), pl.BlockSpec(memory_space=pl.ANY)],
        out_specs=pl.BlockSpec(memory_space=pltpu.SMEM),
        scratch_shapes=[pltpu.SemaphoreType.DMA(())],
        name="moe_slot_index",
    )(dest_flat, dst0)


GMM_ISSUE_GROUPS = 8
GMM_BUFFERS = 3
GATHER_PRIORITY = 0
SCATTER_PRIORITY = 1


def _gmm_kernel(te_ref, nv_ref, dst_ref, xn_ref, wg_ref, wu_ref, wd_ref, yp_ref,
                xbuf, ybuf, wgb, wub, wdb, gsem, ssem):
    i = pl.program_id(0)
    nv = nv_ref[0]
    n_tiles = pl.num_programs(0) - 1
    slot = i % GMM_BUFFERS
    nxt = (i + 1) % GMM_BUFFERS
    prv = (i + 2) % GMM_BUFFERS

    in_rows = xn_ref.shape[0]

    def gather_copy(tile, r, buf):
        d = dst_ref[tile * GMM_TILE + r]
        d = jnp.where(d >= in_rows, d - in_rows, d)
        src = pl.multiple_of(jnp.where(d >= in_rows, d - in_rows, d), SUBLANES)
        return pltpu.make_async_copy(xn_ref.at[pl.ds(src, SUBLANES), :],
                                     xbuf.at[buf, pl.ds(r * SUBLANES, SUBLANES), :], gsem.at[buf])

    def scatter_copy(tile, r, buf):
        d = pl.multiple_of(dst_ref[tile * GMM_TILE + r], SUBLANES)
        return pltpu.make_async_copy(ybuf.at[buf, pl.ds(r * SUBLANES, SUBLANES), :],
                                     yp_ref.at[pl.ds(d, SUBLANES), :], ssem.at[buf])

    def wait_tile(copy_of_row, buf):
        def body(r, c):
            copy_of_row(0, 0, buf).wait()
            return c

        lax.fori_loop(0, GMM_TILE, body, 0, unroll=16)

    @pl.when(i == 0)
    def _():
        ybuf[...] = jnp.zeros_like(ybuf)

        def body(r, c):
            gather_copy(0, r, 0).start(priority=GATHER_PRIORITY)
            gather_copy(1, r, 1).start(priority=GATHER_PRIORITY)
            return c

        lax.fori_loop(0, GMM_TILE, body, 0, unroll=8)

    @pl.when(jnp.logical_and(i >= 2, i <= nv))
    def _():
        wait_tile(scatter_copy, slot)

    @pl.when(i < nv)
    def _():
        wait_tile(gather_copy, slot)
        prev = te_ref[jnp.maximum(i - 1, 0)]

        @pl.when(jnp.logical_or(i == 0, te_ref[i] != prev))
        def _():
            wgb[...] = wg_ref[0, 0].astype(BF16)
            wub[...] = wu_ref[0, 0].astype(BF16)
            wdb[...] = wd_ref[0, 0].astype(BF16)

        prev_tile = jnp.where(i >= 1, i - 1, n_tiles)
        rows_per_group = GMM_TILE // GMM_ISSUE_GROUPS

        def issue(group):
            for r in range(group * rows_per_group, (group + 1) * rows_per_group):
                gather_copy(i + 2, r, prv).start(priority=GATHER_PRIORITY)
                scatter_copy(prev_tile, r, prv).start(priority=SCATTER_PRIORITY)

        x = _from_row_tiles(xbuf.at[slot], GMM_TILE).astype(BF16)
        half = D_EXPERT // 2
        quarter = D_MODEL // 4
        a0 = _dot(x, wgb[:, 0:half])
        issue(0)
        a1 = _dot(x, wgb[:, half:D_EXPERT])
        issue(1)
        b0 = _dot(x, wub[:, 0:half])
        issue(2)
        b1 = _dot(x, wub[:, half:D_EXPERT])
        issue(3)
        he = jnp.concatenate([(a0 * _sigmoid(a0)) * b0, (a1 * _sigmoid(a1)) * b1], axis=1).astype(BF16)
        for n in range(4):
            yn = _dot(he, wdb[:, n * quarter:(n + 1) * quarter])
            for c in range(quarter // LANES):
                ybuf[slot, pl.ds(n * (quarter // LANES) + c, GMM_TILE, stride=SUBLANES), :] = yn[:, c * LANES:(c + 1) * LANES]
            issue(4 + n)

    @pl.when(i == nv)
    def _():
        wait_tile(gather_copy, slot)
        wait_tile(gather_copy, nxt)
        wait_tile(scatter_copy, nxt)

        def body(r, c):
            scatter_copy(nv - 1, r, prv).start(priority=SCATTER_PRIORITY)
            return c

        lax.fori_loop(0, GMM_TILE, body, 0, unroll=8)
        wait_tile(scatter_copy, prv)


def _gmm(tile_expert, n_valid, dst, xn_tiles, w_gate, w_up, w_down, layer):
    n_rows = xn_tiles.shape[0] // SUBLANES
    n_tiles = tile_expert.shape[0] - 1

    def w_idx(i, te, nv, dst):
        return (layer, te[jnp.maximum(jnp.minimum(i, nv[0] - 1), 0)], 0, 0)

    return pl.pallas_call(
        _gmm_kernel,
        out_shape=jax.ShapeDtypeStruct(((2 * n_rows + GMM_BUFFERS * GMM_TILE) * SUBLANES, LANES), F32),
        grid_spec=pltpu.PrefetchScalarGridSpec(
            num_scalar_prefetch=3,
            grid=(n_tiles + 1,),
            in_specs=[
                pl.BlockSpec(memory_space=pl.ANY),
                pl.BlockSpec((1, 1, D_MODEL, D_EXPERT), w_idx),
                pl.BlockSpec((1, 1, D_MODEL, D_EXPERT), w_idx),
                pl.BlockSpec((1, 1, D_EXPERT, D_MODEL), w_idx),
            ],
            out_specs=pl.BlockSpec(memory_space=pl.ANY),
            scratch_shapes=[
                pltpu.VMEM((GMM_BUFFERS, GMM_TILE * SUBLANES, LANES), F32),
                pltpu.VMEM((GMM_BUFFERS, GMM_TILE * SUBLANES, LANES), F32),
                pltpu.VMEM((D_MODEL, D_EXPERT), BF16),
                pltpu.VMEM((D_MODEL, D_EXPERT), BF16),
                pltpu.VMEM((D_EXPERT, D_MODEL), BF16),
                pltpu.SemaphoreType.DMA((GMM_BUFFERS,)),
                pltpu.SemaphoreType.DMA((GMM_BUFFERS,)),
            ],
        ),
        compiler_params=pltpu.CompilerParams(dimension_semantics=("arbitrary",), vmem_limit_bytes=VMEM_LIMIT),
        name="moe_gmm",
    )(tile_expert, n_valid, dst, xn_tiles, w_gate, w_up, w_down)


def _combine_kernel(*refs, final):
    if final:
        h_ref, mw_ref, y0_ref, y1_ref, gfin_ref, out_ref = refs
    else:
        h_ref, mw_ref, y0_ref, y1_ref, out_ref = refs
    mw = mw_ref[...]
    y0 = _from_row_tiles(y0_ref, ROW_BLOCK)
    y1 = _from_row_tiles(y1_ref, ROW_BLOCK)
    hn = h_ref[...] + (mw[:, 0:1] * y0 + mw[:, 1:2] * y1)
    out_ref[...] = _rmsnorm(hn, gfin_ref[...]) if final else hn


def _combine(h, mw, yp, gfin, *, block0, n_blocks):
    final = gfin is not None
    total_blocks = h.shape[0] // ROW_BLOCK
    in_specs = [
        pl.BlockSpec((ROW_BLOCK, D_MODEL), lambda i: (block0 + i, 0)),
        pl.BlockSpec((ROW_BLOCK, LANES), lambda i: (block0 + i, 0)),
        pl.BlockSpec((ROW_BLOCK * SUBLANES, LANES), lambda i: (block0 + i, 0)),
        pl.BlockSpec((ROW_BLOCK * SUBLANES, LANES), lambda i: (total_blocks + block0 + i, 0)),
    ]
    args = [h, mw, yp, yp]
    if final:
        in_specs.append(pl.BlockSpec((1, D_MODEL), lambda i: (0, 0)))
        args.append(gfin)
    return pl.pallas_call(
        functools.partial(_combine_kernel, final=final),
        out_shape=jax.ShapeDtypeStruct((n_blocks * ROW_BLOCK, D_MODEL), F32),
        grid=(n_blocks,),
        in_specs=in_specs,
        out_specs=pl.BlockSpec((ROW_BLOCK, D_MODEL), lambda i: (i, 0)),
        compiler_params=pltpu.CompilerParams(dimension_semantics=("arbitrary",), vmem_limit_bytes=VMEM_LIMIT),
        name="moe_combine_final" if final else "moe_combine",
    )(*args)


def _moe_tiles(n_rows):
    return (2 * n_rows + N_EXPERTS * (GMM_TILE - 1) + GMM_TILE - 1) // GMM_TILE


def _hier_moe(h, g, w_rg, b_rg, w_re, b_re, w_gate, w_up, w_down, layer, gfin, splits):
    n_rows = h.shape[0]
    n_blocks = n_rows // ROW_BLOCK
    n_re = N_GROUPS * EXPERTS_PER_GROUP
    wr = jnp.zeros((D_MODEL, LANES), F32)
    wr = wr.at[:, :N_GROUPS].set(w_rg).at[:, N_GROUPS:N_GROUPS + n_re].set(w_re.reshape(D_MODEL, n_re))
    br = jnp.zeros((1, LANES), F32)
    br = br.at[0, :N_GROUPS].set(b_rg).at[0, N_GROUPS:N_GROUPS + n_re].set(b_re.reshape(n_re))
    xn_tiles, meta_i, meta_w, counts = _route(h, g.reshape(1, D_MODEL), wr, br, n_blocks)

    n_tiles = _moe_tiles(n_rows)
    cnt = counts[0, :N_EXPERTS]
    padded = ((cnt + GMM_TILE - 1) // GMM_TILE) * GMM_TILE
    ends = jnp.cumsum(padded)
    offs = ends - padded
    tile_start = jnp.arange(n_tiles + 1, dtype=jnp.int32) * GMM_TILE
    tile_expert = jnp.minimum(jnp.sum((tile_start[:, None] >= ends[None, :]).astype(jnp.int32), axis=1), N_EXPERTS - 1)
    n_valid = (ends[-1:] // GMM_TILE).astype(jnp.int32)
    eid = meta_i[:, 0:2, :]
    seg_start = jnp.sum(jnp.where(eid[..., None] == jnp.arange(N_EXPERTS, dtype=jnp.int32), offs, 0), axis=-1)
    dest = (seg_start + meta_i[:, 2:4, :]).astype(jnp.int32).reshape(-1)
    slots = jnp.arange((n_tiles + GMM_BUFFERS - 1) * GMM_TILE, dtype=jnp.int32)
    scratch_rows = 2 * n_rows + slots % (GMM_BUFFERS * GMM_TILE)
    dst = _slot_index(dest, scratch_rows * SUBLANES, n_blocks, n_rows)

    yp = _gmm(tile_expert.astype(jnp.int32), n_valid, dst, xn_tiles, w_gate, w_up, w_down, layer)
    gf = None if gfin is None else gfin.reshape(1, D_MODEL)
    return [_combine(h, meta_w, yp, gf, block0=b0, n_blocks=nb) for (b0, nb) in splits]


def _qkv_kernel(h_ref, g_ref, w_ref, q_ref, k_ref, v_ref):
    xn = _rmsnorm(h_ref[...], g_ref[...]).astype(BF16)
    width = ATT_HEADS * ATT_DH
    q_ref[...] = (_dot(xn, w_ref[:, 0:width]) * (ATT_DH ** -0.5)).astype(BF16)
    k_ref[...] = _dot(xn, w_ref[:, width:2 * width])
    v_ref[...] = _dot(xn, w_ref[:, 2 * width:3 * width])


def _qkv(h, g, w_bf16, n_blocks):
    n_rows = n_blocks * ROW_BLOCK
    width = ATT_HEADS * ATT_DH
    row_blk = lambda i: (i, 0)
    return pl.pallas_call(
        _qkv_kernel,
        out_shape=(
            jax.ShapeDtypeStruct((n_rows, width), BF16),
            jax.ShapeDtypeStruct((n_rows, width), F32),
            jax.ShapeDtypeStruct((n_rows, width), F32),
        ),
        grid=(n_blocks,),
        in_specs=[
            pl.BlockSpec((ROW_BLOCK, D_MODEL), row_blk),
            pl.BlockSpec((1, D_MODEL), lambda i: (0, 0)),
            pl.BlockSpec((D_MODEL, 3 * width), lambda i: (0, 0), pipeline_mode=pl.Buffered(1)),
        ],
        out_specs=(
            pl.BlockSpec((ROW_BLOCK, width), row_blk),
            pl.BlockSpec((ROW_BLOCK, width), row_blk),
            pl.BlockSpec((ROW_BLOCK, width), row_blk),
        ),
        compiler_params=pltpu.CompilerParams(dimension_semantics=("arbitrary",), vmem_limit_bytes=VMEM_LIMIT),
        name="qkv_proj",
    )(h, g, w_bf16)


def _softmax_parts(parts):
    m = functools.reduce(jnp.maximum, [jnp.max(s, axis=-1, keepdims=True) for s in parts])
    ps = [jnp.exp(s - m) for s in parts]
    inv = 1.0 / functools.reduce(jnp.add, [jnp.sum(p, axis=-1, keepdims=True) for p in ps])
    return ps, inv


def _band_prompt_kernel(h_ref, q_ref, ka_ref, kb_ref, kc_ref, va_ref, vb_ref, vc_ref, bm_ref, wout_ref,
                        out_ref, kw_ref, vw_ref, o_ref):
    i = pl.program_id(1)
    for b, (k_ref, v_ref) in enumerate(((ka_ref, va_ref), (kb_ref, vb_ref), (kc_ref, vc_ref))):
        kw_ref[pl.ds(b * ATT_QB, ATT_QB), :] = k_ref[...].astype(BF16)
        vw_ref[pl.ds(b * ATT_QB, ATT_QB), :] = v_ref[...].astype(BF16)
    col = lax.broadcasted_iota(jnp.int32, (1, ATT_WIN), 1)
    first_valid = (2 - jnp.minimum(i, 2)) * ATT_QB
    col_mask = jnp.where(col >= first_valid, 0.0, NEG_BIG)
    for h in range(ATT_HEADS):
        hs = slice(h * ATT_DH, (h + 1) * ATT_DH)
        s = _dot_nt(q_ref[:, hs], kw_ref[:, hs]) + bm_ref[h] + col_mask
        (p,), inv = _softmax_parts([s])
        o_ref[:, hs] = (_dot(p.astype(BF16), vw_ref[:, hs]) * inv).astype(BF16)
    out_ref[...] = _dot(o_ref[...], wout_ref[...]) + h_ref[...]


def _band_prompt(h, q, k, v, bm, wout, *, n_rows_total, n_seq, blocks_per_seq):
    width = ATT_HEADS * ATT_DH
    blk = lambda s, i: (s * blocks_per_seq + i, 0)
    back = lambda n: (lambda s, i: (s * blocks_per_seq + jnp.maximum(i - n, 0), 0))
    kv_spec = lambda n: pl.BlockSpec((ATT_QB, width), back(n))
    return pl.pallas_call(
        _band_prompt_kernel,
        out_shape=jax.ShapeDtypeStruct((n_rows_total, D_MODEL), F32),
        grid=(n_seq, blocks_per_seq),
        in_specs=[
            pl.BlockSpec((ATT_QB, D_MODEL), blk),
            pl.BlockSpec((ATT_QB, width), blk),
            kv_spec(2), kv_spec(1), kv_spec(0),
            kv_spec(2), kv_spec(1), kv_spec(0),
            pl.BlockSpec((ATT_HEADS, ATT_QB, ATT_WIN), lambda s, i: (0, 0, 0), pipeline_mode=pl.Buffered(1)),
            pl.BlockSpec((width, D_MODEL), lambda s, i: (0, 0), pipeline_mode=pl.Buffered(1)),
        ],
        out_specs=pl.BlockSpec((ATT_QB, D_MODEL), blk),
        scratch_shapes=[
            pltpu.VMEM((ATT_WIN, width), BF16),
            pltpu.VMEM((ATT_WIN, width), BF16),
            pltpu.VMEM((ATT_QB, width), BF16),
        ],
        compiler_params=pltpu.CompilerParams(dimension_semantics=("arbitrary", "arbitrary"), vmem_limit_bytes=VMEM_LIMIT),
        name="band_attn_prompt",
    )(h, q, k, k, k, v, v, v, bm, wout)


def _band_sample_kernel(h_ref, q_ref, kn_ref, vn_ref, ck_ref, cv_ref, b1_ref, b2_ref, wout_ref, hbuf_ref,
                        out_ref, o_ref):
    del hbuf_ref
    for h in range(ATT_HEADS):
        hs = slice(h * ATT_DH, (h + 1) * ATT_DH)
        qh = q_ref[:, hs]
        s1 = _dot(qh, ck_ref[0, h].astype(BF16)) + b1_ref[h]
        s2 = _dot_nt(qh, kn_ref[:, hs].astype(BF16)) + b2_ref[h]
        (p1, p2), inv = _softmax_parts([s1, s2])
        o = _dot_nt(p1.astype(BF16), cv_ref[0, h].astype(BF16)) + _dot(p2.astype(BF16), vn_ref[:, hs].astype(BF16))
        o_ref[:, hs] = (o * inv).astype(BF16)
    out_ref[...] = _dot(o_ref[...], wout_ref[...]) + h_ref[...]


def _band_sample(h, q, k, v, cache_k, cache_v, b1, b2, wout, hbuf, *, row0, n_seq, seq_len):
    width = ATT_HEADS * ATT_DH
    past = cache_k.shape[3]
    blk0 = row0 // seq_len
    blk = lambda s: (blk0 + s, 0)
    const3 = lambda s: (0, 0, 0)
    return pl.pallas_call(
        _band_sample_kernel,
        out_shape=jax.ShapeDtypeStruct(hbuf.shape, F32),
        grid=(n_seq,),
        in_specs=[
            pl.BlockSpec((seq_len, D_MODEL), blk),
            pl.BlockSpec((seq_len, width), blk),
            pl.BlockSpec((seq_len, width), blk),
            pl.BlockSpec((seq_len, width), blk),
            pl.BlockSpec((1, ATT_HEADS, ATT_DH, past), lambda s: (s, 0, 0, 0)),
            pl.BlockSpec((1, ATT_HEADS, ATT_DH, past), lambda s: (s, 0, 0, 0)),
            pl.BlockSpec((ATT_HEADS, seq_len, past), const3),
            pl.BlockSpec((ATT_HEADS, seq_len, seq_len), const3),
            pl.BlockSpec((width, D_MODEL), lambda s: (0, 0)),
            pl.BlockSpec(memory_space=pl.ANY),
        ],
        out_specs=pl.BlockSpec((seq_len, D_MODEL), blk),
        scratch_shapes=[pltpu.VMEM((seq_len, width), BF16)],
        input_output_aliases={9: 0},
        compiler_params=pltpu.CompilerParams(dimension_semantics=("arbitrary",), vmem_limit_bytes=VMEM_LIMIT),
        name="band_attn_sample",
    )(h, q, k, v, cache_k, cache_v, b1, b2, wout, hbuf)


def _band_bias_table(rb):
    n_top = BAND_PAST - REL_CLIP + 1
    n_var = REL_CLIP + CHUNK - 2
    n_bot = ATT_WIN - n_top - n_var
    n = ATT_WIN + ATT_QB
    heads = rb.shape[0]
    top = jnp.broadcast_to(rb[:, n_var + 1:n_var + 2], (heads, n_top))
    by_d = jnp.concatenate([top, rb[:, 1:n_var + 1][:, ::-1], jnp.broadcast_to(rb[:, 0:1], (heads, n_bot)),
                            jnp.broadcast_to(rb[:, n_var + 1:n_var + 2], (heads, ATT_QB))], axis=1)
    shifted = jnp.tile(by_d, (1, ATT_QB))[:, :ATT_QB * (n - 1)].reshape(heads, ATT_QB, n - 1)
    return shifted[:, :, :ATT_WIN]


def kernel(x_prompt, x_sample, state_gla, state_conv, cache_band_k, cache_band_v, norm_mix_g, norm_ffn_g, norm_final_g, w_in_even, w_alpha_up, b_alpha, gla_norm_g, conv_w, conv_b, conv_ln_g, conv_ln_b, w_out_even, w_qkv_odd, rel_bias, w_out_odd, w_router_grp, b_router_grp, w_router_exp, b_router_exp, w_exp_gate, w_exp_up, w_exp_down):
    batch, seq, _ = x_prompt.shape
    dec_batch, dec_seq, _ = x_sample.shape
    n_prompt = batch * seq
    n_sample = dec_batch * dec_seq
    assert seq % ROW_BLOCK == 0 and n_sample == ROW_BLOCK and seq % ATT_QB == 0
    n_rows = n_prompt + n_sample
    prompt_blocks = n_prompt // ROW_BLOCK
    n_blocks = n_rows // ROW_BLOCK
    width = ATT_HEADS * ATT_DH
    past = cache_band_k.shape[2]

    w_in = w_in_even[0]
    c0, c1, c2, c3, c4, c5 = (QK_W, 2 * QK_W, 2 * QK_W + V_W, 2 * QK_W + 2 * V_W,
                              2 * QK_W + 2 * V_W + GLA_LOWRANK, 2 * QK_W + 2 * V_W + GLA_LOWRANK + CONV_CH)
    w_in_p = jnp.concatenate(
        [w_in[:, :c3], w_in[:, c4:], w_in[:, c3:c4], jnp.zeros((D_MODEL, LANES - GLA_LOWRANK), F32)], axis=1).astype(BF16)
    wau_p = jnp.concatenate([w_alpha_up[0], jnp.zeros((LANES - GLA_LOWRANK, QK_W), F32)], axis=0).astype(BF16)
    even_w = (norm_mix_g[0].reshape(1, D_MODEL), w_in_p, wau_p, b_alpha[0].reshape(1, QK_W),
              gla_norm_g[0].reshape(1, V_W), conv_w[0], conv_b[0].reshape(1, CONV_CH),
              conv_ln_g[0].reshape(1, CONV_CH), conv_ln_b[0].reshape(1, CONV_CH), w_out_even[0].astype(BF16))
    h1, gla_p, conv_p = _even_mixer(
        x_prompt.reshape(n_prompt, D_MODEL), None, even_w,
        jnp.zeros((batch, GLA_HEADS, GLA_DK, GLA_DV), F32), jnp.zeros((batch, HIST, CONV_CH), F32),
        n_rows_total=n_rows, block0=0, n_seq=batch, blocks_per_seq=seq // ROW_BLOCK,
        chunk=CHUNK, cps=ROW_BLOCK // CHUNK, spb=1)
    h1, gla_s, conv_s = _even_mixer(
        x_sample.reshape(n_sample, D_MODEL), h1, even_w, state_gla[0], state_conv[0],
        n_rows_total=n_rows, block0=prompt_blocks, n_seq=1, blocks_per_seq=1,
        chunk=dec_seq, cps=1, spb=dec_batch)

    def moe(h, layer, gfin, splits):
        return _hier_moe(h, norm_ffn_g[layer], w_router_grp[layer], b_router_grp[layer], w_router_exp[layer],
                         b_router_exp[layer], w_exp_gate, w_exp_up, w_exp_down, layer, gfin, splits)

    (h2,) = moe(h1, 0, None, [(0, n_blocks)])

    q, k, v = _qkv(h2, norm_mix_g[1].reshape(1, D_MODEL), w_qkv_odd[0].astype(BF16), n_blocks)
    toe = _band_bias_table(rel_bias[0])
    r = jnp.arange(ATT_QB)[:, None]
    jb = jnp.arange(ATT_WIN)[None, :] - CHUNK * (r // CHUNK)
    in_band = jnp.logical_and(jb >= 0, jb < BAND_PAST + CHUNK)
    bm = jnp.where(in_band[None], toe, NEG_BIG)
    wout_odd = w_out_odd[0].astype(BF16)
    h3 = _band_prompt(h2, q, k, v, bm, wout_odd, n_rows_total=n_rows, n_seq=batch, blocks_per_seq=seq // ATT_QB)
    assert past == BAND_PAST and dec_seq <= CHUNK
    b1 = toe[:, :dec_seq, :past]
    b2 = toe[:, :dec_seq, past:past + dec_seq]
    h3 = _band_sample(h2, q, k, v, jnp.transpose(cache_band_k[0], (0, 2, 3, 1)),
                      jnp.transpose(cache_band_v[0], (0, 2, 3, 1)), b1, b2, wout_odd, h3,
                      row0=n_prompt, n_seq=dec_batch, seq_len=dec_seq)

    y_prompt, y_sample = moe(h3, 1, norm_final_g, [(0, prompt_blocks), (prompt_blocks, n_blocks - prompt_blocks)])

    rows = min(BAND_PAST, seq)
    tail = lambda a: jnp.stack([a[b * seq + seq - rows:(b + 1) * seq] for b in range(batch)]).reshape(
        batch, rows, ATT_HEADS, ATT_DH)
    kp, vp = tail(k), tail(v)
    ks = k[n_prompt:].reshape(dec_batch, dec_seq, ATT_HEADS, ATT_DH)
    vs = v[n_prompt:].reshape(dec_batch, dec_seq, ATT_HEADS, ATT_DH)
    return (y_prompt.reshape(batch, seq, D_MODEL), y_sample.reshape(dec_batch, dec_seq, D_MODEL),
            gla_p[None], gla_s[None], conv_p[None], conv_s[None], kp[None], vp[None], ks[None], vs[None])
```
